```python
import jax, jax.numpy as jnp
from jax import lax
import numpy as np

D_MODEL = 1024
BATCH = 4
SEQ = 4096
DEPTH = 4
DEC_BATCH = 32
DEC_SEQ = 8
PAST_LEN = 8192
PAGE_SIZE = 128

HEAD_DIM = 64
N_A_LAYERS = DEPTH // 2
N_B_LAYERS = DEPTH - N_A_LAYERS
MIX_WIDTH = 3 * D_MODEL // 4
RWKV_HEADS = MIX_WIDTH // HEAD_DIM
MEM_HEADS = 4
MEM_WIDTH = MEM_HEADS * HEAD_DIM
MEM_TOKENS = 256
DIL_GROUPS = ((128, 1), (512, 4), (2048, 16))
N_GROUPS = len(DIL_GROUPS)
GROUP_HEADS = MIX_WIDTH // (N_GROUPS * HEAD_DIM)
DIL_WIDTH = GROUP_HEADS * HEAD_DIM
D_FF = -(-8 * D_MODEL // (3 * 256)) * 256
LORA_W = 64
LORA_A = 64
LORA_V = 32
LORA_G = 128
ROPE_THETA = 10000.0
NORM_EPS = 1e-6
LNX_EPS = HEAD_DIM * 1e-5
Q_BLOCK = 128

kernel_name = 'rwkv7_dilated_yoco_decoder'


def rms_norm(x, g):
    xf = x.astype(jnp.float32)
    y = xf * lax.rsqrt(jnp.mean(xf * xf, axis=-1, keepdims=True) + NORM_EPS)
    return (y * g.astype(jnp.float32)).astype(x.dtype)


def rotary(x, pos):
    half = HEAD_DIM // 2
    inv = ROPE_THETA ** (-jnp.arange(half, dtype=jnp.float32) / half)
    ang = pos.astype(jnp.float32)[:, None] * inv[None, :]
    shp = (ang.shape[0],) + (1,) * (x.ndim - 3) + (half,)
    cos, sin = jnp.cos(ang).reshape(shp), jnp.sin(ang).reshape(shp)
    xf = x.astype(jnp.float32)
    x1, x2 = xf[..., :half], xf[..., half:]
    return jnp.concatenate([x1 * cos - x2 * sin, x2 * cos + x1 * sin], axis=-1).astype(x.dtype)


def mem_kv(mem, g, w):
    B, M, _ = mem.shape
    return (rms_norm(mem, g) @ w).reshape(B, M, 2, MEM_HEADS, HEAD_DIM)


def mem_attention(q, mk, mv):
    s = jnp.einsum('bthd,bmhd->bhtm', q, mk.astype(q.dtype)).astype(jnp.float32) * (HEAD_DIM ** -0.5)
    p = jax.nn.softmax(s, axis=-1).astype(mv.dtype)
    return jnp.einsum('bhtm,bmhd->bthd', p, mv)


def wkv7_scan(S0, r, w, k, v, kk, a):
    def step(S, inp):
        r_t, w_t, k_t, v_t, kk_t, a_t = inp
        sa = jnp.einsum('bhvk,bhk->bhv', S, -kk_t)
        S = (S * w_t[:, :, None, :] + sa[..., None] * (kk_t * a_t)[:, :, None, :]
             + v_t[..., None] * k_t[:, :, None, :])
        return S, jnp.einsum('bhvk,bhk->bhv', S, r_t)
    xs = tuple(jnp.moveaxis(t, 1, 0) for t in (r, w, k, v, kk, a))
    S, o = lax.scan(step, S0, xs)
    return S, jnp.moveaxis(o, 0, 1)


def rwkv7_layer(xn, shift0, S0, v_first, li, W):
    B, T, D = xn.shape
    f32 = jnp.float32
    x_prev = jnp.concatenate([shift0[:, None, :].astype(xn.dtype), xn[:, :-1]], axis=1)
    xx = x_prev - xn
    mu = W['a_mu'][li]
    xr, xw, xk, xv, xa, xg = [xn + xx * mu[i] for i in range(6)]
    w_in = W['a_w_in'][li]
    w_rkv = w_in[:, :3 * MIX_WIDTH].reshape(D, 3, MIX_WIDTH)
    rkv = jnp.einsum('sbtd,dsc->sbtc', jnp.stack([xr, xk, xv]), w_rkv)
    r, k, v = rkv[0], rkv[1], rkv[2]
    q_mem = xn @ w_in[:, 3 * MIX_WIDTH:]
    w_log = -jax.nn.softplus(-(W['a_w0'][li] + jnp.tanh(xw @ W['a_w1'][li]) @ W['a_w2'][li])) - 0.5
    decay = jnp.exp(-jnp.exp(w_log.astype(f32)))
    a = jax.nn.sigmoid(W['a_a0'][li] + (xa @ W['a_a1'][li]) @ W['a_a2'][li])
    if v_first is not None:
        vi = li - 1
        v = v + (v_first - v) * jax.nn.sigmoid(W['a_v0'][vi] + (xv @ W['a_v1'][vi]) @ W['a_v2'][vi])
    g = jax.nn.sigmoid(xg @ W['a_g1'][li]) @ W['a_g2'][li]
    heads = lambda t: t.astype(f32).reshape(B, T, RWKV_HEADS, HEAD_DIM)
    kk = heads(k * W['a_k_k'][li])
    kk = kk / jnp.maximum(jnp.sqrt(jnp.sum(kk * kk, axis=-1, keepdims=True)), 1e-12)
    k = k * (1.0 + (a - 1.0) * W['a_k_a'][li])
    rh, kh, vh, ah, wh = heads(r), heads(k), heads(v), heads(a), heads(decay)
    S, o = wkv7_scan(S0.astype(f32), rh, wh, kh, vh, kk, ah)
    mean = jnp.mean(o, axis=-1, keepdims=True)
    var = jnp.mean(jnp.square(o - mean), axis=-1, keepdims=True)
    o = ((o - mean) * lax.rsqrt(var + LNX_EPS)).reshape(B, T, MIX_WIDTH)
    o = o * W['a_lnx_w'][li].astype(f32) + W['a_lnx_b'][li].astype(f32)
    bonus = jnp.sum(rh * kh * W['a_r_k'][li].astype(f32), axis=-1, keepdims=True) * vh
    o = (o + bonus.reshape(B, T, MIX_WIDTH)) * g.astype(f32)
    return o.astype(xn.dtype), q_mem, S, v


def shared_kv(h, pos, g_kv, w_kv):
    B, T, _ = h.shape
    kv = (rms_norm(h, g_kv) @ w_kv).reshape(B, T, 2, N_GROUPS, GROUP_HEADS, HEAD_DIM)
    return rotary(kv[:, :, 0], pos), kv[:, :, 1]


def prompt_sources(k_sh, v_sh):
    T = k_sh.shape[1]
    srcs, bufs = [], []
    for g, (win, _) in enumerate(DIL_GROUPS):
        kg, vg = k_sh[:, :, g], v_sh[:, :, g]
        srcs.append((kg, vg, 0))
        L = min(win, PAST_LEN)
        kv = jnp.stack([kg, vg], axis=2)
        if T < L:
            kv = jnp.pad(kv, ((0, 0), (L - T, 0), (0, 0), (0, 0), (0, 0)))
        bufs.append(kv[:, -L:])
    return srcs, bufs


def sample_sources(k_sh, v_sh, win_bufs):
    srcs, bufs = [], []
    for g, buf in enumerate(win_bufs):
        L = buf.shape[1]
        ks = jnp.concatenate([buf[:, :, 0].astype(k_sh.dtype), k_sh[:, :, g]], axis=1)
        vs = jnp.concatenate([buf[:, :, 1].astype(v_sh.dtype), v_sh[:, :, g]], axis=1)
        srcs.append((ks, vs, L))
        bufs.append(jnp.stack([ks, vs], axis=2)[:, -L:])
    return srcs, bufs


def dilated_group_attn(q, ks, vs, idx, valid):
    kg = ks[:, idx]
    vg = vs[:, idx]
    s = jnp.einsum('bqhd,bqjhd->bqhj', q, kg).astype(jnp.float32) * (HEAD_DIM ** -0.5)
    s = jnp.where(valid[None, :, None, :], s, -jnp.inf)
    lse = jax.nn.logsumexp(s, axis=-1)
    p = jnp.exp(s - lse[..., None]).astype(vs.dtype)
    return jnp.einsum('bqhj,bqjhd->bqhd', p, vg), lse


def dilated_attention(q, srcs):
    B, T = q.shape[:2]
    qb = Q_BLOCK if T % Q_BLOCK == 0 else T

    def block(bi):
        t0 = bi * qb
        tq = t0 + jnp.arange(qb)
        q_blk = lax.dynamic_slice_in_dim(q, t0, qb, axis=1)
        outs, lses = [], []
        for g, ((win, dil), (ks, vs, base)) in enumerate(zip(DIL_GROUPS, srcs)):
            offs = dil * jnp.arange(win // dil + 1)
            idx = base + tq[:, None] - offs[None, :]
            o, l = dilated_group_attn(q_blk[:, :, g], ks, vs, jnp.maximum(idx, 0), idx >= 0)
            outs.append(o)
            lses.append(l)
        alpha = jax.nn.softmax(jnp.stack(lses, axis=0), axis=0)
        o = jnp.sum(alpha[..., None] * jnp.stack(outs, axis=0).astype(jnp.float32), axis=0)
        return o.astype(q.dtype)

    o = lax.map(block, jnp.arange(T // qb))
    return jnp.moveaxis(o, 0, 1).reshape(B, T, GROUP_HEADS, HEAD_DIM)


def run_trunk(h, pos, mem_kv_all, wkv0, shift0, win_bufs, W):
    B, T, _ = h.shape
    new_wkv, new_shift, new_bufs = [], [], []
    v_first = None
    srcs = None
    for l in range(DEPTH):
        xn = rms_norm(h, W['g_pre_mix'][l])
        mk, mv = mem_kv_all[l][:, :, 0], mem_kv_all[l][:, :, 1]
        if l < N_A_LAYERS:
            o_mix, q_mem, S, v = rwkv7_layer(xn, shift0[l], wkv0[l], v_first, l, W)
            if l == 0:
                v_first = v
            new_wkv.append(S)
            new_shift.append(xn[:, -1])
            w_out = W['a_w_out'][l]
        else:
            if l == N_A_LAYERS:
                k_sh, v_sh = shared_kv(h, pos, W['g_kv'], W['w_kv'])
                if win_bufs is None:
                    srcs, new_bufs = prompt_sources(k_sh, v_sh)
                else:
                    srcs, new_bufs = sample_sources(k_sh, v_sh, win_bufs)
            j = l - N_A_LAYERS
            proj = xn @ W['b_w_in'][j]
            q = rotary(proj[..., :MIX_WIDTH].reshape(B, T, N_GROUPS, GROUP_HEADS, HEAD_DIM), pos)
            q_mem = proj[..., MIX_WIDTH:]
            o_mix = dilated_attention(q, srcs).reshape(B, T, DIL_WIDTH)
            w_out = W['b_w_out'][j]
        o_mem = mem_attention(q_mem.reshape(B, T, MEM_HEADS, HEAD_DIM), mk, mv).reshape(B, T, MEM_WIDTH)
        mixed = jnp.concatenate([o_mix.astype(h.dtype), o_mem.astype(h.dtype)], axis=-1) @ w_out
        h = h + rms_norm(mixed, W['g_post_mix'][l])
        f = rms_norm(h, W['g_pre_ffn'][l]) @ W['w_ffn_in'][l]
        f = (jax.nn.silu(f[..., :D_FF]) * f[..., D_FF:]) @ W['w_ffn_out'][l]
        h = h + rms_norm(f, W['g_post_ffn'][l])
    return h, jnp.stack(new_wkv), jnp.stack(new_shift), new_bufs


def setup_inputs(seed: int = 0) -> dict:
    key = jax.random.key(seed)
    keys = iter(jax.random.split(key, 64))

    def nrm(shape, scale):
        return scale * jax.random.normal(next(keys), shape, jnp.float32)

    def gain(shape):
        return 1.0 + nrm(shape, 0.05)

    D = D_MODEL
    L0, L1, L2 = [min(w, PAST_LEN) for w, _ in DIL_GROUPS]
    NA, NB = N_A_LAYERS, N_B_LAYERS
    return {
        'x_prompt': nrm((BATCH, SEQ, D), 1.0),
        'x_sample': nrm((DEC_BATCH, DEC_SEQ, D), 1.0),
        'state_wkv': nrm((NA, DEC_BATCH, RWKV_HEADS, HEAD_DIM, HEAD_DIM), 0.1),
        'state_shift': nrm((NA, DEC_BATCH, D), 1.0),
        'cache_win_g0': nrm((DEC_BATCH, L0, 2, GROUP_HEADS, HEAD_DIM), 1.0),
        'cache_win_g1': nrm((DEC_BATCH, L1, 2, GROUP_HEADS, HEAD_DIM), 1.0),
        'cache_win_g2': nrm((DEC_BATCH, L2, 2, GROUP_HEADS, HEAD_DIM), 1.0),
        'cache_mem': nrm((DEPTH, DEC_BATCH, MEM_TOKENS, 2, MEM_HEADS, HEAD_DIM), 1.0),
        'mem_prompt': nrm((BATCH, MEM_TOKENS, D), 1.0),
        'g_pre_mix': gain((DEPTH, D)),
        'g_post_mix': gain((DEPTH, D)),
        'g_pre_ffn': gain((DEPTH, D)),
        'g_post_ffn': gain((DEPTH, D)),
        'g_mem': gain((DEPTH, D)),
        'w_mem_kv': nrm((DEPTH, D, 2 * MEM_WIDTH), D ** -0.5),
        'w_ffn_in': nrm((DEPTH, D, 2 * D_FF), D ** -0.5),
        'w_ffn_out': nrm((DEPTH, D_FF, D), D_FF ** -0.5),
        'a_mu': 0.5 + nrm((NA, 6, D), 0.2),
        'a_w_in': nrm((NA, D, 3 * MIX_WIDTH + MEM_WIDTH), D ** -0.5),
        'a_w0': -1.0 + nrm((NA, MIX_WIDTH), 1.0),
        'a_w1': nrm((NA, D, LORA_W), D ** -0.5),
        'a_w2': nrm((NA, LORA_W, MIX_WIDTH), 0.1 * LORA_W ** -0.5),
        'a_a0': nrm((NA, MIX_WIDTH), 0.1),
        'a_a1': nrm((NA, D, LORA_A), D ** -0.5),
        'a_a2': nrm((NA, LORA_A, MIX_WIDTH), 0.1 * LORA_A ** -0.5),
        'a_v0': nrm((NA - 1, MIX_WIDTH), 0.1),
        'a_v1': nrm((NA - 1, D, LORA_V), D ** -0.5),
        'a_v2': nrm((NA - 1, LORA_V, MIX_WIDTH), 0.1 * LORA_V ** -0.5),
        'a_g1': nrm((NA, D, LORA_G), D ** -0.5),
        'a_g2': nrm((NA, LORA_G, MIX_WIDTH), LORA_G ** -0.5),
        'a_k_k': 0.85 + nrm((NA, MIX_WIDTH), 0.05),
        'a_k_a': 1.0 + nrm((NA, MIX_WIDTH), 0.05),
        'a_r_k': nrm((NA, RWKV_HEADS, HEAD_DIM), 0.1),
        'a_lnx_w': gain((NA, MIX_WIDTH)),
        'a_lnx_b': nrm((NA, MIX_WIDTH), 0.02),
        'a_w_out': nrm((NA, MIX_WIDTH + MEM_WIDTH, D), (MIX_WIDTH + MEM_WIDTH) ** -0.5),
        'g_kv': gain((D,)),
        'w_kv': nrm((D, 2 * MIX_WIDTH), D ** -0.5),
        'b_w_in': nrm((NB, D, MIX_WIDTH + MEM_WIDTH), D ** -0.5),
        'b_w_out': nrm((NB, DIL_WIDTH + MEM_WIDTH, D), (DIL_WIDTH + MEM_WIDTH) ** -0.5),
    }


def reference(x_prompt, x_sample, state_wkv, state_shift, cache_win_g0, cache_win_g1, cache_win_g2,
              cache_mem, mem_prompt, g_pre_mix, g_post_mix, g_pre_ffn, g_post_ffn, g_mem, w_mem_kv,
              w_ffn_in, w_ffn_out, a_mu, a_w_in, a_w0, a_w1, a_w2, a_a0, a_a1, a_a2, a_v0, a_v1, a_v2,
              a_g1, a_g2, a_k_k, a_k_a, a_r_k, a_lnx_w, a_lnx_b, a_w_out, g_kv, w_kv, b_w_in, b_w_out):
    W = dict(g_pre_mix=g_pre_mix, g_post_mix=g_post_mix, g_pre_ffn=g_pre_ffn, g_post_ffn=g_post_ffn,
             w_ffn_in=w_ffn_in, w_ffn_out=w_ffn_out, a_mu=a_mu, a_w_in=a_w_in, a_w0=a_w0, a_w1=a_w1,
             a_w2=a_w2, a_a0=a_a0, a_a1=a_a1, a_a2=a_a2, a_v0=a_v0, a_v1=a_v1, a_v2=a_v2, a_g1=a_g1,
             a_g2=a_g2, a_k_k=a_k_k, a_k_a=a_k_a, a_r_k=a_r_k, a_lnx_w=a_lnx_w, a_lnx_b=a_lnx_b,
             a_w_out=a_w_out, g_kv=g_kv, w_kv=w_kv, b_w_in=b_w_in, b_w_out=b_w_out)
    Bp, Tp, _ = x_prompt.shape
    Ts = x_sample.shape[1]
    mem_kv_p = jnp.stack([mem_kv(mem_prompt, g_mem[l], w_mem_kv[l]) for l in range(DEPTH)])
    wkv_zero = jnp.zeros((N_A_LAYERS, Bp, RWKV_HEADS, HEAD_DIM, HEAD_DIM), jnp.float32)
    shift_zero = jnp.zeros((N_A_LAYERS, Bp, D_MODEL), x_prompt.dtype)
    pos_p = jnp.arange(Tp, dtype=jnp.int32)
    y_p, wkv_p, shift_p, bufs_p = run_trunk(x_prompt, pos_p, mem_kv_p, wkv_zero, shift_zero, None, W)
    pos_s = PAST_LEN + jnp.arange(Ts, dtype=jnp.int32)
    y_s, wkv_s, shift_s, bufs_s = run_trunk(x_sample, pos_s, cache_mem, state_wkv, state_shift,
                                            [cache_win_g0, cache_win_g1, cache_win_g2], W)
    return (y_p, y_s, wkv_p, shift_p, bufs_p[0], bufs_p[1], bufs_p[2], mem_kv_p,
            wkv_s, shift_s, bufs_s[0], bufs_s[1], bufs_s[2])
```

```python
import functools
import math

import jax
import jax.numpy as jnp
from jax import lax
from jax.experimental import pallas as pl
from jax.experimental.pallas import tpu as pltpu

F32 = jnp.float32
BF16 = jnp.bfloat16
HIGHEST = lax.Precision.HIGHEST

HEAD_DIM = 64
LANES = 128
COL_TILE = 256
HEADS_PER_TILE = COL_TILE // HEAD_DIM
MEM_HEADS = 4
MEM_WIDTH = MEM_HEADS * HEAD_DIM
DIL_GROUPS = ((128, 1), (512, 4), (2048, 16))
N_GROUPS = len(DIL_GROUPS)
PAST_LEN = 8192
ROPE_THETA = 10000.0
NORM_EPS = 1e-6
LNX_EPS = HEAD_DIM * 1e-5
KK_EPS = 1e-12
ATTN_SCALE = HEAD_DIM ** -0.5
BAND = 128
VMEM_LIMIT = 56 * 1024 * 1024


def _cparams(*sem):
    return pltpu.CompilerParams(dimension_semantics=sem, vmem_limit_bytes=VMEM_LIMIT)


def _rms(x, g):
    return x * lax.rsqrt(jnp.mean(x * x, axis=-1, keepdims=True) + NORM_EPS) * g


def _mm(a, b):
    return jnp.dot(a.astype(BF16), b.astype(BF16), preferred_element_type=F32)


def _mm_nt(a, b):
    return lax.dot_general(a.astype(BF16), b.astype(BF16), (((1,), (1,)), ((), ())),
                           preferred_element_type=F32)


def _mm_tn(a, b):
    return lax.dot_general(a.astype(BF16), b.astype(BF16), (((0,), (0,)), ((), ())),
                           preferred_element_type=F32)


def _sigmoid(x):
    return 1.0 / (1.0 + jnp.exp(-x))


def _head_lane_masks(width):
    lane = lax.broadcasted_iota(jnp.int32, (1, width), 1)
    return [(lane >= h * HEAD_DIM) & (lane < (h + 1) * HEAD_DIM) for h in range(width // HEAD_DIM)]


def _row_tile(n, want):
    t = min(n, want)
    assert n % t == 0, (n, t)
    return t


def _rotate_heads(y, cos, sin_signed):
    width = y.shape[1]
    cos = jnp.concatenate([cos] * (width // LANES), axis=1)
    sin_signed = jnp.concatenate([sin_signed] * (width // LANES), axis=1)
    lane = lax.broadcasted_iota(jnp.int32, y.shape, 1)
    first_half = (lane % HEAD_DIM) < (HEAD_DIM // 2)
    partner = jnp.where(first_half, pltpu.roll(y, width - HEAD_DIM // 2, axis=1),
                        pltpu.roll(y, HEAD_DIM // 2, axis=1))
    return y * cos + partner * sin_signed


def _norm_proj_kernel(x_ref, g_ref, w_ref, cos_ref, sin_ref, o_ref, xn_ref, *, n_rot, tile_major):
    j = pl.program_id(1)

    @pl.when(j == 0)
    def _():
        xn_ref[...] = _rms(x_ref[...], g_ref[...]).astype(BF16)

    y = jnp.dot(xn_ref[...], w_ref[...], preferred_element_type=F32)

    def put(val):
        if tile_major:
            o_ref[0] = val
        else:
            o_ref[...] = val

    if n_rot == 0:
        put(y)
    else:
        @pl.when(j < n_rot)
        def _():
            put(_rotate_heads(y, cos_ref[...], sin_ref[...]))

        @pl.when(j >= n_rot)
        def _():
            put(y)


def _norm_proj(x, g, w_bf16, cos, sin_signed, seq_len, *, n_rot, tile_major, tm_want=512):
    n, d = x.shape
    cols = w_bf16.shape[1]
    n_tiles = cols // COL_TILE
    tm = _row_tile(n, tm_want)
    if tm <= seq_len:
        assert seq_len % tm == 0
        per_seq = seq_len // tm
        tab_map = lambda i, j: (i % per_seq, 0)
    else:
        assert tm % seq_len == 0
        cos = jnp.tile(cos, (tm // seq_len, 1))
        sin_signed = jnp.tile(sin_signed, (tm // seq_len, 1))
        tab_map = lambda i, j: (0, 0)
    if tile_major:
        out_shape = jax.ShapeDtypeStruct((n_tiles, n, COL_TILE), F32)
        out_spec = pl.BlockSpec((1, tm, COL_TILE), lambda i, j: (j, i, 0))
    else:
        out_shape = jax.ShapeDtypeStruct((n, cols), F32)
        out_spec = pl.BlockSpec((tm, COL_TILE), lambda i, j: (i, j))
    return pl.pallas_call(
        functools.partial(_norm_proj_kernel, n_rot=n_rot, tile_major=tile_major),
        grid=(n // tm, n_tiles),
        in_specs=[
            pl.BlockSpec((tm, d), lambda i, j: (i, 0)),
            pl.BlockSpec((1, d), lambda i, j: (0, 0)),
            pl.BlockSpec((d, COL_TILE), lambda i, j: (0, j)),
            pl.BlockSpec((tm, LANES), tab_map),
            pl.BlockSpec((tm, LANES), tab_map),
        ],
        out_specs=out_spec,
        out_shape=out_shape,
        scratch_shapes=[pltpu.VMEM((tm, d), BF16)],
        compiler_params=_cparams("parallel", "arbitrary"),
        name="norm_proj",
    )(x, g.reshape(1, d), w_bf16, cos, sin_signed)


def _rwkv_proj_kernel(*refs, mix, has_vfirst, tm, seq_len, tail):
    it = iter(refs)
    h_ref = next(it)
    hp_ref = next(it) if seq_len >= tm else None
    sh_ref = next(it)
    g_ref, mu_ref, win_ref = next(it), next(it), next(it)
    w0_ref, w1_ref, w2_ref = next(it), next(it), next(it)
    a0_ref, a1_ref, a2_ref = next(it), next(it), next(it)
    g1_ref, g2_ref = next(it), next(it)
    if has_vfirst:
        vf_ref, v0_ref, v1_ref, v2_ref = next(it), next(it), next(it), next(it)
    r_ref, lw_ref, k_ref, v_ref, a_ref, gate_ref, qm_ref, tail_ref = (next(it) for _ in range(8))

    i = pl.program_id(0)
    gain = g_ref[...]
    xn = _rms(h_ref[...], gain)
    row = lax.broadcasted_iota(jnp.int32, (tm, 1), 0)
    rolled = pltpu.roll(xn, 1, axis=0)
    if seq_len >= tm:
        prev_tile_last = _rms(hp_ref[...], gain)[7:8, :]
        at_seq_start = (i * tm) % seq_len == 0
        first = jnp.where(at_seq_start, sh_ref[0][7:8, :], prev_tile_last)
        x_prev = jnp.where(row == 0, first, rolled)
    else:
        x_prev = jnp.where(row % seq_len == 0, sh_ref[...], rolled)
    xx = x_prev - xn
    mu = mu_ref[...]

    def mixed(idx):
        return (xn + xx * mu[idx:idx + 1, :]).astype(BF16)

    xr, xw, xk, xv, xa, xg = (mixed(idx) for idx in range(6))
    r_ref[...] = jnp.dot(xr, win_ref[:, 0:mix], preferred_element_type=F32)
    k_ref[...] = jnp.dot(xk, win_ref[:, mix:2 * mix], preferred_element_type=F32)
    v = jnp.dot(xv, win_ref[:, 2 * mix:3 * mix], preferred_element_type=F32)
    qm_ref[...] = jnp.dot(xn.astype(BF16), win_ref[:, 3 * mix:], preferred_element_type=F32)

    w_lin = w0_ref[...] + _mm(jnp.tanh(_mm(xw, w1_ref[...])), w2_ref[...])
    lw_ref[...] = -math.exp(-0.5) * _sigmoid(w_lin)
    a_ref[...] = _sigmoid(a0_ref[...] + _mm(_mm(xa, a1_ref[...]), a2_ref[...]))
    if has_vfirst:
        v = v + (vf_ref[...] - v) * _sigmoid(v0_ref[...] + _mm(_mm(xv, v1_ref[...]), v2_ref[...]))
    v_ref[...] = v
    gate_ref[...] = _mm(_sigmoid(_mm(xg, g1_ref[...])), g2_ref[...])
    tail_ref[0] = xn[tm - tail:, :]


def _rwkv_proj(h, shift, seq_len, W, li, v_first, *, tm_want=256):
    n, d = h.shape
    mix = W['a_w0'].shape[1]
    tm = _row_tile(n, tm_want)
    n_tiles = n // tm
    has_vfirst = v_first is not None
    row_spec = lambda w: pl.BlockSpec((tm, w), lambda i: (i, 0))
    full = lambda a: pl.BlockSpec(a.shape, lambda i: (0,) * a.ndim)

    args, specs = [h], [row_spec(d)]
    if seq_len >= tm:
        assert seq_len % tm == 0
        tail = 8
        args.append(h)
        specs.append(pl.BlockSpec((8, d), lambda i: (jnp.maximum(i * (tm // 8) - 1, 0), 0)))
        args.append(jnp.broadcast_to(shift[:, None, :], (shift.shape[0], 8, d)))
        specs.append(pl.BlockSpec((1, 8, d), lambda i: ((i * tm) // seq_len, 0, 0)))
    else:
        assert tm % seq_len == 0
        tail = tm
        args.append(jnp.repeat(shift, seq_len, axis=0))
        specs.append(row_spec(d))
    small = [W['g_pre_mix'][li].reshape(1, d), W['a_mu'][li], W['a_w_in_bf16'][li],
             W['a_w0'][li].reshape(1, mix), W['a_w1_bf16'][li], W['a_w2_bf16'][li],
             W['a_a0'][li].reshape(1, mix), W['a_a1_bf16'][li], W['a_a2_bf16'][li],
             W['a_g1_bf16'][li], W['a_g2_bf16'][li]]
    args += small
    specs += [full(a) for a in small]
    if has_vfirst:
        vi = li - 1
        extra = [W['a_v0'][vi].reshape(1, mix), W['a_v1_bf16'][vi], W['a_v2_bf16'][vi]]
        args += [v_first] + extra
        specs += [row_spec(mix)] + [full(a) for a in extra]

    wide = jax.ShapeDtypeStruct((n, mix), F32)
    out_shape = [wide] * 6 + [jax.ShapeDtypeStruct((n, MEM_WIDTH), F32),
                              jax.ShapeDtypeStruct((n_tiles, tail, d), F32)]
    out_specs = [row_spec(mix)] * 6 + [row_spec(MEM_WIDTH),
                                       pl.BlockSpec((1, tail, d), lambda i: (i, 0, 0))]
    r, lw, k, v, a, gate, q_mem, xn_tail = pl.pallas_call(
        functools.partial(_rwkv_proj_kernel, mix=mix, has_vfirst=has_vfirst, tm=tm,
                          seq_len=seq_len, tail=tail),
        grid=(n_tiles,),
        in_specs=specs,
        out_specs=out_specs,
        out_shape=out_shape,
        compiler_params=_cparams("parallel"),
        name="rwkv_proj",
    )(*args)
    xn_rows = xn_tail.reshape(n_tiles * tail, d)
    last = xn_rows.reshape(-1, seq_len if seq_len < tm else tail * (seq_len // tm), d)[:, -1]
    return r, lw, k, v, a, gate, q_mem, last


def _wkv_kernel(r_ref, lw_ref, k_ref, v_ref, a_ref, gate_ref, s0_ref, kkp_ref, kap_ref, rkp_ref,
                lnw_ref, lnb_ref, o_ref, sout_ref, s_scr, *, chunk, heads):
    c = pl.program_id(1)

    @pl.when(c == 0)
    def _():
        s_scr[...] = s0_ref[0]

    r, lw, k, v, a = r_ref[0], lw_ref[0], k_ref[0], v_ref[0], a_ref[0]
    row = lax.broadcasted_iota(jnp.int32, (chunk, chunk), 0)
    col = lax.broadcasted_iota(jnp.int32, (chunk, chunk), 1)
    strict = col < row
    incl = col <= row
    c_inc = jnp.dot(incl.astype(F32), lw, precision=HIGHEST, preferred_element_type=F32)
    c_exc = c_inc - lw
    c_last = c_inc[chunk - 1:chunk, :]
    e_inc = jnp.exp(c_inc)
    e_exc = jnp.exp(c_exc)
    e_neg = jnp.exp(-c_inc)
    e_tail = jnp.exp(c_last - c_inc)
    p_last = jnp.exp(c_last)

    kk_raw = k * kkp_ref[...]
    k2 = k * (1.0 + (a - 1.0) * kap_ref[...])
    rk = r * k2 * rkp_ref[...]
    r_hat = r * e_inc
    k_chk = k2 * e_neg
    k_til = k2 * e_tail

    eye_c = (row == col).astype(F32)
    hr = lax.broadcasted_iota(jnp.int32, (HEAD_DIM, HEAD_DIM), 0)
    hc = lax.broadcasted_iota(jnp.int32, (HEAD_DIM, HEAD_DIM), 1)
    eye_h = hr == hc
    n_levels = max(1, int(math.log2(chunk)))

    normed, bonus = [], []
    for h in range(heads):
        sl = slice(h * HEAD_DIM, (h + 1) * HEAD_DIM)
        kk = kk_raw[:, sl]
        kk = kk / jnp.maximum(jnp.sqrt(jnp.sum(kk * kk, axis=-1, keepdims=True)), KK_EPS)
        kka = kk * a[:, sl]
        a_hat = -kk * e_exc[:, sl]
        b_chk = kka * e_neg[:, sl]
        b_til = kka * e_tail[:, sl]
        v_h = v[:, sl]

        lhs = jnp.concatenate([a_hat, r_hat[:, sl]], axis=0)
        rhs = jnp.concatenate([b_chk, k_chk[:, sl]], axis=0)
        amat = _mm_nt(lhs, rhs)
        a_ab = jnp.where(strict, amat[:chunk, :chunk], 0.0)
        a_ak = jnp.where(strict, amat[:chunk, chunk:], 0.0)
        a_rb = jnp.where(incl, amat[chunk:, :chunk], 0.0)
        a_rk = jnp.where(incl, amat[chunk:, chunk:], 0.0)

        power = a_ab
        inv = eye_c + a_ab
        for _ in range(n_levels - 1):
            power = jnp.dot(power, power, precision=HIGHEST, preferred_element_type=F32)
            inv = inv + jnp.dot(inv, power, precision=HIGHEST, preferred_element_type=F32)

        av = _mm(jnp.concatenate([a_ak, a_rk], axis=0), v_h)
        w12 = jnp.dot(inv, jnp.concatenate([a_hat, av[:chunk]], axis=1),
                      precision=HIGHEST, preferred_element_type=F32)
        rx = _mm(a_rb, w12) + jnp.concatenate([r_hat[:, sl], av[chunk:]], axis=1)
        gh = _mm_tn(w12, b_til)
        g_t = gh[:HEAD_DIM] + jnp.where(eye_h, p_last[:, sl], 0.0)
        h_t = gh[HEAD_DIM:] + _mm_tn(v_h, k_til[:, sl])

        s_prev = s_scr[h]
        o_h = lax.dot_general(rx[:, :HEAD_DIM], s_prev, (((1,), (1,)), ((), ())),
                              precision=HIGHEST, preferred_element_type=F32) + rx[:, HEAD_DIM:]
        s_scr[h] = jnp.dot(s_prev, g_t, precision=HIGHEST, preferred_element_type=F32) + h_t

        mean = jnp.mean(o_h, axis=-1, keepdims=True)
        cen = o_h - mean
        var = jnp.mean(cen * cen, axis=-1, keepdims=True)
        normed.append(cen * lax.rsqrt(var + LNX_EPS))
        bonus.append(jnp.sum(rk[:, sl], axis=-1, keepdims=True) * v_h)

    o = jnp.concatenate(normed, axis=1) * lnw_ref[...] + lnb_ref[...]
    o_ref[0] = (o + jnp.concatenate(bonus, axis=1)) * gate_ref[0]

    @pl.when(c == pl.num_programs(1) - 1)
    def _():
        sout_ref[0] = s_scr[...]


def _wkv(r, lw, k, v, a, gate, s0, W, li, batch, seq_len, chunk):
    mix = r.shape[1]
    heads = mix // HEAD_DIM
    if seq_len % chunk:
        raise ValueError("sequence length must be a multiple of the chunk")
    n_chunks = seq_len // chunk
    seq = lambda t: t.reshape(batch, seq_len, mix)
    blk = pl.BlockSpec((1, chunk, mix), lambda b, c: (b, c, 0))
    st = pl.BlockSpec((1, heads, HEAD_DIM, HEAD_DIM), lambda b, c: (b, 0, 0, 0))
    par = pl.BlockSpec((1, mix), lambda b, c: (0, 0))
    params = [W['a_k_k'][li], W['a_k_a'][li], W['a_r_k'][li], W['a_lnx_w'][li], W['a_lnx_b'][li]]
    o, s_out = pl.pallas_call(
        functools.partial(_wkv_kernel, chunk=chunk, heads=heads),
        grid=(batch, n_chunks),
        in_specs=[blk] * 6 + [st] + [par] * 5,
        out_specs=[blk, st],
        out_shape=[jax.ShapeDtypeStruct((batch, seq_len, mix), F32),
                   jax.ShapeDtypeStruct(s0.shape, F32)],
        scratch_shapes=[pltpu.VMEM((heads, HEAD_DIM, HEAD_DIM), F32)],
        compiler_params=_cparams("parallel", "arbitrary"),
        name="wkv7",
    )(seq(r), seq(lw), seq(k), seq(v), seq(a), seq(gate), s0, *[p.reshape(1, mix) for p in params])
    return o.reshape(batch * seq_len, mix), s_out


def _mem_attn_kernel(q_ref, kv_ref, o_ref):
    q = q_ref[0]
    mk = kv_ref[0, :, :MEM_WIDTH].astype(BF16)
    mv = kv_ref[0, :, MEM_WIDTH:].astype(BF16)
    out = jnp.zeros(q.shape, F32)
    for mask in _head_lane_masks(MEM_WIDTH):
        s = _mm_nt(jnp.where(mask, q, 0.0), mk) * ATTN_SCALE
        p = jnp.exp(s - jnp.max(s, axis=-1, keepdims=True))
        p = p / jnp.sum(p, axis=-1, keepdims=True)
        out = out + jnp.where(mask, _mm(p, mv), 0.0)
    o_ref[0] = out


def _mem_attn(q_mem, mem_kv, batch, seq_len, *, tq_want=512):
    tq = _row_tile(seq_len, tq_want)
    m = mem_kv.shape[1]
    out = pl.pallas_call(
        _mem_attn_kernel,
        grid=(batch, seq_len // tq),
        in_specs=[pl.BlockSpec((1, tq, MEM_WIDTH), lambda b, t: (b, t, 0)),
                  pl.BlockSpec((1, m, 2 * MEM_WIDTH), lambda b, t: (b, 0, 0))],
        out_specs=pl.BlockSpec((1, tq, MEM_WIDTH), lambda b, t: (b, t, 0)),
        out_shape=jax.ShapeDtypeStruct((batch, seq_len, MEM_WIDTH), F32),
        compiler_params=_cparams("parallel", "parallel"),
        name="mem_attn",
    )(q_mem.reshape(batch, seq_len, MEM_WIDTH), mem_kv)
    return out.reshape(batch * seq_len, MEM_WIDTH)


def _out_proj_kernel(*refs, merge_groups, mix_width):
    if merge_groups:
        og_ref, lse_ref, om_ref, w_ref, g_ref, h_ref, o_ref = refs
        lse = lse_ref[...]
        top = jnp.max(lse, axis=0)
        wgt = jnp.exp(lse - top[None])
        o_mix = jnp.sum(wgt * og_ref[...], axis=0) / jnp.sum(wgt, axis=0)
    else:
        omix_ref, om_ref, w_ref, g_ref, h_ref, o_ref = refs
        o_mix = omix_ref[...]
    mixed = (jnp.dot(o_mix.astype(BF16), w_ref[:mix_width, :], preferred_element_type=F32)
             + jnp.dot(om_ref[...].astype(BF16), w_ref[mix_width:, :], preferred_element_type=F32))
    o_ref[...] = h_ref[...] + _rms(mixed, g_ref[...])


def _out_proj(o_mix, lse, o_mem, w_bf16, g, h, *, tm_want=512):
    n, d = h.shape
    tm = _row_tile(n, tm_want)
    merge = lse is not None
    mix_width = w_bf16.shape[0] - MEM_WIDTH
    row = lambda w: pl.BlockSpec((tm, w), lambda i: (i, 0))
    if merge:
        grp = pl.BlockSpec((N_GROUPS, tm, COL_TILE), lambda i: (0, i, 0))
        lead, lead_specs = [o_mix, lse], [grp, grp]
    else:
        lead, lead_specs = [o_mix], [row(mix_width)]
    return pl.pallas_call(
        functools.partial(_out_proj_kernel, merge_groups=merge, mix_width=mix_width),
        grid=(n // tm,),
        in_specs=lead_specs + [row(MEM_WIDTH), pl.BlockSpec(w_bf16.shape, lambda i: (0, 0)),
                               pl.BlockSpec((1, d), lambda i: (0, 0)), row(d)],
        out_specs=row(d),
        out_shape=jax.ShapeDtypeStruct((n, d), F32),
        compiler_params=_cparams("parallel"),
        name="out_proj",
    )(*lead, o_mem, w_bf16, g.reshape(1, d), h)


def _ffn_kernel(x_ref, gpre_ref, wg_ref, wu_ref, wo_ref, gpost_ref, o_ref, xn_ref, acc_ref):
    j = pl.program_id(1)

    @pl.when(j == 0)
    def _():
        xn_ref[...] = _rms(x_ref[...], gpre_ref[...]).astype(BF16)
        acc_ref[...] = jnp.zeros_like(acc_ref)

    xn = xn_ref[...]
    gate = jnp.dot(xn, wg_ref[...], preferred_element_type=F32)
    up = jnp.dot(xn, wu_ref[...], preferred_element_type=F32)
    act = (gate * _sigmoid(gate) * up).astype(BF16)
    acc_ref[...] += jnp.dot(act, wo_ref[...], preferred_element_type=F32)

    @pl.when(j == pl.num_programs(1) - 1)
    def _():
        o_ref[...] = x_ref[...] + _rms(acc_ref[...], gpost_ref[...])


def _ffn(h, g_pre, w_in_bf16, w_out_bf16, g_post, *, tm_want=1024, tf=256):
    n, d = h.shape
    d_ff = w_out_bf16.shape[0]
    assert d_ff % tf == 0
    n_ff = d_ff // tf
    tm = _row_tile(n, tm_want)
    return pl.pallas_call(
        _ffn_kernel,
        grid=(n // tm, n_ff),
        in_specs=[pl.BlockSpec((tm, d), lambda i, j: (i, 0)),
                  pl.BlockSpec((1, d), lambda i, j: (0, 0)),
                  pl.BlockSpec((d, tf), lambda i, j: (0, j)),
                  pl.BlockSpec((d, tf), lambda i, j: (0, n_ff + j)),
                  pl.BlockSpec((tf, d), lambda i, j: (j, 0)),
                  pl.BlockSpec((1, d), lambda i, j: (0, 0))],
        out_specs=pl.BlockSpec((tm, d), lambda i, j: (i, 0)),
        out_shape=jax.ShapeDtypeStruct((n, d), F32),
        scratch_shapes=[pltpu.VMEM((tm, d), BF16), pltpu.VMEM((tm, d), F32)],
        compiler_params=_cparams("parallel", "arbitrary"),
        name="ffn",
    )(h, g_pre.reshape(1, d), w_in_bf16, w_in_bf16, w_out_bf16, g_post.reshape(1, d))


def _dil_prompt_kernel(q_ref, kc_ref, kh_ref, vc_ref, vh_ref, o_ref, lse_ref, kx_ref, vx_ref, *, tile):
    t = pl.program_id(2)
    kx_ref[0:BAND, :] = kh_ref[0].astype(BF16)
    kx_ref[BAND:, :] = kc_ref[0].astype(BF16)
    vx_ref[0:BAND, :] = vh_ref[0].astype(BF16)
    vx_ref[BAND:, :] = vc_ref[0].astype(BF16)

    qi = lax.broadcasted_iota(jnp.int32, (BAND, 2 * BAND), 0)
    kj = lax.broadcasted_iota(jnp.int32, (BAND, 2 * BAND), 1)
    band = (kj >= qi) & (kj <= qi + BAND)
    masks = _head_lane_masks(COL_TILE)

    def block(blk, carry):
        start = pl.multiple_of(blk * BAND, BAND)
        q = q_ref[0, pl.ds(start, BAND), :] * ATTN_SCALE
        keys = kx_ref[pl.ds(start, 2 * BAND), :]
        vals = vx_ref[pl.ds(start, 2 * BAND), :]
        lo = jnp.where((t == 0) & (blk == 0), BAND, 0)
        valid = band & (kj >= lo)
        out = jnp.zeros((BAND, COL_TILE), F32)
        lse = jnp.zeros((BAND, COL_TILE), F32)
        for mask in masks:
            s = _mm_nt(jnp.where(mask, q, 0.0), keys)
            s = jnp.where(valid, s, -jnp.inf)
            top = jnp.max(s, axis=-1, keepdims=True)
            p = jnp.exp(s - top)
            den = jnp.sum(p, axis=-1, keepdims=True)
            out = out + jnp.where(mask, _mm(p, vals) / den, 0.0)
            lse = lse + jnp.where(mask, top + jnp.log(den), 0.0)
        o_ref[0, pl.ds(start, BAND), :] = out
        lse_ref[0, pl.ds(start, BAND), :] = lse
        return carry

    lax.fori_loop(0, tile // BAND, block, 0)


def _dil_prompt_group(q, k, v, batch, seq_len, dil):
    cls_len = seq_len // dil
    tile = min(cls_len, 1024)
    assert cls_len % tile == 0 and tile % BAND == 0
    n_t = cls_len // tile
    view = lambda x: x.reshape(batch, cls_len, dil * COL_TILE)
    cur = pl.BlockSpec((1, tile, COL_TILE), lambda b, c, t: (b, t, c))
    halo = pl.BlockSpec((1, BAND, COL_TILE),
                        lambda b, c, t: (b, jnp.maximum(t * (tile // BAND) - 1, 0), c))
    out_shape = jax.ShapeDtypeStruct((batch, cls_len, dil * COL_TILE), F32)
    o, lse = pl.pallas_call(
        functools.partial(_dil_prompt_kernel, tile=tile),
        grid=(batch, dil, n_t),
        in_specs=[cur, cur, halo, cur, halo],
        out_specs=[cur, cur],
        out_shape=[out_shape, out_shape],
        scratch_shapes=[pltpu.VMEM((tile + BAND, COL_TILE), BF16)] * 2,
        compiler_params=_cparams("parallel", "parallel", "arbitrary"),
        name="dil_attn_prompt",
    )(view(q), view(k), view(k), view(v), view(v))
    return o.reshape(batch * seq_len, COL_TILE), lse.reshape(batch * seq_len, COL_TILE)


def _dil_sample_kernel(*refs, seq_len, write_cache):
    q_ref, kv_ref = refs[0], refs[1]
    cache_refs = refs[2:2 + N_GROUPS]
    o_ref = refs[2 + N_GROUPS]
    new_refs = refs[3 + N_GROUPS:]
    masks = _head_lane_masks(COL_TILE)
    rows = HEADS_PER_TILE * seq_len

    run_top = jnp.full((rows, 1), -jnp.inf, F32)
    run_den = jnp.zeros((rows, 1), F32)
    run_acc = jnp.zeros((rows, COL_TILE), F32)
    for g, (win, dil) in enumerate(DIL_GROUPS):
        cache = cache_refs[g]
        length = cache.shape[1]
        q = q_ref[g] * ATTN_SCALE
        stacked = jnp.concatenate([jnp.where(m, q, 0.0) for m in masks], axis=0)
        k_new, v_new = kv_ref[g], kv_ref[N_GROUPS + g]
        ck = cache[0, :, :COL_TILE]
        cv = cache[0, :, COL_TILE:]
        s_c = _mm_nt(stacked, ck)
        s_n = _mm_nt(stacked, k_new)
        tq_c = lax.broadcasted_iota(jnp.int32, s_c.shape, 0) % seq_len
        dist_c = length + tq_c - lax.broadcasted_iota(jnp.int32, s_c.shape, 1)
        ok_c = (dist_c % dil == 0) & (dist_c <= win)
        tq_n = lax.broadcasted_iota(jnp.int32, s_n.shape, 0) % seq_len
        dist_n = tq_n - lax.broadcasted_iota(jnp.int32, s_n.shape, 1)
        ok_n = (dist_n >= 0) & (dist_n % dil == 0)
        s_c = jnp.where(ok_c, s_c, -jnp.inf)
        s_n = jnp.where(ok_n, s_n, -jnp.inf)
        top = jnp.maximum(jnp.maximum(jnp.max(s_c, axis=-1, keepdims=True),
                                      jnp.max(s_n, axis=-1, keepdims=True)), run_top)
        p_c = jnp.exp(s_c - top)
        p_n = jnp.exp(s_n - top)
        scale = jnp.exp(run_top - top)
        run_den = run_den * scale + jnp.sum(p_c, axis=-1, keepdims=True) + jnp.sum(p_n, axis=-1, keepdims=True)
        run_acc = run_acc * scale + _mm(p_c, cv) + _mm(p_n, v_new)
        run_top = top
        if write_cache:
            new = new_refs[g]
            if length > seq_len:
                new[0, 0:length - seq_len, :] = cache[0, seq_len:, :]
            new[0, length - seq_len:, :COL_TILE] = k_new
            new[0, length - seq_len:, COL_TILE:] = v_new
    res = run_acc / run_den
    out = jnp.zeros((seq_len, COL_TILE), F32)
    for h, m in enumerate(masks):
        out = out + jnp.where(m, res[h * seq_len:(h + 1) * seq_len, :], 0.0)
    o_ref[...] = out


def _dil_sample(q_tiles, kv_tiles, caches, batch, seq_len, write_cache):
    n = batch * seq_len
    in_specs = [pl.BlockSpec((N_GROUPS, seq_len, COL_TILE), lambda b: (0, b, 0)),
                pl.BlockSpec((2 * N_GROUPS, seq_len, COL_TILE), lambda b: (0, b, 0))]
    cache_specs = [pl.BlockSpec((1,) + c.shape[1:], lambda b: (b, 0, 0)) for c in caches]
    out_shape = [jax.ShapeDtypeStruct((n, COL_TILE), F32)]
    out_specs = [pl.BlockSpec((seq_len, COL_TILE), lambda b: (b, 0))]
    if write_cache:
        out_shape += [jax.ShapeDtypeStruct(c.shape, F32) for c in caches]
        out_specs += cache_specs
    res = pl.pallas_call(
        functools.partial(_dil_sample_kernel, seq_len=seq_len, write_cache=write_cache),
        grid=(batch,),
        in_specs=in_specs + cache_specs,
        out_specs=out_specs,
        out_shape=out_shape,
        compiler_params=_cparams("parallel"),
        name="dil_attn_sample",
    )(q_tiles, kv_tiles, *caches)
    return res[0], list(res[1:])


def _rope_tables(pos):
    half = HEAD_DIM // 2
    inv = ROPE_THETA ** (-jnp.arange(half, dtype=F32) / half)
    ang = pos.astype(F32)[:, None] * inv[None, :]
    cos, sin = jnp.cos(ang), jnp.sin(ang)
    cos = jnp.tile(jnp.concatenate([cos, cos], axis=-1), (1, LANES // HEAD_DIM))
    sin_signed = jnp.tile(jnp.concatenate([-sin, sin], axis=-1), (1, LANES // HEAD_DIM))
    return cos, sin_signed


def _trunk(h, pos, mem_kv_all, wkv0, shift0, caches, W, batch, seq_len, chunk):
    depth = W['g_pre_mix'].shape[0]
    n_a = W['a_mu'].shape[0]
    cos, sin_signed = _rope_tables(pos)
    new_wkv, new_shift, new_caches = [], [], None
    v_first = None
    kv_tiles = None
    for l in range(depth):
        if l < n_a:
            r, lw, k, v, a, gate, q_mem, xn_last = _rwkv_proj(h, shift0[l], seq_len, W, l, v_first)
            if l == 0:
                v_first = v
            o_mix, s_out = _wkv(r, lw, k, v, a, gate, wkv0[l], W, l, batch, seq_len, chunk)
            new_wkv.append(s_out)
            new_shift.append(xn_last)
            lse = None
            w_out = W['a_w_out_bf16'][l]
        else:
            j = l - n_a
            if kv_tiles is None:
                kv_tiles = _norm_proj(h, W['g_kv'], W['w_kv_bf16'], cos, sin_signed, seq_len,
                                      n_rot=N_GROUPS, tile_major=True)
            proj = _norm_proj(h, W['g_pre_mix'][l], W['b_w_in_bf16'][j], cos, sin_signed, seq_len,
                              n_rot=N_GROUPS, tile_major=True)
            q_mem = proj[N_GROUPS]
            if caches is None:
                outs = [_dil_prompt_group(proj[g], kv_tiles[g], kv_tiles[N_GROUPS + g], batch, seq_len, dil)
                        for g, (_, dil) in enumerate(DIL_GROUPS)]
                o_mix = jnp.stack([o for o, _ in outs])
                lse = jnp.stack([s for _, s in outs])
            else:
                o_mix, written = _dil_sample(proj, kv_tiles, caches, batch, seq_len, new_caches is None)
                if new_caches is None:
                    new_caches = written
                lse = None
            w_out = W['b_w_out_bf16'][j]
        o_mem = _mem_attn(q_mem, mem_kv_all[l], batch, seq_len)
        h = _out_proj(o_mix, lse, o_mem, w_out, W['g_post_mix'][l], h)
        h = _ffn(h, W['g_pre_ffn'][l], W['w_ffn_in_bf16'][l], W['w_ffn_out_bf16'][l], W['g_post_ffn'][l])
    return h, jnp.stack(new_wkv), jnp.stack(new_shift), kv_tiles, new_caches


def _prompt_windows(kv_tiles, batch, seq_len):
    bufs = []
    for g, (win, _) in enumerate(DIL_GROUPS):
        length = min(win, PAST_LEN)
        assert seq_len >= length
        heads = lambda t: t.reshape(batch, seq_len, HEADS_PER_TILE, HEAD_DIM)[:, seq_len - length:]
        bufs.append(jnp.stack([heads(kv_tiles[g]), heads(kv_tiles[N_GROUPS + g])], axis=2))
    return bufs


def kernel(x_prompt, x_sample, state_wkv, state_shift, cache_win_g0, cache_win_g1, cache_win_g2, cache_mem, mem_prompt, g_pre_mix, g_post_mix, g_pre_ffn, g_post_ffn, g_mem, w_mem_kv, w_ffn_in, w_ffn_out, a_mu, a_w_in, a_w0, a_w1, a_w2, a_a0, a_a1, a_a2, a_v0, a_v1, a_v2, a_g1, a_g2, a_k_k, a_k_a, a_r_k, a_lnx_w, a_lnx_b, a_w_out, g_kv, w_kv, b_w_in, b_w_out):
    W = dict(g_pre_mix=g_pre_mix, g_post_mix=g_post_mix, g_pre_ffn=g_pre_ffn, g_post_ffn=g_post_ffn,
             a_mu=a_mu, a_w0=a_w0, a_a0=a_a0, a_v0=a_v0, a_k_k=a_k_k, a_k_a=a_k_a,
             a_r_k=a_r_k.reshape(a_r_k.shape[0], -1), a_lnx_w=a_lnx_w, a_lnx_b=a_lnx_b, g_kv=g_kv)
    for name, w in dict(w_ffn_in=w_ffn_in, w_ffn_out=w_ffn_out, a_w_in=a_w_in, a_w1=a_w1, a_w2=a_w2,
                        a_a1=a_a1, a_a2=a_a2, a_v1=a_v1, a_v2=a_v2, a_g1=a_g1, a_g2=a_g2,
                        a_w_out=a_w_out, w_kv=w_kv, b_w_in=b_w_in, b_w_out=b_w_out,
                        w_mem_kv=w_mem_kv).items():
        W[name + '_bf16'] = w.astype(BF16)

    bp, tp, d = x_prompt.shape
    bs, ts, _ = x_sample.shape
    depth = g_pre_mix.shape[0]
    n_a = a_mu.shape[0]
    heads = a_w0.shape[1] // HEAD_DIM
    m_tok = mem_prompt.shape[1]

    mem_rows = mem_prompt.reshape(bp * m_tok, d)
    no_tab = jnp.zeros((m_tok, LANES), F32)
    mem_kv_p = jnp.stack([
        _norm_proj(mem_rows, g_mem[l], W['w_mem_kv_bf16'][l], no_tab, no_tab, m_tok,
                   n_rot=0, tile_major=False).reshape(bp, m_tok, 2 * MEM_WIDTH)
        for l in range(depth)])
    wkv_zero = jnp.zeros((n_a, bp, heads, HEAD_DIM, HEAD_DIM), F32)
    shift_zero = jnp.zeros((n_a, bp, d), F32)
    y_p, wkv_p, shift_p, kv_p, _ = _trunk(
        x_prompt.reshape(bp * tp, d), jnp.arange(tp, dtype=jnp.int32), mem_kv_p, wkv_zero, shift_zero,
        None, W, bp, tp, chunk=64)
    bufs_p = _prompt_windows(kv_p, bp, tp)

    caches = [c.reshape(bs, c.shape[1], 2 * COL_TILE) for c in (cache_win_g0, cache_win_g1, cache_win_g2)]
    mem_kv_s = cache_mem.reshape(depth, bs, m_tok, 2 * MEM_WIDTH)
    y_s, wkv_s, shift_s, _, bufs_s = _trunk(
        x_sample.reshape(bs * ts, d), PAST_LEN + jnp.arange(ts, dtype=jnp.int32), mem_kv_s, state_wkv,
        state_shift, caches, W, bs, ts, chunk=ts)
    bufs_s = [b.reshape(c.shape) for b, c in zip(bufs_s, (cache_win_g0, cache_win_g1, cache_win_g2))]

    return (y_p.reshape(bp, tp, d), y_s.reshape(bs, ts, d), wkv_p, shift_p, bufs_p[0], bufs_p[1], bufs_p[2],
            mem_kv_p.reshape(depth, bp, m_tok, 2, MEM_HEADS, HEAD_DIM),
            wkv_s, shift_s, bufs_s[0], bufs_s[1], bufs_s[2])
```

```python
import functools
import math

import jax
import jax.numpy as jnp
from jax import lax
from jax.experimental import pallas as pl
from jax.experimental.pallas import tpu as pltpu

F32 = jnp.float32
BF16 = jnp.bfloat16
HIGHEST = lax.Precision.HIGHEST

HEAD_DIM = 64
LANES = 128
COL_TILE = 256
HEADS_PER_TILE = COL_TILE // HEAD_DIM
MEM_HEADS = 4
MEM_WIDTH = MEM_HEADS * HEAD_DIM
DIL_GROUPS = ((128, 1), (512, 4), (2048, 16))
N_GROUPS = len(DIL_GROUPS)
PAST_LEN = 8192
ROPE_THETA = 10000.0
NORM_EPS = 1e-6
LNX_EPS = HEAD_DIM * 1e-5
KK_EPS = 1e-12
ATTN_SCALE = HEAD_DIM ** -0.5
BAND = 128
VMEM_LIMIT = 56 * 1024 * 1024


def _cparams(*sem):
    return pltpu.CompilerParams(dimension_semantics=sem, vmem_limit_bytes=VMEM_LIMIT)


def _rms(x, g):
    return x * lax.rsqrt(jnp.mean(x * x, axis=-1, keepdims=True) + NORM_EPS) * g


def _mm(a, b):
    return jnp.dot(a.astype(BF16), b.astype(BF16), preferred_element_type=F32)


def _mm_nt(a, b):
    return lax.dot_general(a.astype(BF16), b.astype(BF16), (((1,), (1,)), ((), ())),
                           preferred_element_type=F32)


def _mm_tn(a, b):
    return lax.dot_general(a.astype(BF16), b.astype(BF16), (((0,), (0,)), ((), ())),
                           preferred_element_type=F32)


def _sigmoid(x):
    return 1.0 / (1.0 + jnp.exp(-x))


def _head_lane_masks(width):
    lane = lax.broadcasted_iota(jnp.int32, (1, width), 1)
    return [(lane >= h * HEAD_DIM) & (lane < (h + 1) * HEAD_DIM) for h in range(width // HEAD_DIM)]


def _row_tile(n, want):
    t = min(n, want)
    assert n % t == 0, (n, t)
    return t


def _rotate_heads(y, cos, sin_signed):
    width = y.shape[1]
    cos = jnp.concatenate([cos] * (width // LANES), axis=1)
    sin_signed = jnp.concatenate([sin_signed] * (width // LANES), axis=1)
    lane = lax.broadcasted_iota(jnp.int32, y.shape, 1)
    first_half = (lane % HEAD_DIM) < (HEAD_DIM // 2)
    partner = jnp.where(first_half, pltpu.roll(y, width - HEAD_DIM // 2, axis=1),
                        pltpu.roll(y, HEAD_DIM // 2, axis=1))
    return y * cos + partner * sin_signed


def _norm_proj_kernel(x_ref, g_ref, w_ref, cos_ref, sin_ref, *o_refs, n_rot):
    xn = _rms(x_ref[...], g_ref[...]).astype(BF16)
    for j, o_ref in enumerate(o_refs):
        width = o_ref.shape[1]
        y = jnp.dot(xn, w_ref[:, j * width:(j + 1) * width], preferred_element_type=F32)
        o_ref[...] = _rotate_heads(y, cos_ref[...], sin_ref[...]) if j < n_rot else y


def _resident(shape):
    return pl.BlockSpec(shape, lambda *_: (0,) * len(shape), pipeline_mode=pl.Buffered(1))


def _norm_proj(x, g, w_bf16, cos, sin_signed, seq_len, *, n_rot, out_width=COL_TILE, tm_want=512):
    n, d = x.shape
    cols = w_bf16.shape[1]
    n_out = cols // out_width
    tm = _row_tile(n, tm_want)
    if tm <= seq_len:
        assert seq_len % tm == 0
        per_seq = seq_len // tm
        tab_map = lambda i: (i % per_seq, 0)
    else:
        assert tm % seq_len == 0
        cos = jnp.tile(cos, (tm // seq_len, 1))
        sin_signed = jnp.tile(sin_signed, (tm // seq_len, 1))
        tab_map = lambda i: (0, 0)
    return pl.pallas_call(
        functools.partial(_norm_proj_kernel, n_rot=n_rot),
        grid=(n // tm,),
        in_specs=[
            pl.BlockSpec((tm, d), lambda i: (i, 0)),
            _resident((1, d)),
            _resident(w_bf16.shape),
            pl.BlockSpec((tm, LANES), tab_map),
            pl.BlockSpec((tm, LANES), tab_map),
        ],
        out_specs=[pl.BlockSpec((tm, out_width), lambda i: (i, 0))] * n_out,
        out_shape=[jax.ShapeDtypeStruct((n, out_width), F32)] * n_out,
        compiler_params=_cparams("parallel"),
        name="norm_proj",
    )(x, g.reshape(1, d), w_bf16, cos, sin_signed)


def _rwkv_proj_kernel(*refs, mix, has_vfirst, tm, seq_len, tail):
    it = iter(refs)
    h_ref = next(it)
    hp_ref = next(it) if seq_len >= tm else None
    sh_ref = next(it)
    g_ref, mu_ref, win_ref = next(it), next(it), next(it)
    w0_ref, w1_ref, w2_ref = next(it), next(it), next(it)
    a0_ref, a1_ref, a2_ref = next(it), next(it), next(it)
    g1_ref, g2_ref = next(it), next(it)
    if has_vfirst:
        vf_ref, v0_ref, v1_ref, v2_ref = next(it), next(it), next(it), next(it)
    r_ref, lw_ref, k_ref, v_ref, a_ref, gate_ref, qm_ref, tail_ref = (next(it) for _ in range(8))

    i = pl.program_id(0)
    gain = g_ref[...]
    xn = _rms(h_ref[...], gain)
    row = lax.broadcasted_iota(jnp.int32, (tm, 1), 0)
    rolled = pltpu.roll(xn, 1, axis=0)
    if seq_len >= tm:
        prev_tile_last = _rms(hp_ref[...], gain)[7:8, :]
        at_seq_start = (i * tm) % seq_len == 0
        first = jnp.where(at_seq_start, sh_ref[0][7:8, :], prev_tile_last)
        x_prev = jnp.where(row == 0, first, rolled)
    else:
        x_prev = jnp.where(row % seq_len == 0, sh_ref[...], rolled)
    xx = x_prev - xn
    mu = mu_ref[...]

    def mixed(idx):
        return (xn + xx * mu[idx:idx + 1, :]).astype(BF16)

    xr, xw, xk, xv, xa, xg = (mixed(idx) for idx in range(6))
    r_ref[...] = jnp.dot(xr, win_ref[:, 0:mix], preferred_element_type=F32)
    k_ref[...] = jnp.dot(xk, win_ref[:, mix:2 * mix], preferred_element_type=F32)
    v = jnp.dot(xv, win_ref[:, 2 * mix:3 * mix], preferred_element_type=F32)
    qm_ref[...] = jnp.dot(xn.astype(BF16), win_ref[:, 3 * mix:], preferred_element_type=F32)

    w_lin = w0_ref[...] + _mm(jnp.tanh(_mm(xw, w1_ref[...])), w2_ref[...])
    lw_ref[...] = -math.exp(-0.5) * _sigmoid(w_lin)
    a_ref[...] = _sigmoid(a0_ref[...] + _mm(_mm(xa, a1_ref[...]), a2_ref[...]))
    if has_vfirst:
        v = v + (vf_ref[...] - v) * _sigmoid(v0_ref[...] + _mm(_mm(xv, v1_ref[...]), v2_ref[...]))
    v_ref[...] = v
    gate_ref[...] = _mm(_sigmoid(_mm(xg, g1_ref[...])), g2_ref[...])
    tail_ref[0] = xn[tm - tail:, :]


def _rwkv_proj(h, shift, seq_len, W, li, v_first, *, tm_want=512):
    n, d = h.shape
    mix = W['a_w0'].shape[1]
    tm = _row_tile(n, tm_want)
    n_tiles = n // tm
    has_vfirst = v_first is not None
    row_spec = lambda w: pl.BlockSpec((tm, w), lambda i: (i, 0))
    full = lambda a: _resident(a.shape)

    args, specs = [h], [row_spec(d)]
    if seq_len >= tm:
        assert seq_len % tm == 0
        tail = 8
        args.append(h)
        specs.append(pl.BlockSpec((8, d), lambda i: (jnp.maximum(i * (tm // 8) - 1, 0), 0)))
        args.append(jnp.broadcast_to(shift[:, None, :], (shift.shape[0], 8, d)))
        specs.append(pl.BlockSpec((1, 8, d), lambda i: ((i * tm) // seq_len, 0, 0)))
    else:
        assert tm % seq_len == 0
        tail = tm
        args.append(jnp.repeat(shift, seq_len, axis=0))
        specs.append(row_spec(d))
    small = [W['g_pre_mix'][li].reshape(1, d), W['a_mu'][li], W['a_w_in_bf16'][li],
             W['a_w0'][li].reshape(1, mix), W['a_w1_bf16'][li], W['a_w2_bf16'][li],
             W['a_a0'][li].reshape(1, mix), W['a_a1_bf16'][li], W['a_a2_bf16'][li],
             W['a_g1_bf16'][li], W['a_g2_bf16'][li]]
    args += small
    specs += [full(a) for a in small]
    if has_vfirst:
        vi = li - 1
        extra = [W['a_v0'][vi].reshape(1, mix), W['a_v1_bf16'][vi], W['a_v2_bf16'][vi]]
        args += [v_first] + extra
        specs += [row_spec(mix)] + [full(a) for a in extra]

    wide = jax.ShapeDtypeStruct((n, mix), F32)
    out_shape = [wide] * 6 + [jax.ShapeDtypeStruct((n, MEM_WIDTH), F32),
                              jax.ShapeDtypeStruct((n_tiles, tail, d), F32)]
    out_specs = [row_spec(mix)] * 6 + [row_spec(MEM_WIDTH),
                                       pl.BlockSpec((1, tail, d), lambda i: (i, 0, 0))]
    r, lw, k, v, a, gate, q_mem, xn_tail = pl.pallas_call(
        functools.partial(_rwkv_proj_kernel, mix=mix, has_vfirst=has_vfirst, tm=tm,
                          seq_len=seq_len, tail=tail),
        grid=(n_tiles,),
        in_specs=specs,
        out_specs=out_specs,
        out_shape=out_shape,
        compiler_params=_cparams("parallel"),
        name="rwkv_proj",
    )(*args)
    xn_rows = xn_tail.reshape(n_tiles * tail, d)
    last = xn_rows.reshape(-1, seq_len if seq_len < tm else tail * (seq_len // tm), d)[:, -1]
    return r, lw, k, v, a, gate, q_mem, last


def _split(x):
    hi = x.astype(BF16)
    return hi, (x - hi.astype(F32)).astype(BF16)


_NN = (((1,), (0,)), ((), ()))
_NT = (((1,), (1,)), ((), ()))


def _dot3(a, b, dims=_NN):
    ah, al = _split(a)
    bh, bl = _split(b)
    dg = functools.partial(lax.dot_general, dimension_numbers=dims, preferred_element_type=F32)
    return dg(ah, bh) + dg(ah, bl) + dg(al, bh)


def _wkv_kernel(r_ref, lw_ref, k_ref, v_ref, a_ref, gate_ref, s0_ref, kkp_ref, kap_ref, rkp_ref,
                lnw_ref, lnb_ref, o_ref, sout_ref, s_scr, *, chunk, heads):
    c = pl.program_id(1)

    @pl.when(c == 0)
    def _():
        s_scr[...] = s0_ref[0]

    r, lw, k, v, a = r_ref[0], lw_ref[0], k_ref[0], v_ref[0], a_ref[0]
    row = lax.broadcasted_iota(jnp.int32, (chunk, chunk), 0)
    col = lax.broadcasted_iota(jnp.int32, (chunk, chunk), 1)
    strict = col < row
    incl = col <= row
    tri = incl.astype(BF16)
    lw_hi = lw.astype(BF16)
    lw_rest = lw - lw_hi.astype(F32)
    lw_mid = lw_rest.astype(BF16)
    lw_lo = (lw_rest - lw_mid.astype(F32)).astype(BF16)
    c_inc = (jnp.dot(tri, lw_hi, preferred_element_type=F32)
             + jnp.dot(tri, lw_mid, preferred_element_type=F32)
             + jnp.dot(tri, lw_lo, preferred_element_type=F32))
    c_exc = c_inc - lw
    c_last = c_inc[chunk - 1:chunk, :]
    e_inc = jnp.exp(c_inc)
    e_exc = jnp.exp(c_exc)
    e_neg = jnp.exp(-c_inc)
    e_tail = jnp.exp(c_last - c_inc)
    p_last = jnp.exp(c_last)

    kk_raw = k * kkp_ref[...]
    k2 = k * (1.0 + (a - 1.0) * kap_ref[...])
    rk = r * k2 * rkp_ref[...]
    r_hat = r * e_inc
    k_chk = k2 * e_neg
    k_til = k2 * e_tail

    hr = lax.broadcasted_iota(jnp.int32, (HEAD_DIM, HEAD_DIM), 0)
    hc = lax.broadcasted_iota(jnp.int32, (HEAD_DIM, HEAD_DIM), 1)
    eye_h = hr == hc
    n_levels = max(1, int(math.log2(chunk)))
    hs = range(heads)
    sl = [slice(h * HEAD_DIM, (h + 1) * HEAD_DIM) for h in hs]

    kk = [kk_raw[:, s] for s in sl]
    kk = [x / jnp.maximum(jnp.sqrt(jnp.sum(x * x, axis=-1, keepdims=True)), KK_EPS) for x in kk]
    kka = [kk[h] * a[:, sl[h]] for h in hs]
    a_hat = [-kk[h] * e_exc[:, sl[h]] for h in hs]
    b_chk = [kka[h] * e_neg[:, sl[h]] for h in hs]
    b_til = [kka[h] * e_tail[:, sl[h]] for h in hs]
    v_h = [v[:, s] for s in sl]

    amat = [_mm_nt(jnp.concatenate([a_hat[h], r_hat[:, sl[h]]], axis=0),
                   jnp.concatenate([b_chk[h], k_chk[:, sl[h]]], axis=0)) for h in hs]
    a_ab = [jnp.where(strict, m[:chunk, :chunk], 0.0) for m in amat]
    a_ak = [jnp.where(strict, m[:chunk, chunk:], 0.0) for m in amat]
    a_rb = [jnp.where(incl, m[chunk:, :chunk], 0.0) for m in amat]
    a_rk = [jnp.where(incl, m[chunk:, chunk:], 0.0) for m in amat]
    av = [_mm(jnp.concatenate([a_ak[h], a_rk[h]], axis=0), v_h[h]) for h in hs]

    w12 = [jnp.concatenate([a_hat[h], av[h][:chunk]], axis=1) for h in hs]
    power = a_ab
    for level in range(n_levels):
        w12 = [w12[h] + _dot3(power[h], w12[h]) for h in hs]
        if level + 1 < n_levels:
            power = [_dot3(p, p) for p in power]

    rx = [_mm(a_rb[h], w12[h]) + jnp.concatenate([r_hat[:, sl[h]], av[h][chunk:]], axis=1) for h in hs]
    gh = [_mm_tn(w12[h], b_til[h]) for h in hs]
    vk = [_mm_tn(v_h[h], k_til[:, sl[h]]) for h in hs]
    g_t = [gh[h][:HEAD_DIM] + jnp.where(eye_h, p_last[:, sl[h]], 0.0) for h in hs]
    h_t = [gh[h][HEAD_DIM:] + vk[h] for h in hs]

    s_prev = [s_scr[h] for h in hs]
    o_h = [_dot3(rx[h][:, :HEAD_DIM], s_prev[h], _NT) + rx[h][:, HEAD_DIM:] for h in hs]
    s_new = [_dot3(s_prev[h], g_t[h]) + h_t[h] for h in hs]
    for h in hs:
        s_scr[h] = s_new[h]

    normed, bonus = [], []
    for h in hs:
        mean = jnp.mean(o_h[h], axis=-1, keepdims=True)
        cen = o_h[h] - mean
        var = jnp.mean(cen * cen, axis=-1, keepdims=True)
        normed.append(cen * lax.rsqrt(var + LNX_EPS))
        bonus.append(jnp.sum(rk[:, sl[h]], axis=-1, keepdims=True) * v_h[h])

    o = jnp.concatenate(normed, axis=1) * lnw_ref[...] + lnb_ref[...]
    o_ref[0] = (o + jnp.concatenate(bonus, axis=1)) * gate_ref[0]

    @pl.when(c == pl.num_programs(1) - 1)
    def _():
        sout_ref[0] = s_scr[...]


def _wkv(r, lw, k, v, a, gate, s0, W, li, batch, seq_len, chunk):
    mix = r.shape[1]
    heads = mix // HEAD_DIM
    if seq_len % chunk:
        raise ValueError("sequence length must be a multiple of the chunk")
    n_chunks = seq_len // chunk
    seq = lambda t: t.reshape(batch, seq_len, mix)
    blk = pl.BlockSpec((1, chunk, mix), lambda b, c: (b, c, 0))
    st = pl.BlockSpec((1, heads, HEAD_DIM, HEAD_DIM), lambda b, c: (b, 0, 0, 0))
    par = pl.BlockSpec((1, mix), lambda b, c: (0, 0))
    params = [W['a_k_k'][li], W['a_k_a'][li], W['a_r_k'][li], W['a_lnx_w'][li], W['a_lnx_b'][li]]
    o, s_out = pl.pallas_call(
        functools.partial(_wkv_kernel, chunk=chunk, heads=heads),
        grid=(batch, n_chunks),
        in_specs=[blk] * 6 + [st] + [par] * 5,
        out_specs=[blk, st],
        out_shape=[jax.ShapeDtypeStruct((batch, seq_len, mix), F32),
                   jax.ShapeDtypeStruct(s0.shape, F32)],
        scratch_shapes=[pltpu.VMEM((heads, HEAD_DIM, HEAD_DIM), F32)],
        compiler_params=_cparams("parallel", "arbitrary"),
        name="wkv7",
    )(seq(r), seq(lw), seq(k), seq(v), seq(a), seq(gate), s0, *[p.reshape(1, mix) for p in params])
    return o.reshape(batch * seq_len, mix), s_out


def _mem_attn_kernel(q_ref, kv_ref, o_ref):
    q = q_ref[0]
    mk = kv_ref[0, :, :MEM_WIDTH].astype(BF16)
    mv = kv_ref[0, :, MEM_WIDTH:].astype(BF16)
    out = jnp.zeros(q.shape, F32)
    for mask in _head_lane_masks(MEM_WIDTH):
        s = _mm_nt(jnp.where(mask, q, 0.0), mk) * ATTN_SCALE
        p = jnp.exp(s - jnp.max(s, axis=-1, keepdims=True))
        p = p / jnp.sum(p, axis=-1, keepdims=True)
        out = out + jnp.where(mask, _mm(p, mv), 0.0)
    o_ref[0] = out


def _mem_attn(q_mem, mem_kv, batch, seq_len, *, tq_want=512):
    tq = _row_tile(seq_len, tq_want)
    m = mem_kv.shape[1]
    out = pl.pallas_call(
        _mem_attn_kernel,
        grid=(batch, seq_len // tq),
        in_specs=[pl.BlockSpec((1, tq, MEM_WIDTH), lambda b, t: (b, t, 0)),
                  pl.BlockSpec((1, m, 2 * MEM_WIDTH), lambda b, t: (b, 0, 0))],
        out_specs=pl.BlockSpec((1, tq, MEM_WIDTH), lambda b, t: (b, t, 0)),
        out_shape=jax.ShapeDtypeStruct((batch, seq_len, MEM_WIDTH), F32),
        compiler_params=_cparams("parallel", "parallel"),
        name="mem_attn",
    )(q_mem.reshape(batch, seq_len, MEM_WIDTH), mem_kv)
    return out.reshape(batch * seq_len, MEM_WIDTH)


def _out_ffn_kernel(*refs, n_groups, mix_width, d_ff, tf):
    it = iter(refs)
    if n_groups:
        og = [next(it)[...] for _ in range(n_groups)]
        lse = [next(it)[...] for _ in range(n_groups)]
        top = functools.reduce(jnp.maximum, lse)
        wgt = [jnp.exp(s - top) for s in lse]
        o_mix = sum(w * o for w, o in zip(wgt, og)) / sum(wgt)
    else:
        o_mix = next(it)[...]
    om_ref, wout_ref, gpm_ref, h_ref, gpre_ref, win_ref, wo_ref, gpost_ref, o_ref, acc_ref = it
    mixed = (jnp.dot(o_mix.astype(BF16), wout_ref[:mix_width, :], preferred_element_type=F32)
             + jnp.dot(om_ref[...].astype(BF16), wout_ref[mix_width:, :], preferred_element_type=F32))
    h1 = h_ref[...] + _rms(mixed, gpm_ref[...])
    xn = _rms(h1, gpre_ref[...]).astype(BF16)
    for j in range(d_ff // tf):
        gate = jnp.dot(xn, win_ref[:, j * tf:(j + 1) * tf], preferred_element_type=F32)
        up = jnp.dot(xn, win_ref[:, d_ff + j * tf:d_ff + (j + 1) * tf], preferred_element_type=F32)
        act = (gate * _sigmoid(gate) * up).astype(BF16)
        part = jnp.dot(act, wo_ref[j * tf:(j + 1) * tf, :], preferred_element_type=F32)
        if j == 0:
            acc_ref[...] = part
        else:
            acc_ref[...] += part
    o_ref[...] = h1 + _rms(acc_ref[...], gpost_ref[...])


def _out_ffn(o_mix, lse, o_mem, h, W, l, w_out_bf16, *, tm_want=512, tf=256):
    n, d = h.shape
    tm = _row_tile(n, tm_want)
    w_in, w_o = W['w_ffn_in_bf16'][l], W['w_ffn_out_bf16'][l]
    d_ff = w_o.shape[0]
    assert d_ff % tf == 0
    mix_width = w_out_bf16.shape[0] - MEM_WIDTH
    row = lambda w: pl.BlockSpec((tm, w), lambda i: (i, 0))
    if lse is not None:
        lead = list(o_mix) + list(lse)
        n_groups = len(lse)
    else:
        lead, n_groups = [o_mix], 0
    vec = lambda g: g.reshape(1, d)
    return pl.pallas_call(
        functools.partial(_out_ffn_kernel, n_groups=n_groups, mix_width=mix_width, d_ff=d_ff, tf=tf),
        grid=(n // tm,),
        in_specs=[row(x.shape[1]) for x in lead] + [
            row(MEM_WIDTH), _resident(w_out_bf16.shape), _resident((1, d)), row(d),
            _resident((1, d)), _resident(w_in.shape), _resident(w_o.shape), _resident((1, d))],
        out_specs=row(d),
        out_shape=jax.ShapeDtypeStruct((n, d), F32),
        scratch_shapes=[pltpu.VMEM((tm, d), F32)],
        compiler_params=_cparams("parallel"),
        name="out_ffn",
    )(*lead, o_mem, w_out_bf16, vec(W['g_post_mix'][l]), h, vec(W['g_pre_ffn'][l]), w_in, w_o,
      vec(W['g_post_ffn'][l]))


def _dil_prompt_kernel(q_ref, kc_ref, kh_ref, vc_ref, vh_ref, o_ref, lse_ref, kx_ref, vx_ref, *, tile):
    t = pl.program_id(2)
    kx_ref[0:BAND, :] = kh_ref[0].astype(BF16)
    kx_ref[BAND:, :] = kc_ref[0].astype(BF16)
    vx_ref[0:BAND, :] = vh_ref[0].astype(BF16)
    vx_ref[BAND:, :] = vc_ref[0].astype(BF16)

    qi = lax.broadcasted_iota(jnp.int32, (BAND, 2 * BAND), 0)
    kj = lax.broadcasted_iota(jnp.int32, (BAND, 2 * BAND), 1)
    band = (kj >= qi) & (kj <= qi + BAND)
    masks = _head_lane_masks(COL_TILE)

    def block(blk, carry):
        start = pl.multiple_of(blk * BAND, BAND)
        q = q_ref[0, pl.ds(start, BAND), :] * ATTN_SCALE
        keys = kx_ref[pl.ds(start, 2 * BAND), :]
        vals = vx_ref[pl.ds(start, 2 * BAND), :]
        lo = jnp.where((t == 0) & (blk == 0), BAND, 0)
        valid = band & (kj >= lo)
        out = jnp.zeros((BAND, COL_TILE), F32)
        lse = jnp.zeros((BAND, COL_TILE), F32)
        for mask in masks:
            s = _mm_nt(jnp.where(mask, q, 0.0), keys)
            s = jnp.where(valid, s, -jnp.inf)
            top = jnp.max(s, axis=-1, keepdims=True)
            p = jnp.exp(s - top)
            den = jnp.sum(p, axis=-1, keepdims=True)
            out = out + jnp.where(mask, _mm(p, vals) / den, 0.0)
            lse = lse + jnp.where(mask, top + jnp.log(den), 0.0)
        o_ref[0, pl.ds(start, BAND), :] = out
        lse_ref[0, pl.ds(start, BAND), :] = lse
        return carry

    lax.fori_loop(0, tile // BAND, block, 0)


def _dil_prompt_group(q, k, v, batch, seq_len, dil):
    cls_len = seq_len // dil
    tile = min(cls_len, 1024)
    assert cls_len % tile == 0 and tile % BAND == 0
    n_t = cls_len // tile
    view = lambda x: x.reshape(batch, cls_len, dil * COL_TILE)
    cur = pl.BlockSpec((1, tile, COL_TILE), lambda b, c, t: (b, t, c))
    halo = pl.BlockSpec((1, BAND, COL_TILE),
                        lambda b, c, t: (b, jnp.maximum(t * (tile // BAND) - 1, 0), c))
    out_shape = jax.ShapeDtypeStruct((batch, cls_len, dil * COL_TILE), F32)
    o, lse = pl.pallas_call(
        functools.partial(_dil_prompt_kernel, tile=tile),
        grid=(batch, dil, n_t),
        in_specs=[cur, cur, halo, cur, halo],
        out_specs=[cur, cur],
        out_shape=[out_shape, out_shape],
        scratch_shapes=[pltpu.VMEM((tile + BAND, COL_TILE), BF16)] * 2,
        compiler_params=_cparams("parallel", "parallel", "arbitrary"),
        name="dil_attn_prompt",
    )(view(q), view(k), view(k), view(v), view(v))
    return o.reshape(batch * seq_len, COL_TILE), lse.reshape(batch * seq_len, COL_TILE)


def _dil_sample_kernel(*refs, seq_len, write_cache):
    q_refs = refs[:N_GROUPS]
    k_refs = refs[N_GROUPS:2 * N_GROUPS]
    v_refs = refs[2 * N_GROUPS:3 * N_GROUPS]
    cache_refs = refs[3 * N_GROUPS:4 * N_GROUPS]
    o_ref = refs[4 * N_GROUPS]
    new_refs = refs[4 * N_GROUPS + 1:]
    masks = _head_lane_masks(COL_TILE)
    rows = HEADS_PER_TILE * seq_len

    run_top = jnp.full((rows, 1), -jnp.inf, F32)
    run_den = jnp.zeros((rows, 1), F32)
    run_acc = jnp.zeros((rows, COL_TILE), F32)
    for g, (win, dil) in enumerate(DIL_GROUPS):
        cache = cache_refs[g]
        length = cache.shape[1]
        q = q_refs[g][...] * ATTN_SCALE
        stacked = jnp.concatenate([jnp.where(m, q, 0.0) for m in masks], axis=0)
        k_new, v_new = k_refs[g][...], v_refs[g][...]
        ck = cache[0, :, :COL_TILE]
        cv = cache[0, :, COL_TILE:]
        s_c = _mm_nt(stacked, ck)
        s_n = _mm_nt(stacked, k_new)
        tq_c = lax.broadcasted_iota(jnp.int32, s_c.shape, 0) % seq_len
        dist_c = length + tq_c - lax.broadcasted_iota(jnp.int32, s_c.shape, 1)
        ok_c = (dist_c % dil == 0) & (dist_c <= win)
        tq_n = lax.broadcasted_iota(jnp.int32, s_n.shape, 0) % seq_len
        dist_n = tq_n - lax.broadcasted_iota(jnp.int32, s_n.shape, 1)
        ok_n = (dist_n >= 0) & (dist_n % dil == 0)
        s_c = jnp.where(ok_c, s_c, -jnp.inf)
        s_n = jnp.where(ok_n, s_n, -jnp.inf)
        top = jnp.maximum(jnp.maximum(jnp.max(s_c, axis=-1, keepdims=True),
                                      jnp.max(s_n, axis=-1, keepdims=True)), run_top)
        p_c = jnp.exp(s_c - top)
        p_n = jnp.exp(s_n - top)
        scale = jnp.exp(run_top - top)
        run_den = run_den * scale + jnp.sum(p_c, axis=-1, keepdims=True) + jnp.sum(p_n, axis=-1, keepdims=True)
        run_acc = run_acc * scale + _mm(p_c, cv) + _mm(p_n, v_new)
        run_top = top
        if write_cache:
            new = new_refs[g]
            if length > seq_len:
                new[0, 0:length - seq_len, :] = cache[0, seq_len:, :]
            new[0, length - seq_len:, :COL_TILE] = k_new
            new[0, length - seq_len:, COL_TILE:] = v_new
    res = run_acc / run_den
    out = jnp.zeros((seq_len, COL_TILE), F32)
    for h, m in enumerate(masks):
        out = out + jnp.where(m, res[h * seq_len:(h + 1) * seq_len, :], 0.0)
    o_ref[...] = out


def _dil_sample(q_slabs, kv_slabs, caches, batch, seq_len, write_cache):
    n = batch * seq_len
    in_specs = [pl.BlockSpec((seq_len, COL_TILE), lambda b: (b, 0))] * (3 * N_GROUPS)
    cache_specs = [pl.BlockSpec((1,) + c.shape[1:], lambda b: (b, 0, 0)) for c in caches]
    out_shape = [jax.ShapeDtypeStruct((n, COL_TILE), F32)]
    out_specs = [pl.BlockSpec((seq_len, COL_TILE), lambda b: (b, 0))]
    if write_cache:
        out_shape += [jax.ShapeDtypeStruct(c.shape, F32) for c in caches]
        out_specs += cache_specs
    res = pl.pallas_call(
        functools.partial(_dil_sample_kernel, seq_len=seq_len, write_cache=write_cache),
        grid=(batch,),
        in_specs=in_specs + cache_specs,
        out_specs=out_specs,
        out_shape=out_shape,
        compiler_params=_cparams("parallel"),
        name="dil_attn_sample",
    )(*q_slabs[:N_GROUPS], *kv_slabs, *caches)
    return res[0], list(res[1:])


def _rope_tables(pos):
    half = HEAD_DIM // 2
    inv = ROPE_THETA ** (-jnp.arange(half, dtype=F32) / half)
    ang = pos.astype(F32)[:, None] * inv[None, :]
    cos, sin = jnp.cos(ang), jnp.sin(ang)
    cos = jnp.tile(jnp.concatenate([cos, cos], axis=-1), (1, LANES // HEAD_DIM))
    sin_signed = jnp.tile(jnp.concatenate([-sin, sin], axis=-1), (1, LANES // HEAD_DIM))
    return cos, sin_signed


def _trunk(h, pos, mem_kv_all, wkv0, shift0, caches, W, batch, seq_len, chunk):
    depth = W['g_pre_mix'].shape[0]
    n_a = W['a_mu'].shape[0]
    cos, sin_signed = _rope_tables(pos)
    new_wkv, new_shift, new_caches = [], [], None
    v_first = None
    kv_tiles = None
    for l in range(depth):
        if l < n_a:
            r, lw, k, v, a, gate, q_mem, xn_last = _rwkv_proj(h, shift0[l], seq_len, W, l, v_first)
            if l == 0:
                v_first = v
            o_mix, s_out = _wkv(r, lw, k, v, a, gate, wkv0[l], W, l, batch, seq_len, chunk)
            new_wkv.append(s_out)
            new_shift.append(xn_last)
            lse = None
            w_out = W['a_w_out_bf16'][l]
        else:
            j = l - n_a
            if kv_tiles is None:
                kv_tiles = _norm_proj(h, W['g_kv'], W['w_kv_bf16'], cos, sin_signed, seq_len, n_rot=N_GROUPS)
            proj = _norm_proj(h, W['g_pre_mix'][l], W['b_w_in_bf16'][j], cos, sin_signed, seq_len,
                              n_rot=N_GROUPS)
            q_mem = proj[N_GROUPS]
            if caches is None:
                outs = [_dil_prompt_group(proj[g], kv_tiles[g], kv_tiles[N_GROUPS + g], batch, seq_len, dil)
                        for g, (_, dil) in enumerate(DIL_GROUPS)]
                o_mix = [o for o, _ in outs]
                lse = [s for _, s in outs]
            else:
                o_mix, written = _dil_sample(proj, kv_tiles, caches, batch, seq_len, new_caches is None)
                if new_caches is None:
                    new_caches = written
                lse = None
            w_out = W['b_w_out_bf16'][j]
        o_mem = _mem_attn(q_mem, mem_kv_all[l], batch, seq_len)
        h = _out_ffn(o_mix, lse, o_mem, h, W, l, w_out)
    return h, jnp.stack(new_wkv), jnp.stack(new_shift), kv_tiles, new_caches


def _prompt_windows(kv_tiles, batch, seq_len):
    bufs = []
    for g, (win, _) in enumerate(DIL_GROUPS):
        length = min(win, PAST_LEN)
        assert seq_len >= length
        heads = lambda t: t.reshape(batch, seq_len, HEADS_PER_TILE, HEAD_DIM)[:, seq_len - length:]
        bufs.append(jnp.stack([heads(kv_tiles[g]), heads(kv_tiles[N_GROUPS + g])], axis=2))
    return bufs


def kernel(x_prompt, x_sample, state_wkv, state_shift, cache_win_g0, cache_win_g1, cache_win_g2, cache_mem, mem_prompt, g_pre_mix, g_post_mix, g_pre_ffn, g_post_ffn, g_mem, w_mem_kv, w_ffn_in, w_ffn_out, a_mu, a_w_in, a_w0, a_w1, a_w2, a_a0, a_a1, a_a2, a_v0, a_v1, a_v2, a_g1, a_g2, a_k_k, a_k_a, a_r_k, a_lnx_w, a_lnx_b, a_w_out, g_kv, w_kv, b_w_in, b_w_out):
    W = dict(g_pre_mix=g_pre_mix, g_post_mix=g_post_mix, g_pre_ffn=g_pre_ffn, g_post_ffn=g_post_ffn,
             a_mu=a_mu, a_w0=a_w0, a_a0=a_a0, a_v0=a_v0, a_k_k=a_k_k, a_k_a=a_k_a,
             a_r_k=a_r_k.reshape(a_r_k.shape[0], -1), a_lnx_w=a_lnx_w, a_lnx_b=a_lnx_b, g_kv=g_kv)
    for name, w in dict(w_ffn_in=w_ffn_in, w_ffn_out=w_ffn_out, a_w_in=a_w_in, a_w1=a_w1, a_w2=a_w2,
                        a_a1=a_a1, a_a2=a_a2, a_v1=a_v1, a_v2=a_v2, a_g1=a_g1, a_g2=a_g2,
                        a_w_out=a_w_out, w_kv=w_kv, b_w_in=b_w_in, b_w_out=b_w_out,
                        w_mem_kv=w_mem_kv).items():
        W[name + '_bf16'] = w.astype(BF16)

    bp, tp, d = x_prompt.shape
    bs, ts, _ = x_sample.shape
    depth = g_pre_mix.shape[0]
    n_a = a_mu.shape[0]
    heads = a_w0.shape[1] // HEAD_DIM
    m_tok = mem_prompt.shape[1]

    mem_rows = mem_prompt.reshape(bp * m_tok, d)
    no_tab = jnp.zeros((m_tok, LANES), F32)
    mem_kv_p = jnp.stack([
        _norm_proj(mem_rows, g_mem[l], W['w_mem_kv_bf16'][l], no_tab, no_tab, m_tok,
                   n_rot=0, out_width=2 * MEM_WIDTH)[0].reshape(bp, m_tok, 2 * MEM_WIDTH)
        for l in range(depth)])
    wkv_zero = jnp.zeros((n_a, bp, heads, HEAD_DIM, HEAD_DIM), F32)
    shift_zero = jnp.zeros((n_a, bp, d), F32)
    y_p, wkv_p, shift_p, kv_p, _ = _trunk(
        x_prompt.reshape(bp * tp, d), jnp.arange(tp, dtype=jnp.int32), mem_kv_p, wkv_zero, shift_zero,
        None, W, bp, tp, chunk=64)
    bufs_p = _prompt_windows(kv_p, bp, tp)

    caches = [c.reshape(bs, c.shape[1], 2 * COL_TILE) for c in (cache_win_g0, cache_win_g1, cache_win_g2)]
    mem_kv_s = cache_mem.reshape(depth, bs, m_tok, 2 * MEM_WIDTH)
    y_s, wkv_s, shift_s, _, bufs_s = _trunk(
        x_sample.reshape(bs * ts, d), PAST_LEN + jnp.arange(ts, dtype=jnp.int32), mem_kv_s, state_wkv,
        state_shift, caches, W, bs, ts, chunk=ts)
    bufs_s = [b.reshape(c.shape) for b, c in zip(bufs_s, (cache_win_g0, cache_win_g1, cache_win_g2))]

    return (y_p.reshape(bp, tp, d), y_s.reshape(bs, ts, d), wkv_p, shift_p, bufs_p[0], bufs_p[1], bufs_p[2],
            mem_kv_p.reshape(depth, bp, m_tok, 2, MEM_HEADS, HEAD_DIM),
            wkv_s, shift_s, bufs_s[0], bufs_s[1], bufs_s[2])
```

```python
import functools
import math

import jax
import jax.numpy as jnp
from jax import lax
from jax.experimental import pallas as pl
from jax.experimental.pallas import tpu as pltpu

F32 = jnp.float32
BF16 = jnp.bfloat16

HEAD_DIM = 64
LANES = 128
COL_TILE = 256
HEADS_PER_TILE = COL_TILE // HEAD_DIM
MEM_HEADS = 4
MEM_WIDTH = MEM_HEADS * HEAD_DIM
DIL_GROUPS = ((128, 1), (512, 4), (2048, 16))
N_GROUPS = len(DIL_GROUPS)
PAST_LEN = 8192
ROPE_THETA = 10000.0
NORM_EPS = 1e-6
LNX_EPS = HEAD_DIM * 1e-5
KK_EPS = 1e-12
ATTN_SCALE = HEAD_DIM ** -0.5
BAND = 128
VMEM_LIMIT = 56 * 1024 * 1024


def _cparams(*sem):
    return pltpu.CompilerParams(dimension_semantics=sem, vmem_limit_bytes=VMEM_LIMIT)


def _rms(x, g):
    return x * lax.rsqrt(jnp.mean(x * x, axis=-1, keepdims=True) + NORM_EPS) * g


def _mm(a, b):
    return jnp.dot(a.astype(BF16), b.astype(BF16), preferred_element_type=F32)


def _mm_nt(a, b):
    return lax.dot_general(a.astype(BF16), b.astype(BF16), (((1,), (1,)), ((), ())),
                           preferred_element_type=F32)


def _mm_tn(a, b):
    return lax.dot_general(a.astype(BF16), b.astype(BF16), (((0,), (0,)), ((), ())),
                           preferred_element_type=F32)


def _sigmoid(x):
    return 1.0 / (1.0 + jnp.exp(-x))


def _head_lane_masks(width):
    lane = lax.broadcasted_iota(jnp.int32, (1, width), 1)
    return [(lane >= h * HEAD_DIM) & (lane < (h + 1) * HEAD_DIM) for h in range(width // HEAD_DIM)]


def _row_tile(n, want):
    t = min(n, want)
    assert n % t == 0, (n, t)
    return t


def _rotate_heads(y, cos, sin_signed):
    width = y.shape[1]
    cos = jnp.concatenate([cos] * (width // LANES), axis=1)
    sin_signed = jnp.concatenate([sin_signed] * (width // LANES), axis=1)
    lane = lax.broadcasted_iota(jnp.int32, y.shape, 1)
    first_half = (lane % HEAD_DIM) < (HEAD_DIM // 2)
    partner = jnp.where(first_half, pltpu.roll(y, width - HEAD_DIM // 2, axis=1),
                        pltpu.roll(y, HEAD_DIM // 2, axis=1))
    return y * cos + partner * sin_signed


def _norm_proj_kernel(x_ref, g_ref, w_ref, cos_ref, sin_ref, *o_refs, n_rot):
    xn = _rms(x_ref[...], g_ref[...]).astype(BF16)
    for j, o_ref in enumerate(o_refs):
        width = o_ref.shape[1]
        y = jnp.dot(xn, w_ref[:, j * width:(j + 1) * width], preferred_element_type=F32)
        o_ref[...] = _rotate_heads(y, cos_ref[...], sin_ref[...]) if j < n_rot else y


def _resident(shape):
    return pl.BlockSpec(shape, lambda *_: (0,) * len(shape), pipeline_mode=pl.Buffered(1))


def _norm_proj(x, g, w_bf16, cos, sin_signed, seq_len, *, n_rot, out_width=COL_TILE, tm_want=512):
    n, d = x.shape
    cols = w_bf16.shape[1]
    n_out = cols // out_width
    tm = _row_tile(n, tm_want)
    if tm <= seq_len:
        assert seq_len % tm == 0
        per_seq = seq_len // tm
        tab_map = lambda i: (i % per_seq, 0)
    else:
        assert tm % seq_len == 0
        cos = jnp.tile(cos, (tm // seq_len, 1))
        sin_signed = jnp.tile(sin_signed, (tm // seq_len, 1))
        tab_map = lambda i: (0, 0)
    return pl.pallas_call(
        functools.partial(_norm_proj_kernel, n_rot=n_rot),
        grid=(n // tm,),
        in_specs=[
            pl.BlockSpec((tm, d), lambda i: (i, 0)),
            _resident((1, d)),
            _resident(w_bf16.shape),
            pl.BlockSpec((tm, LANES), tab_map),
            pl.BlockSpec((tm, LANES), tab_map),
        ],
        out_specs=[pl.BlockSpec((tm, out_width), lambda i: (i, 0))] * n_out,
        out_shape=[jax.ShapeDtypeStruct((n, out_width), F32)] * n_out,
        compiler_params=_cparams("parallel"),
        name="norm_proj",
    )(x, g.reshape(1, d), w_bf16, cos, sin_signed)


def _rwkv_proj_kernel(*refs, mix, has_vfirst, tm, seq_len, tail):
    it = iter(refs)
    h_ref = next(it)
    hp_ref = next(it) if seq_len >= tm else None
    sh_ref = next(it)
    g_ref, mu_ref, win_ref = next(it), next(it), next(it)
    w0_ref, w1_ref, w2_ref = next(it), next(it), next(it)
    a0_ref, a1_ref, a2_ref = next(it), next(it), next(it)
    g1_ref, g2_ref = next(it), next(it)
    if has_vfirst:
        vf_ref, v0_ref, v1_ref, v2_ref = next(it), next(it), next(it), next(it)
    r_ref, lw_ref, k_ref, v_ref, a_ref, gate_ref, qm_ref, tail_ref = (next(it) for _ in range(8))

    i = pl.program_id(0)
    gain = g_ref[...]
    xn = _rms(h_ref[...], gain)
    row = lax.broadcasted_iota(jnp.int32, (tm, 1), 0)
    rolled = pltpu.roll(xn, 1, axis=0)
    if seq_len >= tm:
        prev_tile_last = _rms(hp_ref[...], gain)[7:8, :]
        at_seq_start = (i * tm) % seq_len == 0
        first = jnp.where(at_seq_start, sh_ref[0][7:8, :], prev_tile_last)
        x_prev = jnp.where(row == 0, first, rolled)
    else:
        x_prev = jnp.where(row % seq_len == 0, sh_ref[...], rolled)
    xx = x_prev - xn
    mu = mu_ref[...]

    def mixed(idx):
        return (xn + xx * mu[idx:idx + 1, :]).astype(BF16)

    xr, xw, xk, xv, xa, xg = (mixed(idx) for idx in range(6))
    r_ref[...] = jnp.dot(xr, win_ref[:, 0:mix], preferred_element_type=F32)
    k_ref[...] = jnp.dot(xk, win_ref[:, mix:2 * mix], preferred_element_type=F32)
    v = jnp.dot(xv, win_ref[:, 2 * mix:3 * mix], preferred_element_type=F32)
    qm_ref[...] = jnp.dot(xn.astype(BF16), win_ref[:, 3 * mix:], preferred_element_type=F32)

    w_lin = w0_ref[...] + _mm(jnp.tanh(_mm(xw, w1_ref[...])), w2_ref[...])
    lw_ref[...] = -math.exp(-0.5) * _sigmoid(w_lin)
    a_ref[...] = _sigmoid(a0_ref[...] + _mm(_mm(xa, a1_ref[...]), a2_ref[...]))
    if has_vfirst:
        v = v + (vf_ref[...] - v) * _sigmoid(v0_ref[...] + _mm(_mm(xv, v1_ref[...]), v2_ref[...]))
    v_ref[...] = v
    gate_ref[...] = _mm(_sigmoid(_mm(xg, g1_ref[...])), g2_ref[...])
    tail_ref[0] = xn[tm - tail:, :]


def _rwkv_proj(h, shift, seq_len, W, li, v_first, *, tm_want=512):
    n, d = h.shape
    mix = W['a_w0'].shape[1]
    tm = _row_tile(n, tm_want)
    n_tiles = n // tm
    has_vfirst = v_first is not None
    row_spec = lambda w: pl.BlockSpec((tm, w), lambda i: (i, 0))
    full = lambda a: _resident(a.shape)

    args, specs = [h], [row_spec(d)]
    if seq_len >= tm:
        assert seq_len % tm == 0
        tail = 8
        args.append(h)
        specs.append(pl.BlockSpec((8, d), lambda i: (jnp.maximum(i * (tm // 8) - 1, 0), 0)))
        args.append(jnp.broadcast_to(shift[:, None, :], (shift.shape[0], 8, d)))
        specs.append(pl.BlockSpec((1, 8, d), lambda i: ((i * tm) // seq_len, 0, 0)))
    else:
        assert tm % seq_len == 0
        tail = tm
        args.append(jnp.repeat(shift, seq_len, axis=0))
        specs.append(row_spec(d))
    small = [W['g_pre_mix'][li].reshape(1, d), W['a_mu'][li], W['a_w_in_bf16'][li],
             W['a_w0'][li].reshape(1, mix), W['a_w1_bf16'][li], W['a_w2_bf16'][li],
             W['a_a0'][li].reshape(1, mix), W['a_a1_bf16'][li], W['a_a2_bf16'][li],
             W['a_g1_bf16'][li], W['a_g2_bf16'][li]]
    args += small
    specs += [full(a) for a in small]
    if has_vfirst:
        vi = li - 1
        extra = [W['a_v0'][vi].reshape(1, mix), W['a_v1_bf16'][vi], W['a_v2_bf16'][vi]]
        args += [v_first] + extra
        specs += [row_spec(mix)] + [full(a) for a in extra]

    wide = jax.ShapeDtypeStruct((n, mix), F32)
    out_shape = [wide] * 6 + [jax.ShapeDtypeStruct((n, MEM_WIDTH), F32),
                              jax.ShapeDtypeStruct((n_tiles, tail, d), F32)]
    out_specs = [row_spec(mix)] * 6 + [row_spec(MEM_WIDTH),
                                       pl.BlockSpec((1, tail, d), lambda i: (i, 0, 0))]
    r, lw, k, v, a, gate, q_mem, xn_tail = pl.pallas_call(
        functools.partial(_rwkv_proj_kernel, mix=mix, has_vfirst=has_vfirst, tm=tm,
                          seq_len=seq_len, tail=tail),
        grid=(n_tiles,),
        in_specs=specs,
        out_specs=out_specs,
        out_shape=out_shape,
        compiler_params=_cparams("parallel"),
        name="rwkv_proj",
    )(*args)
    xn_rows = xn_tail.reshape(n_tiles * tail, d)
    last = xn_rows.reshape(-1, seq_len if seq_len < tm else tail * (seq_len // tm), d)[:, -1]
    return r, lw, k, v, a, gate, q_mem, last


def _wkv_kernel(r_ref, lw_ref, k_ref, v_ref, a_ref, gate_ref, s0_ref, kkp_ref, kap_ref, rkp_ref,
                lnw_ref, lnb_ref, o_ref, sout_ref, s_scr, *, chunk, heads):
    c = pl.program_id(1)

    @pl.when(c == 0)
    def _():
        s_scr[...] = s0_ref[0]

    r, lw, k, v, a = r_ref[0], lw_ref[0], k_ref[0], v_ref[0], a_ref[0]
    row = lax.broadcasted_iota(jnp.int32, (chunk, chunk), 0)
    col = lax.broadcasted_iota(jnp.int32, (chunk, chunk), 1)
    strict = col < row
    incl = col <= row
    tri = incl.astype(BF16)
    lw_hi = lw.astype(BF16)
    lw_rest = lw - lw_hi.astype(F32)
    lw_mid = lw_rest.astype(BF16)
    lw_lo = (lw_rest - lw_mid.astype(F32)).astype(BF16)
    c_inc = (jnp.dot(tri, lw_hi, preferred_element_type=F32)
             + jnp.dot(tri, lw_mid, preferred_element_type=F32)
             + jnp.dot(tri, lw_lo, preferred_element_type=F32))
    c_exc = c_inc - lw
    c_last = c_inc[chunk - 1:chunk, :]
    e_inc = jnp.exp(c_inc)
    e_exc = jnp.exp(c_exc)
    e_neg = jnp.exp(-c_inc)
    e_tail = jnp.exp(c_last - c_inc)
    p_last = jnp.exp(c_last)

    kk_raw = k * kkp_ref[...]
    k2 = k * (1.0 + (a - 1.0) * kap_ref[...])
    rk = r * k2 * rkp_ref[...]
    r_hat = r * e_inc
    k_chk = k2 * e_neg
    k_til = k2 * e_tail

    hr = lax.broadcasted_iota(jnp.int32, (HEAD_DIM, HEAD_DIM), 0)
    hc = lax.broadcasted_iota(jnp.int32, (HEAD_DIM, HEAD_DIM), 1)
    eye_h = hr == hc
    n_levels = max(1, int(math.log2(chunk)))
    hs = range(heads)
    sl = [slice(h * HEAD_DIM, (h + 1) * HEAD_DIM) for h in hs]

    kk = [kk_raw[:, s] for s in sl]
    kk = [x / jnp.maximum(jnp.sqrt(jnp.sum(x * x, axis=-1, keepdims=True)), KK_EPS) for x in kk]
    kka = [kk[h] * a[:, sl[h]] for h in hs]
    a_hat = [-kk[h] * e_exc[:, sl[h]] for h in hs]
    b_chk = [kka[h] * e_neg[:, sl[h]] for h in hs]
    b_til = [kka[h] * e_tail[:, sl[h]] for h in hs]
    v_h = [v[:, s] for s in sl]

    amat = [_mm_nt(jnp.concatenate([a_hat[h], r_hat[:, sl[h]]], axis=0),
                   jnp.concatenate([b_chk[h], k_chk[:, sl[h]]], axis=0)) for h in hs]
    a_ab = [jnp.where(strict, m[:chunk, :chunk], 0.0) for m in amat]
    a_ak = [jnp.where(strict, m[:chunk, chunk:], 0.0) for m in amat]
    a_rb = [jnp.where(incl, m[chunk:, :chunk], 0.0) for m in amat]
    a_rk = [jnp.where(incl, m[chunk:, chunk:], 0.0) for m in amat]
    av = [_mm(jnp.concatenate([a_ak[h], a_rk[h]], axis=0), v_h[h]) for h in hs]

    w12 = [jnp.concatenate([a_hat[h], av[h][:chunk]], axis=1) for h in hs]
    power = a_ab
    for level in range(n_levels):
        w12 = [w12[h] + _mm(power[h], w12[h]) for h in hs]
        if level + 1 < n_levels:
            power = [_mm(p, p) for p in power]

    rx = [_mm(a_rb[h], w12[h]) + jnp.concatenate([r_hat[:, sl[h]], av[h][chunk:]], axis=1) for h in hs]
    gh = [_mm_tn(w12[h], b_til[h]) for h in hs]
    vk = [_mm_tn(v_h[h], k_til[:, sl[h]]) for h in hs]
    g_t = [gh[h][:HEAD_DIM] + jnp.where(eye_h, p_last[:, sl[h]], 0.0) for h in hs]
    h_t = [gh[h][HEAD_DIM:] + vk[h] for h in hs]

    s_prev = [s_scr[h] for h in hs]
    o_h = [_mm_nt(rx[h][:, :HEAD_DIM], s_prev[h]) + rx[h][:, HEAD_DIM:] for h in hs]
    s_new = [_mm(s_prev[h], g_t[h]) + h_t[h] for h in hs]
    for h in hs:
        s_scr[h] = s_new[h]

    normed, bonus = [], []
    for h in hs:
        mean = jnp.mean(o_h[h], axis=-1, keepdims=True)
        cen = o_h[h] - mean
        var = jnp.mean(cen * cen, axis=-1, keepdims=True)
        normed.append(cen * lax.rsqrt(var + LNX_EPS))
        bonus.append(jnp.sum(rk[:, sl[h]], axis=-1, keepdims=True) * v_h[h])

    o = jnp.concatenate(normed, axis=1) * lnw_ref[...] + lnb_ref[...]
    o_ref[0] = (o + jnp.concatenate(bonus, axis=1)) * gate_ref[0]

    @pl.when(c == pl.num_programs(1) - 1)
    def _():
        sout_ref[0] = s_scr[...]


def _wkv(r, lw, k, v, a, gate, s0, W, li, batch, seq_len, chunk):
    mix = r.shape[1]
    heads = mix // HEAD_DIM
    if seq_len % chunk:
        raise ValueError("sequence length must be a multiple of the chunk")
    n_chunks = seq_len // chunk
    seq = lambda t: t.reshape(batch, seq_len, mix)
    blk = pl.BlockSpec((1, chunk, mix), lambda b, c: (b, c, 0))
    st = pl.BlockSpec((1, heads, HEAD_DIM, HEAD_DIM), lambda b, c: (b, 0, 0, 0))
    par = pl.BlockSpec((1, mix), lambda b, c: (0, 0))
    params = [W['a_k_k'][li], W['a_k_a'][li], W['a_r_k'][li], W['a_lnx_w'][li], W['a_lnx_b'][li]]
    o, s_out = pl.pallas_call(
        functools.partial(_wkv_kernel, chunk=chunk, heads=heads),
        grid=(batch, n_chunks),
        in_specs=[blk] * 6 + [st] + [par] * 5,
        out_specs=[blk, st],
        out_shape=[jax.ShapeDtypeStruct((batch, seq_len, mix), F32),
                   jax.ShapeDtypeStruct(s0.shape, F32)],
        scratch_shapes=[pltpu.VMEM((heads, HEAD_DIM, HEAD_DIM), F32)],
        compiler_params=_cparams("parallel", "arbitrary"),
        name="wkv7",
    )(seq(r), seq(lw), seq(k), seq(v), seq(a), seq(gate), s0, *[p.reshape(1, mix) for p in params])
    return o.reshape(batch * seq_len, mix), s_out


def _mem_attn_kernel(q_ref, kv_ref, o_ref):
    q = q_ref[0]
    tq = q.shape[0]
    mk = kv_ref[0, :, :MEM_WIDTH].astype(BF16)
    mv = kv_ref[0, :, MEM_WIDTH:].astype(BF16)
    masks = _head_lane_masks(MEM_WIDTH)
    stacked = jnp.concatenate([jnp.where(m, q, 0.0) for m in masks], axis=0)
    s = _mm_nt(stacked, mk) * ATTN_SCALE
    p = jnp.exp(s - jnp.max(s, axis=-1, keepdims=True))
    p = p * (1.0 / jnp.sum(p, axis=-1, keepdims=True))
    pv = _mm(p, mv)
    out = jnp.zeros(q.shape, F32)
    for h, m in enumerate(masks):
        out = out + jnp.where(m, pv[h * tq:(h + 1) * tq, :], 0.0)
    o_ref[0] = out


def _mem_attn(q_mem, mem_kv, batch, seq_len, *, tq_want=512):
    tq = _row_tile(seq_len, tq_want)
    m = mem_kv.shape[1]
    out = pl.pallas_call(
        _mem_attn_kernel,
        grid=(batch, seq_len // tq),
        in_specs=[pl.BlockSpec((1, tq, MEM_WIDTH), lambda b, t: (b, t, 0)),
                  pl.BlockSpec((1, m, 2 * MEM_WIDTH), lambda b, t: (b, 0, 0))],
        out_specs=pl.BlockSpec((1, tq, MEM_WIDTH), lambda b, t: (b, t, 0)),
        out_shape=jax.ShapeDtypeStruct((batch, seq_len, MEM_WIDTH), F32),
        compiler_params=_cparams("parallel", "parallel"),
        name="mem_attn",
    )(q_mem.reshape(batch, seq_len, MEM_WIDTH), mem_kv)
    return out.reshape(batch * seq_len, MEM_WIDTH)


def _out_ffn_kernel(*refs, n_groups, mix_width, d_ff, tf):
    it = iter(refs)
    if n_groups:
        og = [next(it)[...] for _ in range(n_groups)]
        lse = [next(it)[...] for _ in range(n_groups)]
        top = functools.reduce(jnp.maximum, lse)
        wgt = [jnp.exp(s - top) for s in lse]
        o_mix = sum(w * o for w, o in zip(wgt, og)) / sum(wgt)
    else:
        o_mix = next(it)[...]
    om_ref, wout_ref, gpm_ref, h_ref, gpre_ref, win_ref, wo_ref, gpost_ref, o_ref, acc_ref = it
    mixed = (jnp.dot(o_mix.astype(BF16), wout_ref[:mix_width, :], preferred_element_type=F32)
             + jnp.dot(om_ref[...].astype(BF16), wout_ref[mix_width:, :], preferred_element_type=F32))
    h1 = h_ref[...] + _rms(mixed, gpm_ref[...])
    xn = _rms(h1, gpre_ref[...]).astype(BF16)
    for j in range(d_ff // tf):
        gate = jnp.dot(xn, win_ref[:, j * tf:(j + 1) * tf], preferred_element_type=F32)
        up = jnp.dot(xn, win_ref[:, d_ff + j * tf:d_ff + (j + 1) * tf], preferred_element_type=F32)
        act = (gate * _sigmoid(gate) * up).astype(BF16)
        part = jnp.dot(act, wo_ref[j * tf:(j + 1) * tf, :], preferred_element_type=F32)
        if j == 0:
            acc_ref[...] = part
        else:
            acc_ref[...] += part
    o_ref[...] = h1 + _rms(acc_ref[...], gpost_ref[...])


def _out_ffn(o_mix, lse, o_mem, h, W, l, w_out_bf16, *, tm_want=512, tf=256):
    n, d = h.shape
    tm = _row_tile(n, tm_want)
    w_in, w_o = W['w_ffn_in_bf16'][l], W['w_ffn_out_bf16'][l]
    d_ff = w_o.shape[0]
    assert d_ff % tf == 0
    mix_width = w_out_bf16.shape[0] - MEM_WIDTH
    row = lambda w: pl.BlockSpec((tm, w), lambda i: (i, 0))
    if lse is not None:
        lead = list(o_mix) + list(lse)
        n_groups = len(lse)
    else:
        lead, n_groups = [o_mix], 0
    vec = lambda g: g.reshape(1, d)
    return pl.pallas_call(
        functools.partial(_out_ffn_kernel, n_groups=n_groups, mix_width=mix_width, d_ff=d_ff, tf=tf),
        grid=(n // tm,),
        in_specs=[row(x.shape[1]) for x in lead] + [
            row(MEM_WIDTH), _resident(w_out_bf16.shape), _resident((1, d)), row(d),
            _resident((1, d)), _resident(w_in.shape), _resident(w_o.shape), _resident((1, d))],
        out_specs=row(d),
        out_shape=jax.ShapeDtypeStruct((n, d), F32),
        scratch_shapes=[pltpu.VMEM((tm, d), F32)],
        compiler_params=_cparams("parallel"),
        name="out_ffn",
    )(*lead, o_mem, w_out_bf16, vec(W['g_post_mix'][l]), h, vec(W['g_pre_ffn'][l]), w_in, w_o,
      vec(W['g_post_ffn'][l]))


def _dil_prompt_kernel(q_ref, kc_ref, kh_ref, vc_ref, vh_ref, o_ref, lse_ref, kx_ref, vx_ref, *, tile):
    t = pl.program_id(2)
    kx_ref[0:BAND, :] = kh_ref[0].astype(BF16)
    kx_ref[BAND:, :] = kc_ref[0].astype(BF16)
    vx_ref[0:BAND, :] = vh_ref[0].astype(BF16)
    vx_ref[BAND:, :] = vc_ref[0].astype(BF16)

    rows = HEADS_PER_TILE * BAND
    qi = lax.broadcasted_iota(jnp.int32, (rows, 2 * BAND), 0) % BAND
    kj = lax.broadcasted_iota(jnp.int32, (rows, 2 * BAND), 1)
    band = (kj >= qi) & (kj <= qi + BAND)
    masks = _head_lane_masks(COL_TILE)

    def block(blk, carry):
        start = pl.multiple_of(blk * BAND, BAND)
        q = q_ref[0, pl.ds(start, BAND), :] * ATTN_SCALE
        keys = kx_ref[pl.ds(start, 2 * BAND), :]
        vals = vx_ref[pl.ds(start, 2 * BAND), :]
        lo = jnp.where((t == 0) & (blk == 0), BAND, 0)
        valid = band & (kj >= lo)
        stacked = jnp.concatenate([jnp.where(m, q, 0.0) for m in masks], axis=0)
        s = jnp.where(valid, _mm_nt(stacked, keys), -jnp.inf)
        top = jnp.max(s, axis=-1, keepdims=True)
        p = jnp.exp(s - top)
        den = jnp.sum(p, axis=-1, keepdims=True)
        pv = _mm(p, vals) * (1.0 / den)
        lse_rows = top + jnp.log(den)
        out = jnp.zeros((BAND, COL_TILE), F32)
        lse = jnp.zeros((BAND, COL_TILE), F32)
        for h, m in enumerate(masks):
            out = out + jnp.where(m, pv[h * BAND:(h + 1) * BAND, :], 0.0)
            lse = lse + jnp.where(m, lse_rows[h * BAND:(h + 1) * BAND, :], 0.0)
        o_ref[0, pl.ds(start, BAND), :] = out
        lse_ref[0, pl.ds(start, BAND), :] = lse
        return carry

    lax.fori_loop(0, tile // BAND, block, 0, unroll=2)


def _dil_prompt_group(q, k, v, batch, seq_len, dil):
    cls_len = seq_len // dil
    tile = min(cls_len, 1024)
    assert cls_len % tile == 0 and tile % BAND == 0
    n_t = cls_len // tile
    view = lambda x: x.reshape(batch, cls_len, dil * COL_TILE)
    cur = pl.BlockSpec((1, tile, COL_TILE), lambda b, c, t: (b, t, c))
    halo = pl.BlockSpec((1, BAND, COL_TILE),
                        lambda b, c, t: (b, jnp.maximum(t * (tile // BAND) - 1, 0), c))
    out_shape = jax.ShapeDtypeStruct((batch, cls_len, dil * COL_TILE), F32)
    o, lse = pl.pallas_call(
        functools.partial(_dil_prompt_kernel, tile=tile),
        grid=(batch, dil, n_t),
        in_specs=[cur, cur, halo, cur, halo],
        out_specs=[cur, cur],
        out_shape=[out_shape, out_shape],
        scratch_shapes=[pltpu.VMEM((tile + BAND, COL_TILE), BF16)] * 2,
        compiler_params=_cparams("parallel", "parallel", "arbitrary"),
        name="dil_attn_prompt",
    )(view(q), view(k), view(k), view(v), view(v))
    return o.reshape(batch * seq_len, COL_TILE), lse.reshape(batch * seq_len, COL_TILE)


def _dil_sample_kernel(*refs, seq_len, write_cache):
    q_refs = refs[:N_GROUPS]
    k_refs = refs[N_GROUPS:2 * N_GROUPS]
    v_refs = refs[2 * N_GROUPS:3 * N_GROUPS]
    cache_refs = refs[3 * N_GROUPS:4 * N_GROUPS]
    o_ref = refs[4 * N_GROUPS]
    new_refs = refs[4 * N_GROUPS + 1:]
    masks = _head_lane_masks(COL_TILE)
    rows = HEADS_PER_TILE * seq_len

    run_top = jnp.full((rows, 1), -jnp.inf, F32)
    run_den = jnp.zeros((rows, 1), F32)
    run_acc = jnp.zeros((rows, COL_TILE), F32)
    for g, (win, dil) in enumerate(DIL_GROUPS):
        cache = cache_refs[g]
        length = cache.shape[1]
        q = q_refs[g][...] * ATTN_SCALE
        stacked = jnp.concatenate([jnp.where(m, q, 0.0) for m in masks], axis=0)
        k_new, v_new = k_refs[g][...], v_refs[g][...]
        ck = cache[0, :, :COL_TILE]
        cv = cache[0, :, COL_TILE:]
        s_c = _mm_nt(stacked, ck)
        s_n = _mm_nt(stacked, k_new)
        tq_c = lax.broadcasted_iota(jnp.int32, s_c.shape, 0) % seq_len
        dist_c = length + tq_c - lax.broadcasted_iota(jnp.int32, s_c.shape, 1)
        ok_c = (dist_c % dil == 0) & (dist_c <= win)
        tq_n = lax.broadcasted_iota(jnp.int32, s_n.shape, 0) % seq_len
        dist_n = tq_n - lax.broadcasted_iota(jnp.int32, s_n.shape, 1)
        ok_n = (dist_n >= 0) & (dist_n % dil == 0)
        s_c = jnp.where(ok_c, s_c, -jnp.inf)
        s_n = jnp.where(ok_n, s_n, -jnp.inf)
        top = jnp.maximum(jnp.maximum(jnp.max(s_c, axis=-1, keepdims=True),
                                      jnp.max(s_n, axis=-1, keepdims=True)), run_top)
        p_c = jnp.exp(s_c - top)
        p_n = jnp.exp(s_n - top)
        scale = jnp.exp(run_top - top)
        run_den = run_den * scale + jnp.sum(p_c, axis=-1, keepdims=True) + jnp.sum(p_n, axis=-1, keepdims=True)
        run_acc = run_acc * scale + _mm(p_c, cv) + _mm(p_n, v_new)
        run_top = top
        if write_cache:
            new = new_refs[g]
            if length > seq_len:
                new[0, 0:length - seq_len, :] = cache[0, seq_len:, :]
            new[0, length - seq_len:, :COL_TILE] = k_new
            new[0, length - seq_len:, COL_TILE:] = v_new
    res = run_acc / run_den
    out = jnp.zeros((seq_len, COL_TILE), F32)
    for h, m in enumerate(masks):
        out = out + jnp.where(m, res[h * seq_len:(h + 1) * seq_len, :], 0.0)
    o_ref[...] = out


def _dil_sample(q_slabs, kv_slabs, caches, batch, seq_len, write_cache):
    n = batch * seq_len
    in_specs = [pl.BlockSpec((seq_len, COL_TILE), lambda b: (b, 0))] * (3 * N_GROUPS)
    cache_specs = [pl.BlockSpec((1,) + c.shape[1:], lambda b: (b, 0, 0)) for c in caches]
    out_shape = [jax.ShapeDtypeStruct((n, COL_TILE), F32)]
    out_specs = [pl.BlockSpec((seq_len, COL_TILE), lambda b: (b, 0))]
    if write_cache:
        out_shape += [jax.ShapeDtypeStruct(c.shape, F32) for c in caches]
        out_specs += cache_specs
    res = pl.pallas_call(
        functools.partial(_dil_sample_kernel, seq_len=seq_len, write_cache=write_cache),
        grid=(batch,),
        in_specs=in_specs + cache_specs,
        out_specs=out_specs,
        out_shape=out_shape,
        compiler_params=_cparams("parallel"),
        name="dil_attn_sample",
    )(*q_slabs[:N_GROUPS], *kv_slabs, *caches)
    return res[0], list(res[1:])


def _rope_tables(pos):
    half = HEAD_DIM // 2
    inv = ROPE_THETA ** (-jnp.arange(half, dtype=F32) / half)
    ang = pos.astype(F32)[:, None] * inv[None, :]
    cos, sin = jnp.cos(ang), jnp.sin(ang)
    cos = jnp.tile(jnp.concatenate([cos, cos], axis=-1), (1, LANES // HEAD_DIM))
    sin_signed = jnp.tile(jnp.concatenate([-sin, sin], axis=-1), (1, LANES // HEAD_DIM))
    return cos, sin_signed


def _trunk(h, pos, mem_kv_all, wkv0, shift0, caches, W, batch, seq_len, chunk):
    depth = W['g_pre_mix'].shape[0]
    n_a = W['a_mu'].shape[0]
    cos, sin_signed = _rope_tables(pos)
    new_wkv, new_shift, new_caches = [], [], None
    v_first = None
    kv_tiles = None
    for l in range(depth):
        if l < n_a:
            r, lw, k, v, a, gate, q_mem, xn_last = _rwkv_proj(h, shift0[l], seq_len, W, l, v_first)
            if l == 0:
                v_first = v
            o_mix, s_out = _wkv(r, lw, k, v, a, gate, wkv0[l], W, l, batch, seq_len, chunk)
            new_wkv.append(s_out)
            new_shift.append(xn_last)
            lse = None
            w_out = W['a_w_out_bf16'][l]
        else:
            j = l - n_a
            if kv_tiles is None:
                kv_tiles = _norm_proj(h, W['g_kv'], W['w_kv_bf16'], cos, sin_signed, seq_len, n_rot=N_GROUPS)
            proj = _norm_proj(h, W['g_pre_mix'][l], W['b_w_in_bf16'][j], cos, sin_signed, seq_len,
                              n_rot=N_GROUPS)
            q_mem = proj[N_GROUPS]
            if caches is None:
                outs = [_dil_prompt_group(proj[g], kv_tiles[g], kv_tiles[N_GROUPS + g], batch, seq_len, dil)
                        for g, (_, dil) in enumerate(DIL_GROUPS)]
                o_mix = [o for o, _ in outs]
                lse = [s for _, s in outs]
            else:
                o_mix, written = _dil_sample(proj, kv_tiles, caches, batch, seq_len, new_caches is None)
                if new_caches is None:
                    new_caches = written
                lse = None
            w_out = W['b_w_out_bf16'][j]
        o_mem = _mem_attn(q_mem, mem_kv_all[l], batch, seq_len)
        h = _out_ffn(o_mix, lse, o_mem, h, W, l, w_out)
    return h, jnp.stack(new_wkv), jnp.stack(new_shift), kv_tiles, new_caches


def _prompt_windows(kv_tiles, batch, seq_len):
    bufs = []
    for g, (win, _) in enumerate(DIL_GROUPS):
        length = min(win, PAST_LEN)
        assert seq_len >= length
        heads = lambda t: t.reshape(batch, seq_len, HEADS_PER_TILE, HEAD_DIM)[:, seq_len - length:]
        bufs.append(jnp.stack([heads(kv_tiles[g]), heads(kv_tiles[N_GROUPS + g])], axis=2))
    return bufs


def kernel(x_prompt, x_sample, state_wkv, state_shift, cache_win_g0, cache_win_g1, cache_win_g2, cache_mem, mem_prompt, g_pre_mix, g_post_mix, g_pre_ffn, g_post_ffn, g_mem, w_mem_kv, w_ffn_in, w_ffn_out, a_mu, a_w_in, a_w0, a_w1, a_w2, a_a0, a_a1, a_a2, a_v0, a_v1, a_v2, a_g1, a_g2, a_k_k, a_k_a, a_r_k, a_lnx_w, a_lnx_b, a_w_out, g_kv, w_kv, b_w_in, b_w_out):
    W = dict(g_pre_mix=g_pre_mix, g_post_mix=g_post_mix, g_pre_ffn=g_pre_ffn, g_post_ffn=g_post_ffn,
             a_mu=a_mu, a_w0=a_w0, a_a0=a_a0, a_v0=a_v0, a_k_k=a_k_k, a_k_a=a_k_a,
             a_r_k=a_r_k.reshape(a_r_k.shape[0], -1), a_lnx_w=a_lnx_w, a_lnx_b=a_lnx_b, g_kv=g_kv)
    for name, w in dict(w_ffn_in=w_ffn_in, w_ffn_out=w_ffn_out, a_w_in=a_w_in, a_w1=a_w1, a_w2=a_w2,
                        a_a1=a_a1, a_a2=a_a2, a_v1=a_v1, a_v2=a_v2, a_g1=a_g1, a_g2=a_g2,
                        a_w_out=a_w_out, w_kv=w_kv, b_w_in=b_w_in, b_w_out=b_w_out,
                        w_mem_kv=w_mem_kv).items():
        W[name + '_bf16'] = w.astype(BF16)

    bp, tp, d = x_prompt.shape
    bs, ts, _ = x_sample.shape
    depth = g_pre_mix.shape[0]
    n_a = a_mu.shape[0]
    heads = a_w0.shape[1] // HEAD_DIM
    m_tok = mem_prompt.shape[1]

    mem_rows = mem_prompt.reshape(bp * m_tok, d)
    no_tab = jnp.zeros((m_tok, LANES), F32)
    mem_kv_p = jnp.stack([
        _norm_proj(mem_rows, g_mem[l], W['w_mem_kv_bf16'][l], no_tab, no_tab, m_tok,
                   n_rot=0, out_width=2 * MEM_WIDTH)[0].reshape(bp, m_tok, 2 * MEM_WIDTH)
        for l in range(depth)])
    wkv_zero = jnp.zeros((n_a, bp, heads, HEAD_DIM, HEAD_DIM), F32)
    shift_zero = jnp.zeros((n_a, bp, d), F32)
    y_p, wkv_p, shift_p, kv_p, _ = _trunk(
        x_prompt.reshape(bp * tp, d), jnp.arange(tp, dtype=jnp.int32), mem_kv_p, wkv_zero, shift_zero,
        None, W, bp, tp, chunk=64)
    bufs_p = _prompt_windows(kv_p, bp, tp)

    caches = [c.reshape(bs, c.shape[1], 2 * COL_TILE) for c in (cache_win_g0, cache_win_g1, cache_win_g2)]
    mem_kv_s = cache_mem.reshape(depth, bs, m_tok, 2 * MEM_WIDTH)
    y_s, wkv_s, shift_s, _, bufs_s = _trunk(
        x_sample.reshape(bs * ts, d), PAST_LEN + jnp.arange(ts, dtype=jnp.int32), mem_kv_s, state_wkv,
        state_shift, caches, W, bs, ts, chunk=ts)
    bufs_s = [b.reshape(c.shape) for b, c in zip(bufs_s, (cache_win_g0, cache_win_g1, cache_win_g2))]

    return (y_p.reshape(bp, tp, d), y_s.reshape(bs, ts, d), wkv_p, shift_p, bufs_p[0], bufs_p[1], bufs_p[2],
            mem_kv_p.reshape(depth, bp, m_tok, 2, MEM_HEADS, HEAD_DIM),
            wkv_s, shift_s, bufs_s[0], bufs_s[1], bufs_s[2])
```

```python
import functools
import math

import jax
import jax.numpy as jnp
from jax import lax
from jax.experimental import pallas as pl
from jax.experimental.pallas import tpu as pltpu

F32 = jnp.float32
BF16 = jnp.bfloat16

HEAD_DIM = 64
LANES = 128
COL_TILE = 256
HEADS_PER_TILE = COL_TILE // HEAD_DIM
MEM_HEADS = 4
MEM_WIDTH = MEM_HEADS * HEAD_DIM
DIL_GROUPS = ((128, 1), (512, 4), (2048, 16))
N_GROUPS = len(DIL_GROUPS)
PAST_LEN = 8192
ROPE_THETA = 10000.0
NORM_EPS = 1e-6
LNX_EPS = HEAD_DIM * 1e-5
KK_EPS = 1e-12
ATTN_SCALE = HEAD_DIM ** -0.5
BAND = 128
VMEM_LIMIT = 56 * 1024 * 1024


def _cparams(*sem):
    return pltpu.CompilerParams(dimension_semantics=sem, vmem_limit_bytes=VMEM_LIMIT)


def _rms(x, g):
    return x * lax.rsqrt(jnp.mean(x * x, axis=-1, keepdims=True) + NORM_EPS) * g


def _mm(a, b):
    return jnp.dot(a.astype(BF16), b.astype(BF16), preferred_element_type=F32)


def _mm_nt(a, b):
    return lax.dot_general(a.astype(BF16), b.astype(BF16), (((1,), (1,)), ((), ())),
                           preferred_element_type=F32)


def _mm_tn(a, b):
    return lax.dot_general(a.astype(BF16), b.astype(BF16), (((0,), (0,)), ((), ())),
                           preferred_element_type=F32)


def _sigmoid(x):
    return 1.0 / (1.0 + jnp.exp(-x))


def _head_lane_masks(width):
    lane = lax.broadcasted_iota(jnp.int32, (1, width), 1)
    return [(lane >= h * HEAD_DIM) & (lane < (h + 1) * HEAD_DIM) for h in range(width // HEAD_DIM)]


def _row_tile(n, want):
    t = min(n, want)
    assert n % t == 0, (n, t)
    return t


def _rotate_heads(y, cos, sin_signed):
    width = y.shape[1]
    cos = jnp.concatenate([cos] * (width // LANES), axis=1)
    sin_signed = jnp.concatenate([sin_signed] * (width // LANES), axis=1)
    lane = lax.broadcasted_iota(jnp.int32, y.shape, 1)
    first_half = (lane % HEAD_DIM) < (HEAD_DIM // 2)
    partner = jnp.where(first_half, pltpu.roll(y, width - HEAD_DIM // 2, axis=1),
                        pltpu.roll(y, HEAD_DIM // 2, axis=1))
    return y * cos + partner * sin_signed


def _to_class_view(y, o_ref, scr, dil):
    rows = y.shape[0] // dil
    for half in range(COL_TILE // LANES):
        scr[half] = y[:, half * LANES:(half + 1) * LANES]
    for cls in range(dil):
        for half in range(COL_TILE // LANES):
            lane0 = cls * COL_TILE + half * LANES
            o_ref[:, lane0:lane0 + LANES] = scr[half, pl.ds(cls, rows, stride=dil), :]


def _from_class_view(ref, scr, dil):
    rows = ref.shape[0]
    for cls in range(dil):
        for half in range(COL_TILE // LANES):
            lane0 = cls * COL_TILE + half * LANES
            scr[half, pl.ds(cls, rows, stride=dil), :] = ref[:, lane0:lane0 + LANES]
    return jnp.concatenate([scr[half] for half in range(COL_TILE // LANES)], axis=1)


def _class_view_scratch(tm):
    return pltpu.VMEM((COL_TILE // LANES, tm, LANES), F32)


def _norm_proj_kernel(x_ref, g_ref, w_ref, cos_ref, sin_ref, *refs, n_rot, dils):
    o_refs, scratch = refs[:len(dils)], list(refs[len(dils):])
    xn = _rms(x_ref[...], g_ref[...]).astype(BF16)
    width = w_ref.shape[1] // len(dils)
    for j, (o_ref, dil) in enumerate(zip(o_refs, dils)):
        y = jnp.dot(xn, w_ref[:, j * width:(j + 1) * width], preferred_element_type=F32)
        if j < n_rot:
            y = _rotate_heads(y, cos_ref[...], sin_ref[...])
        if dil == 1:
            o_ref[...] = y
        else:
            _to_class_view(y, o_ref, scratch.pop(0), dil)


def _resident(shape):
    return pl.BlockSpec(shape, lambda *_: (0,) * len(shape), pipeline_mode=pl.Buffered(1))


def _layer_resident(stacked, layer):
    shape = stacked.shape[1:]
    return pl.BlockSpec((None,) + shape, lambda *_: (layer,) + (0,) * len(shape),
                        pipeline_mode=pl.Buffered(1))


def _norm_proj(x, g, w_bf16, cos, sin_signed, seq_len, *, n_rot, out_width=COL_TILE, dils=None, tm_want=512):
    n, d = x.shape
    cols = w_bf16.shape[1]
    n_out = cols // out_width
    dils = tuple(dils) if dils is not None else (1,) * n_out
    assert len(dils) == n_out and (out_width == COL_TILE or set(dils) == {1})
    tm = _row_tile(n, tm_want)
    if tm <= seq_len:
        assert seq_len % tm == 0
        per_seq = seq_len // tm
        tab_map = lambda i: (i % per_seq, 0)
    else:
        assert tm % seq_len == 0
        cos = jnp.tile(cos, (tm // seq_len, 1))
        sin_signed = jnp.tile(sin_signed, (tm // seq_len, 1))
        tab_map = lambda i: (0, 0)
    assert all(tm % (8 * dil) == 0 or dil == 1 for dil in dils)
    return pl.pallas_call(
        functools.partial(_norm_proj_kernel, n_rot=n_rot, dils=dils),
        grid=(n // tm,),
        in_specs=[
            pl.BlockSpec((tm, d), lambda i: (i, 0)),
            _resident((1, d)),
            _resident(w_bf16.shape),
            pl.BlockSpec((tm, LANES), tab_map),
            pl.BlockSpec((tm, LANES), tab_map),
        ],
        out_specs=[pl.BlockSpec((tm // dil, dil * out_width), lambda i: (i, 0)) for dil in dils],
        out_shape=[jax.ShapeDtypeStruct((n // dil, dil * out_width), F32) for dil in dils],
        scratch_shapes=[_class_view_scratch(tm) for dil in dils if dil > 1],
        compiler_params=_cparams("parallel"),
        name="norm_proj",
    )(x, g.reshape(1, d), w_bf16, cos, sin_signed)


def _rwkv_proj_kernel(*refs, mix, has_vfirst, tm, seq_len, tail):
    it = iter(refs)
    h_ref = next(it)
    hp_ref = next(it) if seq_len >= tm else None
    sh_ref = next(it)
    g_ref, mu_ref, win_ref = next(it), next(it), next(it)
    w0_ref, w1_ref, w2_ref = next(it), next(it), next(it)
    a0_ref, a1_ref, a2_ref = next(it), next(it), next(it)
    g1_ref, g2_ref = next(it), next(it)
    if has_vfirst:
        vf_ref, v0_ref, v1_ref, v2_ref = next(it), next(it), next(it), next(it)
    r_ref, lw_ref, k_ref, v_ref, a_ref, gate_ref, qm_ref, tail_ref = (next(it) for _ in range(8))

    i = pl.program_id(0)
    gain = g_ref[...]
    xn = _rms(h_ref[...], gain)
    row = lax.broadcasted_iota(jnp.int32, (tm, 1), 0)
    rolled = pltpu.roll(xn, 1, axis=0)
    if seq_len >= tm:
        prev_tile_last = _rms(hp_ref[...], gain)[7:8, :]
        at_seq_start = (i * tm) % seq_len == 0
        first = jnp.where(at_seq_start, sh_ref[0][7:8, :], prev_tile_last)
        x_prev = jnp.where(row == 0, first, rolled)
    else:
        x_prev = jnp.where(row % seq_len == 0, sh_ref[...], rolled)
    xx = x_prev - xn
    mu = mu_ref[...]

    def mixed(idx):
        return (xn + xx * mu[idx:idx + 1, :]).astype(BF16)

    xr, xw, xk, xv, xa, xg = (mixed(idx) for idx in range(6))
    r_ref[...] = jnp.dot(xr, win_ref[:, 0:mix], preferred_element_type=F32)
    k_ref[...] = jnp.dot(xk, win_ref[:, mix:2 * mix], preferred_element_type=F32)
    v = jnp.dot(xv, win_ref[:, 2 * mix:3 * mix], preferred_element_type=F32)
    qm_ref[...] = jnp.dot(xn.astype(BF16), win_ref[:, 3 * mix:], preferred_element_type=F32)

    w_lin = w0_ref[...] + _mm(jnp.tanh(_mm(xw, w1_ref[...])), w2_ref[...])
    lw_ref[...] = -math.exp(-0.5) * _sigmoid(w_lin)
    a_ref[...] = _sigmoid(a0_ref[...] + _mm(_mm(xa, a1_ref[...]), a2_ref[...]))
    if has_vfirst:
        v = v + (vf_ref[...] - v) * _sigmoid(v0_ref[...] + _mm(_mm(xv, v1_ref[...]), v2_ref[...]))
    v_ref[...] = v
    gate_ref[...] = _mm(_sigmoid(_mm(xg, g1_ref[...])), g2_ref[...])
    tail_ref[0] = xn[tm - tail:, :]


def _rwkv_proj(h, shift, seq_len, W, li, v_first, *, tm_want=512):
    n, d = h.shape
    mix = W['a_w0'].shape[1]
    tm = _row_tile(n, tm_want)
    n_tiles = n // tm
    has_vfirst = v_first is not None
    row_spec = lambda w: pl.BlockSpec((tm, w), lambda i: (i, 0))
    full = lambda a: _resident(a.shape)

    args, specs = [h], [row_spec(d)]
    if seq_len >= tm:
        assert seq_len % tm == 0
        tail = 8
        args.append(h)
        specs.append(pl.BlockSpec((8, d), lambda i: (jnp.maximum(i * (tm // 8) - 1, 0), 0)))
        args.append(jnp.broadcast_to(shift[:, None, :], (shift.shape[0], 8, d)))
        specs.append(pl.BlockSpec((1, 8, d), lambda i: ((i * tm) // seq_len, 0, 0)))
    else:
        assert tm % seq_len == 0
        tail = tm
        args.append(jnp.repeat(shift, seq_len, axis=0))
        specs.append(row_spec(d))
    lead = [W['g_pre_mix'][li].reshape(1, d), W['a_mu'][li]]
    small = [W['a_w0'][li].reshape(1, mix), W['a_w1_bf16'][li], W['a_w2_bf16'][li],
             W['a_a0'][li].reshape(1, mix), W['a_a1_bf16'][li], W['a_a2_bf16'][li],
             W['a_g1_bf16'][li], W['a_g2_bf16'][li]]
    args += lead + [W['a_w_in_bf16']] + small
    specs += [full(a) for a in lead] + [_layer_resident(W['a_w_in_bf16'], li)] + [full(a) for a in small]
    if has_vfirst:
        vi = li - 1
        extra = [W['a_v0'][vi].reshape(1, mix), W['a_v1_bf16'][vi], W['a_v2_bf16'][vi]]
        args += [v_first] + extra
        specs += [row_spec(mix)] + [full(a) for a in extra]

    wide = jax.ShapeDtypeStruct((n, mix), F32)
    out_shape = [wide] * 6 + [jax.ShapeDtypeStruct((n, MEM_WIDTH), F32),
                              jax.ShapeDtypeStruct((n_tiles, tail, d), F32)]
    out_specs = [row_spec(mix)] * 6 + [row_spec(MEM_WIDTH),
                                       pl.BlockSpec((1, tail, d), lambda i: (i, 0, 0))]
    r, lw, k, v, a, gate, q_mem, xn_tail = pl.pallas_call(
        functools.partial(_rwkv_proj_kernel, mix=mix, has_vfirst=has_vfirst, tm=tm,
                          seq_len=seq_len, tail=tail),
        grid=(n_tiles,),
        in_specs=specs,
        out_specs=out_specs,
        out_shape=out_shape,
        compiler_params=_cparams("parallel"),
        name="rwkv_proj",
    )(*args)
    xn_rows = xn_tail.reshape(n_tiles * tail, d)
    last = xn_rows.reshape(-1, seq_len if seq_len < tm else tail * (seq_len // tm), d)[:, -1]
    return r, lw, k, v, a, gate, q_mem, last


_PREP_A_HAT, _PREP_R_HAT, _PREP_B_CHK, _PREP_K_CHK, _PREP_B_TIL, _PREP_K_TIL, _PREP_V, _PREP_BONUS = range(8)
_N_PREP = 8


def _wkv_prepare(r, lw, k, v, a, kkp, kap, rkp, *, chunk):
    row = lax.broadcasted_iota(jnp.int32, (chunk, chunk), 0)
    col = lax.broadcasted_iota(jnp.int32, (chunk, chunk), 1)
    tri = (col <= row).astype(BF16)
    lw_hi = lw.astype(BF16)
    lw_rest = lw - lw_hi.astype(F32)
    lw_mid = lw_rest.astype(BF16)
    lw_lo = (lw_rest - lw_mid.astype(F32)).astype(BF16)
    c_inc = (jnp.dot(tri, lw_hi, preferred_element_type=F32)
             + jnp.dot(tri, lw_mid, preferred_element_type=F32)
             + jnp.dot(tri, lw_lo, preferred_element_type=F32))
    c_last = c_inc[chunk - 1:chunk, :]
    e_inc = jnp.exp(c_inc)
    e_exc = jnp.exp(c_inc - lw)
    e_neg = jnp.exp(-c_inc)
    e_tail = jnp.exp(c_last - c_inc)

    kk_raw = k * kkp
    k2 = k * (1.0 + (a - 1.0) * kap)
    rk = r * k2 * rkp
    r_hat = r * e_inc
    k_chk = k2 * e_neg
    k_til = k2 * e_tail

    def prepare_head(h):
        s = slice(h * HEAD_DIM, (h + 1) * HEAD_DIM)
        kk = kk_raw[:, s]
        kk = kk / jnp.maximum(jnp.sqrt(jnp.sum(kk * kk, axis=-1, keepdims=True)), KK_EPS)
        kka = kk * a[:, s]
        v_h = v[:, s]
        item = [None] * _N_PREP
        item[_PREP_A_HAT] = -kk * e_exc[:, s]
        item[_PREP_R_HAT] = r_hat[:, s]
        item[_PREP_B_CHK] = kka * e_neg[:, s]
        item[_PREP_K_CHK] = k_chk[:, s]
        item[_PREP_B_TIL] = kka * e_tail[:, s]
        item[_PREP_K_TIL] = k_til[:, s]
        item[_PREP_V] = v_h
        item[_PREP_BONUS] = jnp.sum(rk[:, s], axis=-1, keepdims=True) * v_h
        return item

    return prepare_head, jnp.exp(c_last)


def _wkv_kernel(r_ref, lw_ref, k_ref, v_ref, a_ref, gate_ref, s0_ref, kkp_ref, kap_ref, rkp_ref,
                lnw_ref, lnb_ref, o_ref, sout_ref, s_scr, prep_scr, plast_scr, *, chunk, heads, rows):
    c = pl.program_id(1)
    units = rows * heads

    fill = c % 2
    read = 1 - fill

    @pl.when(c == 0)
    def _():
        for b in range(rows):
            s_scr[b * heads:(b + 1) * heads] = s0_ref[b]
        prep_scr[1] = jnp.zeros(prep_scr.shape[1:], F32)
        plast_scr[1] = jnp.ones(plast_scr.shape[1:], F32)

    prepared = [_wkv_prepare(r_ref[b], lw_ref[b], k_ref[b], v_ref[b], a_ref[b], kkp_ref[...], kap_ref[...],
                             rkp_ref[...], chunk=chunk) for b in range(rows)]
    for b in range(rows):
        plast_scr[fill, b:b + 1, :] = prepared[b][1]
    n_prepared = [0]

    def prepare_some(n):
        for _ in range(n):
            u = n_prepared[0]
            if u < units:
                for item, val in enumerate(prepared[u // heads][0](u % heads)):
                    prep_scr[fill, item, u] = val
                n_prepared[0] = u + 1

    row = lax.broadcasted_iota(jnp.int32, (chunk, chunk), 0)
    col = lax.broadcasted_iota(jnp.int32, (chunk, chunk), 1)
    strict = col < row
    incl = col <= row
    eye_h = (lax.broadcasted_iota(jnp.int32, (HEAD_DIM, HEAD_DIM), 0)
             == lax.broadcasted_iota(jnp.int32, (HEAD_DIM, HEAD_DIM), 1))
    n_levels = max(1, int(math.log2(chunk)))
    hs = range(units)
    get = lambda item: [prep_scr[read, item, u] for u in hs]
    a_hat, r_hat, b_chk, k_chk = get(_PREP_A_HAT), get(_PREP_R_HAT), get(_PREP_B_CHK), get(_PREP_K_CHK)
    v_h = get(_PREP_V)
    p_rows = plast_scr[read]
    p_last = [p_rows[u // heads:u // heads + 1, (u % heads) * HEAD_DIM:(u % heads + 1) * HEAD_DIM] for u in hs]
    per_stage = -(-units // (n_levels + 4))

    amat = [_mm_nt(jnp.concatenate([a_hat[h], r_hat[h]], axis=0),
                   jnp.concatenate([b_chk[h], k_chk[h]], axis=0)) for h in hs]
    prepare_some(per_stage)
    a_ab = [jnp.where(strict, m[:chunk, :chunk], 0.0) for m in amat]
    a_ak = [jnp.where(strict, m[:chunk, chunk:], 0.0) for m in amat]
    a_rb = [jnp.where(incl, m[chunk:, :chunk], 0.0) for m in amat]
    a_rk = [jnp.where(incl, m[chunk:, chunk:], 0.0) for m in amat]
    av = [_mm(jnp.concatenate([a_ak[h], a_rk[h]], axis=0), v_h[h]) for h in hs]
    prepare_some(per_stage)

    w12 = [jnp.concatenate([a_hat[h], av[h][:chunk]], axis=1) for h in hs]
    power = a_ab
    for level in range(n_levels):
        w12 = [w12[h] + _mm(power[h], w12[h]) for h in hs]
        if level + 1 < n_levels:
            power = [_mm(p, p) for p in power]
        prepare_some(per_stage)

    b_til, k_til = get(_PREP_B_TIL), get(_PREP_K_TIL)
    rx = [_mm(a_rb[h], w12[h]) + jnp.concatenate([r_hat[h], av[h][chunk:]], axis=1) for h in hs]
    gh = [_mm_tn(w12[h], b_til[h]) for h in hs]
    vk = [_mm_tn(v_h[h], k_til[h]) for h in hs]
    prepare_some(per_stage)
    g_t = [gh[h][:HEAD_DIM] + jnp.where(eye_h, p_last[h], 0.0) for h in hs]
    h_t = [gh[h][HEAD_DIM:] + vk[h] for h in hs]

    s_prev = [s_scr[h] for h in hs]
    o_h = [_mm_nt(rx[h][:, :HEAD_DIM], s_prev[h]) + rx[h][:, HEAD_DIM:] for h in hs]
    s_new = [_mm(s_prev[h], g_t[h]) + h_t[h] for h in hs]
    for h in hs:
        s_scr[h] = s_new[h]
    prepare_some(units)

    bonus = get(_PREP_BONUS)
    normed = []
    for h in hs:
        mean = jnp.mean(o_h[h], axis=-1, keepdims=True)
        cen = o_h[h] - mean
        var = jnp.mean(cen * cen, axis=-1, keepdims=True)
        normed.append(cen * lax.rsqrt(var + LNX_EPS))

    for b in range(rows):
        own = slice(b * heads, (b + 1) * heads)
        o = jnp.concatenate(normed[own], axis=1) * lnw_ref[...] + lnb_ref[...]
        o_ref[b] = (o + jnp.concatenate(bonus[own], axis=1)) * gate_ref[b]

    @pl.when(c == pl.num_programs(1) - 1)
    def _():
        for b in range(rows):
            sout_ref[b] = s_scr[b * heads:(b + 1) * heads]


def _wkv(r, lw, k, v, a, gate, s0, W, li, batch, seq_len, chunk):
    mix = r.shape[1]
    heads = mix // HEAD_DIM
    if seq_len % chunk:
        raise ValueError("sequence length must be a multiple of the chunk")
    n_chunks = seq_len // chunk
    seq = lambda t: t.reshape(batch, seq_len, mix)
    rows = 1
    ahead = pl.BlockSpec((rows, chunk, mix), lambda b, c: (b, jnp.minimum(c, n_chunks - 1), 0))
    behind = pl.BlockSpec((rows, chunk, mix), lambda b, c: (b, jnp.maximum(c - 1, 0), 0))
    st = pl.BlockSpec((rows, heads, HEAD_DIM, HEAD_DIM), lambda b, c: (b, 0, 0, 0))
    par = pl.BlockSpec((1, mix), lambda b, c: (0, 0))
    params = [W['a_k_k'][li], W['a_k_a'][li], W['a_r_k'][li], W['a_lnx_w'][li], W['a_lnx_b'][li]]
    o, s_out = pl.pallas_call(
        functools.partial(_wkv_kernel, chunk=chunk, heads=heads, rows=rows),
        grid=(batch // rows, n_chunks + 1),
        in_specs=[ahead] * 5 + [behind, st] + [par] * 5,
        out_specs=[behind, st],
        out_shape=[jax.ShapeDtypeStruct((batch, seq_len, mix), F32),
                   jax.ShapeDtypeStruct(s0.shape, F32)],
        scratch_shapes=[pltpu.VMEM((rows * heads, HEAD_DIM, HEAD_DIM), F32),
                        pltpu.VMEM((2, _N_PREP, rows * heads, chunk, HEAD_DIM), F32),
                        pltpu.VMEM((2, rows, mix), F32)],
        compiler_params=_cparams("parallel", "arbitrary"),
        name="wkv7",
    )(seq(r), seq(lw), seq(k), seq(v), seq(a), seq(gate), s0, *[p.reshape(1, mix) for p in params])
    return o.reshape(batch * seq_len, mix), s_out


def _mem_attn_kernel(q_ref, kv_ref, o_ref):
    q = q_ref[0]
    tq = q.shape[0]
    mk = kv_ref[0, :, :MEM_WIDTH].astype(BF16)
    mv = kv_ref[0, :, MEM_WIDTH:].astype(BF16)
    masks = _head_lane_masks(MEM_WIDTH)
    stacked = jnp.concatenate([jnp.where(m, q, 0.0) for m in masks], axis=0)
    s = _mm_nt(stacked, mk) * ATTN_SCALE
    p = jnp.exp(s - jnp.max(s, axis=-1, keepdims=True))
    p = p * (1.0 / jnp.sum(p, axis=-1, keepdims=True))
    pv = _mm(p, mv)
    out = jnp.zeros(q.shape, F32)
    for h, m in enumerate(masks):
        out = out + jnp.where(m, pv[h * tq:(h + 1) * tq, :], 0.0)
    o_ref[0] = out


def _mem_attn(q_mem, mem_kv, batch, seq_len, *, tq_want=512):
    tq = _row_tile(seq_len, tq_want)
    m = mem_kv.shape[1]
    out = pl.pallas_call(
        _mem_attn_kernel,
        grid=(batch, seq_len // tq),
        in_specs=[pl.BlockSpec((1, tq, MEM_WIDTH), lambda b, t: (b, t, 0)),
                  pl.BlockSpec((1, m, 2 * MEM_WIDTH), lambda b, t: (b, 0, 0))],
        out_specs=pl.BlockSpec((1, tq, MEM_WIDTH), lambda b, t: (b, t, 0)),
        out_shape=jax.ShapeDtypeStruct((batch, seq_len, MEM_WIDTH), F32),
        compiler_params=_cparams("parallel", "parallel"),
        name="mem_attn",
    )(q_mem.reshape(batch, seq_len, MEM_WIDTH), mem_kv)
    return out.reshape(batch * seq_len, MEM_WIDTH)


def _out_ffn_kernel(*refs, dils, mix_width, d_ff, tf):
    n_groups = len(dils)
    n_scratch = 2 * sum(dil > 1 for dil in dils)
    scratch = list(refs[len(refs) - n_scratch:])
    it = iter(refs[:len(refs) - n_scratch])
    if n_groups:
        read = lambda ref, dil: ref[...] if dil == 1 else _from_class_view(ref, scratch.pop(0), dil)
        og = [read(next(it), dil) for dil in dils]
        lse = [read(next(it), dil) for dil in dils]
        top = functools.reduce(jnp.maximum, lse)
        wgt = [jnp.exp(s - top) for s in lse]
        o_mix = sum(w * o for w, o in zip(wgt, og)) / sum(wgt)
    else:
        o_mix = next(it)[...]
    om_ref, wout_ref, gpm_ref, h_ref, gpre_ref, win_ref, wo_ref, gpost_ref, o_ref, acc_ref = it
    mixed = (jnp.dot(o_mix.astype(BF16), wout_ref[:mix_width, :], preferred_element_type=F32)
             + jnp.dot(om_ref[...].astype(BF16), wout_ref[mix_width:, :], preferred_element_type=F32))
    h1 = h_ref[...] + _rms(mixed, gpm_ref[...])
    xn = _rms(h1, gpre_ref[...]).astype(BF16)
    for j in range(d_ff // tf):
        gate = jnp.dot(xn, win_ref[:, j * tf:(j + 1) * tf], preferred_element_type=F32)
        up = jnp.dot(xn, win_ref[:, d_ff + j * tf:d_ff + (j + 1) * tf], preferred_element_type=F32)
        act = (gate * _sigmoid(gate) * up).astype(BF16)
        part = jnp.dot(act, wo_ref[j * tf:(j + 1) * tf, :], preferred_element_type=F32)
        if j == 0:
            acc_ref[...] = part
        else:
            acc_ref[...] += part
    o_ref[...] = h1 + _rms(acc_ref[...], gpost_ref[...])


def _out_ffn(o_mix, lse, dils, o_mem, h, W, l, w_out_stack, w_out_layer, *, tm_want=512, tf=256):
    n, d = h.shape
    tm = _row_tile(n, tm_want)
    w_in, w_o = W['w_ffn_in_bf16'], W['w_ffn_out_bf16']
    d_ff = w_o.shape[1]
    assert d_ff % tf == 0
    mix_width = w_out_stack.shape[1] - MEM_WIDTH
    row = lambda w: pl.BlockSpec((tm, w), lambda i: (i, 0))
    if lse is not None:
        lead = list(o_mix) + list(lse)
        lead_specs = [pl.BlockSpec((tm // dil, dil * COL_TILE), lambda i: (i, 0)) for dil in dils] * 2
    else:
        lead, lead_specs, dils = [o_mix], [row(o_mix.shape[1])], ()
    vec = lambda g: g.reshape(1, d)
    return pl.pallas_call(
        functools.partial(_out_ffn_kernel, dils=tuple(dils), mix_width=mix_width, d_ff=d_ff, tf=tf),
        grid=(n // tm,),
        in_specs=lead_specs + [
            row(MEM_WIDTH), _layer_resident(w_out_stack, w_out_layer), _resident((1, d)), row(d),
            _resident((1, d)), _layer_resident(w_in, l), _layer_resident(w_o, l), _resident((1, d))],
        out_specs=row(d),
        out_shape=jax.ShapeDtypeStruct((n, d), F32),
        scratch_shapes=[pltpu.VMEM((tm, d), F32)] + [_class_view_scratch(tm) for dil in dils if dil > 1] * 2,
        compiler_params=_cparams("parallel"),
        name="out_ffn",
    )(*lead, o_mem, w_out_stack, vec(W['g_post_mix'][l]), h, vec(W['g_pre_ffn'][l]), w_in, w_o,
      vec(W['g_post_ffn'][l]))


def _dil_prompt_kernel(q_ref, kc_ref, kh_ref, vc_ref, vh_ref, o_ref, lse_ref, kx_ref, vx_ref, *, tile):
    t = pl.program_id(2)
    kx_ref[0:BAND, :] = kh_ref[0].astype(BF16)
    kx_ref[BAND:, :] = kc_ref[0].astype(BF16)
    vx_ref[0:BAND, :] = vh_ref[0].astype(BF16)
    vx_ref[BAND:, :] = vc_ref[0].astype(BF16)

    rows = HEADS_PER_TILE * BAND
    qi = lax.broadcasted_iota(jnp.int32, (rows, 2 * BAND), 0) % BAND
    kj = lax.broadcasted_iota(jnp.int32, (rows, 2 * BAND), 1)
    band = (kj >= qi) & (kj <= qi + BAND)
    masks = _head_lane_masks(COL_TILE)

    def block(blk, carry):
        start = pl.multiple_of(blk * BAND, BAND)
        q = q_ref[0, pl.ds(start, BAND), :] * ATTN_SCALE
        keys = kx_ref[pl.ds(start, 2 * BAND), :]
        vals = vx_ref[pl.ds(start, 2 * BAND), :]
        lo = jnp.where((t == 0) & (blk == 0), BAND, 0)
        valid = band & (kj >= lo)
        stacked = jnp.concatenate([jnp.where(m, q, 0.0) for m in masks], axis=0)
        s = jnp.where(valid, _mm_nt(stacked, keys), -jnp.inf)
        top = jnp.max(s, axis=-1, keepdims=True)
        p = jnp.exp(s - top)
        den = jnp.sum(p, axis=-1, keepdims=True)
        pv = _mm(p, vals) * (1.0 / den)
        lse_rows = top + jnp.log(den)
        out = jnp.zeros((BAND, COL_TILE), F32)
        lse = jnp.zeros((BAND, COL_TILE), F32)
        for h, m in enumerate(masks):
            out = out + jnp.where(m, pv[h * BAND:(h + 1) * BAND, :], 0.0)
            lse = lse + jnp.where(m, lse_rows[h * BAND:(h + 1) * BAND, :], 0.0)
        o_ref[0, pl.ds(start, BAND), :] = out
        lse_ref[0, pl.ds(start, BAND), :] = lse
        return carry

    lax.fori_loop(0, tile // BAND, block, 0, unroll=2)


def _dil_prompt_group(q, k, v, batch, seq_len, dil):
    cls_len = seq_len // dil
    tile = min(cls_len, 1024)
    assert cls_len % tile == 0 and tile % BAND == 0
    n_t = cls_len // tile
    view = lambda x: x.reshape(batch, cls_len, dil * COL_TILE)
    cur = pl.BlockSpec((1, tile, COL_TILE), lambda b, c, t: (b, t, c))
    halo = pl.BlockSpec((1, BAND, COL_TILE),
                        lambda b, c, t: (b, jnp.maximum(t * (tile // BAND) - 1, 0), c))
    out_shape = jax.ShapeDtypeStruct((batch, cls_len, dil * COL_TILE), F32)
    o, lse = pl.pallas_call(
        functools.partial(_dil_prompt_kernel, tile=tile),
        grid=(batch, dil, n_t),
        in_specs=[cur, cur, halo, cur, halo],
        out_specs=[cur, cur],
        out_shape=[out_shape, out_shape],
        scratch_shapes=[pltpu.VMEM((tile + BAND, COL_TILE), BF16)] * 2,
        compiler_params=_cparams("parallel", "parallel", "arbitrary"),
        name="dil_attn_prompt",
    )(view(q), view(k), view(k), view(v), view(v))
    flat = lambda x: x.reshape(batch * cls_len, dil * COL_TILE)
    return flat(o), flat(lse)


def _dil_sample_kernel(*refs, seq_len, write_cache):
    q_refs = refs[:N_GROUPS]
    k_refs = refs[N_GROUPS:2 * N_GROUPS]
    v_refs = refs[2 * N_GROUPS:3 * N_GROUPS]
    cache_refs = refs[3 * N_GROUPS:4 * N_GROUPS]
    o_ref = refs[4 * N_GROUPS]
    new_refs = refs[4 * N_GROUPS + 1:]
    masks = _head_lane_masks(COL_TILE)
    rows = HEADS_PER_TILE * seq_len

    run_top = jnp.full((rows, 1), -jnp.inf, F32)
    run_den = jnp.zeros((rows, 1), F32)
    run_acc = jnp.zeros((rows, COL_TILE), F32)
    for g, (win, dil) in enumerate(DIL_GROUPS):
        cache = cache_refs[g]
        length = cache.shape[1]
        q = q_refs[g][...] * ATTN_SCALE
        stacked = jnp.concatenate([jnp.where(m, q, 0.0) for m in masks], axis=0)
        k_new, v_new = k_refs[g][...], v_refs[g][...]
        ck = cache[0, :, :COL_TILE]
        cv = cache[0, :, COL_TILE:]
        s_c = _mm_nt(stacked, ck)
        s_n = _mm_nt(stacked, k_new)
        tq_c = lax.broadcasted_iota(jnp.int32, s_c.shape, 0) % seq_len
        dist_c = length + tq_c - lax.broadcasted_iota(jnp.int32, s_c.shape, 1)
        ok_c = (dist_c % dil == 0) & (dist_c <= win)
        tq_n = lax.broadcasted_iota(jnp.int32, s_n.shape, 0) % seq_len
        dist_n = tq_n - lax.broadcasted_iota(jnp.int32, s_n.shape, 1)
        ok_n = (dist_n >= 0) & (dist_n % dil == 0)
        s_c = jnp.where(ok_c, s_c, -jnp.inf)
        s_n = jnp.where(ok_n, s_n, -jnp.inf)
        top = jnp.maximum(jnp.maximum(jnp.max(s_c, axis=-1, keepdims=True),
                                      jnp.max(s_n, axis=-1, keepdims=True)), run_top)
        p_c = jnp.exp(s_c - top)
        p_n = jnp.exp(s_n - top)
        scale = jnp.exp(run_top - top)
        run_den = run_den * scale + jnp.sum(p_c, axis=-1, keepdims=True) + jnp.sum(p_n, axis=-1, keepdims=True)
        run_acc = run_acc * scale + _mm(p_c, cv) + _mm(p_n, v_new)
        run_top = top
        if write_cache:
            new = new_refs[g]
            if length > seq_len:
                new[0, 0:length - seq_len, :] = cache[0, seq_len:, :]
            new[0, length - seq_len:, :COL_TILE] = k_new
            new[0, length - seq_len:, COL_TILE:] = v_new
    res = run_acc / run_den
    out = jnp.zeros((seq_len, COL_TILE), F32)
    for h, m in enumerate(masks):
        out = out + jnp.where(m, res[h * seq_len:(h + 1) * seq_len, :], 0.0)
    o_ref[...] = out


def _dil_sample(q_slabs, kv_slabs, caches, batch, seq_len, write_cache):
    n = batch * seq_len
    in_specs = [pl.BlockSpec((seq_len, COL_TILE), lambda b: (b, 0))] * (3 * N_GROUPS)
    cache_specs = [pl.BlockSpec((1,) + c.shape[1:], lambda b: (b, 0, 0)) for c in caches]
    out_shape = [jax.ShapeDtypeStruct((n, COL_TILE), F32)]
    out_specs = [pl.BlockSpec((seq_len, COL_TILE), lambda b: (b, 0))]
    if write_cache:
        out_shape += [jax.ShapeDtypeStruct(c.shape, F32) for c in caches]
        out_specs += cache_specs
    res = pl.pallas_call(
        functools.partial(_dil_sample_kernel, seq_len=seq_len, write_cache=write_cache),
        grid=(batch,),
        in_specs=in_specs + cache_specs,
        out_specs=out_specs,
        out_shape=out_shape,
        compiler_params=_cparams("parallel"),
        name="dil_attn_sample",
    )(*q_slabs[:N_GROUPS], *kv_slabs, *caches)
    return res[0], list(res[1:])


def _rope_tables(pos):
    half = HEAD_DIM // 2
    inv = ROPE_THETA ** (-jnp.arange(half, dtype=F32) / half)
    ang = pos.astype(F32)[:, None] * inv[None, :]
    cos, sin = jnp.cos(ang), jnp.sin(ang)
    cos = jnp.tile(jnp.concatenate([cos, cos], axis=-1), (1, LANES // HEAD_DIM))
    sin_signed = jnp.tile(jnp.concatenate([-sin, sin], axis=-1), (1, LANES // HEAD_DIM))
    return cos, sin_signed


def _trunk(h, pos, mem_kv_all, wkv0, shift0, caches, W, batch, seq_len, chunk):
    depth = W['g_pre_mix'].shape[0]
    n_a = W['a_mu'].shape[0]
    cos, sin_signed = _rope_tables(pos)
    new_wkv, new_shift, new_caches = [], [], None
    v_first = None
    kv_tiles = None
    for l in range(depth):
        if l < n_a:
            r, lw, k, v, a, gate, q_mem, xn_last = _rwkv_proj(h, shift0[l], seq_len, W, l, v_first)
            if l == 0:
                v_first = v
            o_mix, s_out = _wkv(r, lw, k, v, a, gate, wkv0[l], W, l, batch, seq_len, chunk)
            new_wkv.append(s_out)
            new_shift.append(xn_last)
            lse = None
            w_out, w_out_layer = W['a_w_out_bf16'], l
        else:
            j = l - n_a
            dils = tuple(dil for _, dil in DIL_GROUPS) if caches is None else (1,) * N_GROUPS
            if kv_tiles is None:
                kv_tiles = _norm_proj(h, W['g_kv'], W['w_kv_bf16'], cos, sin_signed, seq_len, n_rot=N_GROUPS,
                                      dils=dils * 2)
            proj = _norm_proj(h, W['g_pre_mix'][l], W['b_w_in_bf16'][j], cos, sin_signed, seq_len,
                              n_rot=N_GROUPS, dils=dils + (1,))
            q_mem = proj[N_GROUPS]
            if caches is None:
                outs = [_dil_prompt_group(proj[g], kv_tiles[g], kv_tiles[N_GROUPS + g], batch, seq_len, dil)
                        for g, dil in enumerate(dils)]
                o_mix = [o for o, _ in outs]
                lse = [s for _, s in outs]
            else:
                o_mix, written = _dil_sample(proj, kv_tiles, caches, batch, seq_len, new_caches is None)
                if new_caches is None:
                    new_caches = written
                lse = None
            w_out, w_out_layer = W['b_w_out_bf16'], j
        o_mem = _mem_attn(q_mem, mem_kv_all[l], batch, seq_len)
        h = _out_ffn(o_mix, lse, dils if lse is not None else (), o_mem, h, W, l, w_out, w_out_layer)
    return h, jnp.stack(new_wkv), jnp.stack(new_shift), kv_tiles, new_caches


def _prompt_windows(kv_tiles, batch, seq_len):
    bufs = []
    for g, (win, dil) in enumerate(DIL_GROUPS):
        length = min(win, PAST_LEN)
        assert seq_len >= length and length % dil == 0
        rows = seq_len // dil

        def tail(t):
            t = t.reshape(batch, rows, dil, HEADS_PER_TILE, HEAD_DIM)[:, rows - length // dil:]
            return t.reshape(batch, length, HEADS_PER_TILE, HEAD_DIM)

        bufs.append(jnp.stack([tail(kv_tiles[g]), tail(kv_tiles[N_GROUPS + g])], axis=2))
    return bufs


def kernel(x_prompt, x_sample, state_wkv, state_shift, cache_win_g0, cache_win_g1, cache_win_g2, cache_mem, mem_prompt, g_pre_mix, g_post_mix, g_pre_ffn, g_post_ffn, g_mem, w_mem_kv, w_ffn_in, w_ffn_out, a_mu, a_w_in, a_w0, a_w1, a_w2, a_a0, a_a1, a_a2, a_v0, a_v1, a_v2, a_g1, a_g2, a_k_k, a_k_a, a_r_k, a_lnx_w, a_lnx_b, a_w_out, g_kv, w_kv, b_w_in, b_w_out):
    W = dict(g_pre_mix=g_pre_mix, g_post_mix=g_post_mix, g_pre_ffn=g_pre_ffn, g_post_ffn=g_post_ffn,
             a_mu=a_mu, a_w0=a_w0, a_a0=a_a0, a_v0=a_v0, a_k_k=a_k_k, a_k_a=a_k_a,
             a_r_k=a_r_k.reshape(a_r_k.shape[0], -1), a_lnx_w=a_lnx_w, a_lnx_b=a_lnx_b, g_kv=g_kv)
    for name, w in dict(w_ffn_in=w_ffn_in, w_ffn_out=w_ffn_out, a_w_in=a_w_in, a_w1=a_w1, a_w2=a_w2,
                        a_a1=a_a1, a_a2=a_a2, a_v1=a_v1, a_v2=a_v2, a_g1=a_g1, a_g2=a_g2,
                        a_w_out=a_w_out, w_kv=w_kv, b_w_in=b_w_in, b_w_out=b_w_out,
                        w_mem_kv=w_mem_kv).items():
        W[name + '_bf16'] = w.astype(BF16)

    bp, tp, d = x_prompt.shape
    bs, ts, _ = x_sample.shape
    depth = g_pre_mix.shape[0]
    n_a = a_mu.shape[0]
    heads = a_w0.shape[1] // HEAD_DIM
    m_tok = mem_prompt.shape[1]

    mem_rows = mem_prompt.reshape(bp * m_tok, d)
    no_tab = jnp.zeros((m_tok, LANES), F32)
    mem_kv_p = jnp.stack([
        _norm_proj(mem_rows, g_mem[l], W['w_mem_kv_bf16'][l], no_tab, no_tab, m_tok,
                   n_rot=0, out_width=2 * MEM_WIDTH)[0].reshape(bp, m_tok, 2 * MEM_WIDTH)
        for l in range(depth)])
    wkv_zero = jnp.zeros((n_a, bp, heads, HEAD_DIM, HEAD_DIM), F32)
    shift_zero = jnp.zeros((n_a, bp, d), F32)
    y_p, wkv_p, shift_p, kv_p, _ = _trunk(
        x_prompt.reshape(bp * tp, d), jnp.arange(tp, dtype=jnp.int32), mem_kv_p, wkv_zero, shift_zero,
        None, W, bp, tp, chunk=64)
    bufs_p = _prompt_windows(kv_p, bp, tp)

    caches = [c.reshape(bs, c.shape[1], 2 * COL_TILE) for c in (cache_win_g0, cache_win_g1, cache_win_g2)]
    mem_kv_s = cache_mem.reshape(depth, bs, m_tok, 2 * MEM_WIDTH)
    y_s, wkv_s, shift_s, _, bufs_s = _trunk(
        x_sample.reshape(bs * ts, d), PAST_LEN + jnp.arange(ts, dtype=jnp.int32), mem_kv_s, state_wkv,
        state_shift, caches, W, bs, ts, chunk=ts)
    bufs_s = [b.reshape(c.shape) for b, c in zip(bufs_s, (cache_win_g0, cache_win_g1, cache_win_g2))]

    return (y_p.reshape(bp, tp, d), y_s.reshape(bs, ts, d), wkv_p, shift_p, bufs_p[0], bufs_p[1], bufs_p[2],
            mem_kv_p.reshape(depth, bp, m_tok, 2, MEM_HEADS, HEAD_DIM),
            wkv_s, shift_s, bufs_s[0], bufs_s[1], bufs_s[2])
```

```python
import functools
import math

import jax
import jax.numpy as jnp
from jax import lax
from jax.experimental import pallas as pl
from jax.experimental.pallas import tpu as pltpu

F32 = jnp.float32
BF16 = jnp.bfloat16

HEAD_DIM = 64
LANES = 128
COL_TILE = 256
HEADS_PER_TILE = COL_TILE // HEAD_DIM
MEM_HEADS = 4
MEM_WIDTH = MEM_HEADS * HEAD_DIM
DIL_GROUPS = ((128, 1), (512, 4), (2048, 16))
N_GROUPS = len(DIL_GROUPS)
PAST_LEN = 8192
ROPE_THETA = 10000.0
NORM_EPS = 1e-6
LNX_EPS = HEAD_DIM * 1e-5
KK_EPS = 1e-12
ATTN_SCALE = HEAD_DIM ** -0.5
BAND = 128
VMEM_LIMIT = 56 * 1024 * 1024


def _cparams(*sem):
    return pltpu.CompilerParams(dimension_semantics=sem, vmem_limit_bytes=VMEM_LIMIT)


def _rms(x, g):
    return x * lax.rsqrt(jnp.mean(x * x, axis=-1, keepdims=True) + NORM_EPS) * g


def _mm(a, b):
    return jnp.dot(a.astype(BF16), b.astype(BF16), preferred_element_type=F32)


def _mm_nt(a, b):
    return lax.dot_general(a.astype(BF16), b.astype(BF16), (((1,), (1,)), ((), ())),
                           preferred_element_type=F32)


def _mm_tn(a, b):
    return lax.dot_general(a.astype(BF16), b.astype(BF16), (((0,), (0,)), ((), ())),
                           preferred_element_type=F32)


def _sigmoid(x):
    return 1.0 / (1.0 + jnp.exp(-x))


def _head_lane_masks(width):
    lane = lax.broadcasted_iota(jnp.int32, (1, width), 1)
    return [(lane >= h * HEAD_DIM) & (lane < (h + 1) * HEAD_DIM) for h in range(width // HEAD_DIM)]


def _row_tile(n, want):
    t = min(n, want)
    assert n % t == 0, (n, t)
    return t


def _rotate_heads(y, cos, sin_signed):
    width = y.shape[1]
    cos = jnp.concatenate([cos] * (width // LANES), axis=1)
    sin_signed = jnp.concatenate([sin_signed] * (width // LANES), axis=1)
    lane = lax.broadcasted_iota(jnp.int32, y.shape, 1)
    first_half = (lane % HEAD_DIM) < (HEAD_DIM // 2)
    partner = jnp.where(first_half, pltpu.roll(y, width - HEAD_DIM // 2, axis=1),
                        pltpu.roll(y, HEAD_DIM // 2, axis=1))
    return y * cos + partner * sin_signed


def _to_class_view(y, o_ref, scr, dil):
    rows = y.shape[0] // dil
    for half in range(COL_TILE // LANES):
        scr[half] = y[:, half * LANES:(half + 1) * LANES]
    for cls in range(dil):
        for half in range(COL_TILE // LANES):
            lane0 = cls * COL_TILE + half * LANES
            o_ref[:, lane0:lane0 + LANES] = scr[half, pl.ds(cls, rows, stride=dil), :]


def _from_class_view(ref, scr, dil):
    rows = ref.shape[0]
    for cls in range(dil):
        for half in range(COL_TILE // LANES):
            lane0 = cls * COL_TILE + half * LANES
            scr[half, pl.ds(cls, rows, stride=dil), :] = ref[:, lane0:lane0 + LANES]
    return jnp.concatenate([scr[half] for half in range(COL_TILE // LANES)], axis=1)


def _class_view_scratch(tm):
    return pltpu.VMEM((COL_TILE // LANES, tm, LANES), F32)


def _norm_proj_kernel(x_ref, g_ref, w_ref, cos_ref, sin_ref, *refs, n_rot, dils):
    o_refs, scratch = refs[:len(dils)], list(refs[len(dils):])
    xn = _rms(x_ref[...], g_ref[...]).astype(BF16)
    width = w_ref.shape[1] // len(dils)
    for j, (o_ref, dil) in enumerate(zip(o_refs, dils)):
        y = jnp.dot(xn, w_ref[:, j * width:(j + 1) * width], preferred_element_type=F32)
        if j < n_rot:
            y = _rotate_heads(y, cos_ref[...], sin_ref[...])
        if dil == 1:
            o_ref[...] = y
        else:
            _to_class_view(y, o_ref, scratch.pop(0), dil)


def _resident(shape):
    return pl.BlockSpec(shape, lambda *_: (0,) * len(shape), pipeline_mode=pl.Buffered(1))


def _layer_resident(stacked, layer):
    shape = stacked.shape[1:]
    return pl.BlockSpec((None,) + shape, lambda *_: (layer,) + (0,) * len(shape),
                        pipeline_mode=pl.Buffered(1))


def _norm_proj(x, g, w_bf16, cos, sin_signed, seq_len, *, n_rot, out_width=COL_TILE, dils=None, tm_want=512):
    n, d = x.shape
    cols = w_bf16.shape[1]
    n_out = cols // out_width
    dils = tuple(dils) if dils is not None else (1,) * n_out
    assert len(dils) == n_out and (out_width == COL_TILE or set(dils) == {1})
    tm = _row_tile(n, tm_want)
    if tm <= seq_len:
        assert seq_len % tm == 0
        per_seq = seq_len // tm
        tab_map = lambda i: (i % per_seq, 0)
    else:
        assert tm % seq_len == 0
        cos = jnp.tile(cos, (tm // seq_len, 1))
        sin_signed = jnp.tile(sin_signed, (tm // seq_len, 1))
        tab_map = lambda i: (0, 0)
    assert all(tm % (8 * dil) == 0 or dil == 1 for dil in dils)
    return pl.pallas_call(
        functools.partial(_norm_proj_kernel, n_rot=n_rot, dils=dils),
        grid=(n // tm,),
        in_specs=[
            pl.BlockSpec((tm, d), lambda i: (i, 0)),
            _resident((1, d)),
            _resident(w_bf16.shape),
            pl.BlockSpec((tm, LANES), tab_map),
            pl.BlockSpec((tm, LANES), tab_map),
        ],
        out_specs=[pl.BlockSpec((tm // dil, dil * out_width), lambda i: (i, 0)) for dil in dils],
        out_shape=[jax.ShapeDtypeStruct((n // dil, dil * out_width), F32) for dil in dils],
        scratch_shapes=[_class_view_scratch(tm) for dil in dils if dil > 1],
        compiler_params=_cparams("parallel"),
        name="norm_proj",
    )(x, g.reshape(1, d), w_bf16, cos, sin_signed)


def _rwkv_proj_kernel(*refs, mix, has_vfirst, tm, seq_len, tail):
    it = iter(refs)
    h_ref = next(it)
    hp_ref = next(it) if seq_len >= tm else None
    sh_ref = next(it)
    g_ref, mu_ref, win_ref = next(it), next(it), next(it)
    w0_ref, w1_ref, w2_ref = next(it), next(it), next(it)
    a0_ref, a1_ref, a2_ref = next(it), next(it), next(it)
    g1_ref, g2_ref = next(it), next(it)
    if has_vfirst:
        vf_ref, v0_ref, v1_ref, v2_ref = next(it), next(it), next(it), next(it)
    r_ref, lw_ref, k_ref, v_ref, a_ref, gate_ref, qm_ref, tail_ref = (next(it) for _ in range(8))

    i = pl.program_id(0)
    gain = g_ref[...]
    xn = _rms(h_ref[...], gain)
    row = lax.broadcasted_iota(jnp.int32, (tm, 1), 0)
    rolled = pltpu.roll(xn, 1, axis=0)
    if seq_len >= tm:
        prev_tile_last = _rms(hp_ref[...], gain)[7:8, :]
        at_seq_start = (i * tm) % seq_len == 0
        first = jnp.where(at_seq_start, sh_ref[0][7:8, :], prev_tile_last)
        x_prev = jnp.where(row == 0, first, rolled)
    else:
        x_prev = jnp.where(row % seq_len == 0, sh_ref[...], rolled)
    xx = x_prev - xn
    mu = mu_ref[...]

    def mixed(idx):
        return (xn + xx * mu[idx:idx + 1, :]).astype(BF16)

    qm_ref[...] = jnp.dot(xn.astype(BF16), win_ref[:, 3 * mix:], preferred_element_type=F32)
    tail_ref[0] = xn[tm - tail:, :]
    r_ref[...] = jnp.dot(mixed(0), win_ref[:, 0:mix], preferred_element_type=F32)
    w_lin = w0_ref[...] + _mm(jnp.tanh(_mm(mixed(1), w1_ref[...])), w2_ref[...])
    lw_ref[...] = -math.exp(-0.5) * _sigmoid(w_lin)
    k_ref[...] = jnp.dot(mixed(2), win_ref[:, mix:2 * mix], preferred_element_type=F32)
    xv = mixed(3)
    v = jnp.dot(xv, win_ref[:, 2 * mix:3 * mix], preferred_element_type=F32)
    if has_vfirst:
        v = v + (vf_ref[...] - v) * _sigmoid(v0_ref[...] + _mm(_mm(xv, v1_ref[...]), v2_ref[...]))
    v_ref[...] = v
    a_ref[...] = _sigmoid(a0_ref[...] + _mm(_mm(mixed(4), a1_ref[...]), a2_ref[...]))
    gate_ref[...] = _mm(_sigmoid(_mm(mixed(5), g1_ref[...])), g2_ref[...])


def _rwkv_proj(h, shift, seq_len, W, li, v_first, *, tm_want=512):
    n, d = h.shape
    mix = W['a_w0'].shape[1]
    tm = _row_tile(n, tm_want)
    n_tiles = n // tm
    has_vfirst = v_first is not None
    row_spec = lambda w: pl.BlockSpec((tm, w), lambda i: (i, 0))
    full = lambda a: _resident(a.shape)

    args, specs = [h], [row_spec(d)]
    if seq_len >= tm:
        assert seq_len % tm == 0
        tail = 8
        args.append(h)
        specs.append(pl.BlockSpec((8, d), lambda i: (jnp.maximum(i * (tm // 8) - 1, 0), 0)))
        args.append(jnp.broadcast_to(shift[:, None, :], (shift.shape[0], 8, d)))
        specs.append(pl.BlockSpec((1, 8, d), lambda i: ((i * tm) // seq_len, 0, 0)))
    else:
        assert tm % seq_len == 0
        tail = tm
        args.append(jnp.repeat(shift, seq_len, axis=0))
        specs.append(row_spec(d))
    lead = [W['g_pre_mix'][li].reshape(1, d), W['a_mu'][li]]
    small = [W['a_w0'][li].reshape(1, mix), W['a_w1_bf16'][li], W['a_w2_bf16'][li],
             W['a_a0'][li].reshape(1, mix), W['a_a1_bf16'][li], W['a_a2_bf16'][li],
             W['a_g1_bf16'][li], W['a_g2_bf16'][li]]
    args += lead + [W['a_w_in_bf16']] + small
    specs += [full(a) for a in lead] + [_layer_resident(W['a_w_in_bf16'], li)] + [full(a) for a in small]
    if has_vfirst:
        vi = li - 1
        extra = [W['a_v0'][vi].reshape(1, mix), W['a_v1_bf16'][vi], W['a_v2_bf16'][vi]]
        args += [v_first] + extra
        specs += [row_spec(mix)] + [full(a) for a in extra]

    wide = jax.ShapeDtypeStruct((n, mix), F32)
    out_shape = [wide] * 6 + [jax.ShapeDtypeStruct((n, MEM_WIDTH), F32),
                              jax.ShapeDtypeStruct((n_tiles, tail, d), F32)]
    out_specs = [row_spec(mix)] * 6 + [row_spec(MEM_WIDTH),
                                       pl.BlockSpec((1, tail, d), lambda i: (i, 0, 0))]
    r, lw, k, v, a, gate, q_mem, xn_tail = pl.pallas_call(
        functools.partial(_rwkv_proj_kernel, mix=mix, has_vfirst=has_vfirst, tm=tm,
                          seq_len=seq_len, tail=tail),
        grid=(n_tiles,),
        in_specs=specs,
        out_specs=out_specs,
        out_shape=out_shape,
        compiler_params=_cparams("parallel"),
        name="rwkv_proj",
    )(*args)
    xn_rows = xn_tail.reshape(n_tiles * tail, d)
    last = xn_rows.reshape(-1, seq_len if seq_len < tm else tail * (seq_len // tm), d)[:, -1]
    return r, lw, k, v, a, gate, q_mem, last


def _head_sums(x, ones_bd):
    hi = x.astype(BF16)
    rest = x - hi.astype(F32)
    mid = rest.astype(BF16)
    lo = (rest - mid.astype(F32)).astype(BF16)
    tiles = []
    for t in range(x.shape[1] // COL_TILE):
        s = slice(t * COL_TILE, (t + 1) * COL_TILE)
        tiles.append(jnp.dot(hi[:, s], ones_bd, preferred_element_type=F32)
                     + jnp.dot(mid[:, s], ones_bd, preferred_element_type=F32)
                     + jnp.dot(lo[:, s], ones_bd, preferred_element_type=F32))
    return jnp.concatenate(tiles, axis=1)


def _wkv_kernel(r_ref, lw_ref, k_ref, v_ref, a_ref, gate_ref, s0_ref, kkp_ref, kap_ref, rkp_ref,
                lnw_ref, lnb_ref, o_ref, sout_ref, s_scr, *, chunk, heads):
    c = pl.program_id(1)
    rows = r_ref.shape[0]
    mix = r_ref.shape[2]

    @pl.when(c == 0)
    def _():
        for b in range(rows):
            s_scr[b * heads:(b + 1) * heads] = s0_ref[b]

    side_by_side = lambda ref: jnp.concatenate([ref[b] for b in range(rows)], axis=1)
    per_row = lambda ref: jnp.concatenate([ref[...]] * rows, axis=1)
    r, lw, k, v, a = (side_by_side(ref) for ref in (r_ref, lw_ref, k_ref, v_ref, a_ref))
    row = lax.broadcasted_iota(jnp.int32, (chunk, chunk), 0)
    col = lax.broadcasted_iota(jnp.int32, (chunk, chunk), 1)
    strict = col < row
    incl = col <= row
    tri = incl.astype(BF16)
    lw_hi = lw.astype(BF16)
    lw_rest = lw - lw_hi.astype(F32)
    lw_mid = lw_rest.astype(BF16)
    lw_lo = (lw_rest - lw_mid.astype(F32)).astype(BF16)
    c_inc = (jnp.dot(tri, lw_hi, preferred_element_type=F32)
             + jnp.dot(tri, lw_mid, preferred_element_type=F32)
             + jnp.dot(tri, lw_lo, preferred_element_type=F32))
    c_last = c_inc[chunk - 1:chunk, :]
    e_inc = jnp.exp(c_inc)
    e_exc = jnp.exp(c_inc - lw)
    e_neg = jnp.exp(-c_inc)
    e_tail = jnp.exp(c_last - c_inc)
    p_last = jnp.exp(c_last)

    bd_r = lax.broadcasted_iota(jnp.int32, (COL_TILE, COL_TILE), 0) // HEAD_DIM
    bd_c = lax.broadcasted_iota(jnp.int32, (COL_TILE, COL_TILE), 1) // HEAD_DIM
    ones_bd = (bd_r == bd_c).astype(BF16)

    kk_raw = k * per_row(kkp_ref)
    kk = kk_raw * (1.0 / jnp.maximum(jnp.sqrt(_head_sums(kk_raw * kk_raw, ones_bd)), KK_EPS))
    kka = kk * a
    k2 = k * (1.0 + (a - 1.0) * per_row(kap_ref))
    bonus = _head_sums(r * k2 * per_row(rkp_ref), ones_bd) * v
    a_hat_f = -kk * e_exc
    r_hat_f = r * e_inc
    b_chk_f = kka * e_neg
    k_chk_f = k2 * e_neg
    b_til_f = kka * e_tail
    k_til_f = k2 * e_tail

    eye_h = (lax.broadcasted_iota(jnp.int32, (HEAD_DIM, HEAD_DIM), 0)
             == lax.broadcasted_iota(jnp.int32, (HEAD_DIM, HEAD_DIM), 1))
    n_levels = max(1, int(math.log2(chunk)))
    hs = range(rows * heads)
    sl = [slice(h * HEAD_DIM, (h + 1) * HEAD_DIM) for h in hs]
    a_hat = [a_hat_f[:, s] for s in sl]
    r_hat = [r_hat_f[:, s] for s in sl]
    v_h = [v[:, s] for s in sl]

    amat = [_mm_nt(jnp.concatenate([a_hat[h], r_hat[h]], axis=0),
                   jnp.concatenate([b_chk_f[:, sl[h]], k_chk_f[:, sl[h]]], axis=0)) for h in hs]
    a_ab = [jnp.where(strict, m[:chunk, :chunk], 0.0) for m in amat]
    a_ak = [jnp.where(strict, m[:chunk, chunk:], 0.0) for m in amat]
    a_rb = [jnp.where(incl, m[chunk:, :chunk], 0.0) for m in amat]
    a_rk = [jnp.where(incl, m[chunk:, chunk:], 0.0) for m in amat]
    av = [_mm(jnp.concatenate([a_ak[h], a_rk[h]], axis=0), v_h[h]) for h in hs]

    w12 = [jnp.concatenate([a_hat[h], av[h][:chunk]], axis=1) for h in hs]
    power = a_ab
    for level in range(n_levels):
        w12 = [w12[h] + _mm(power[h], w12[h]) for h in hs]
        if level + 1 < n_levels:
            power = [_mm(p, p) for p in power]

    rx = [_mm(a_rb[h], w12[h]) + jnp.concatenate([r_hat[h], av[h][chunk:]], axis=1) for h in hs]
    gh = [_mm_tn(w12[h], b_til_f[:, sl[h]]) for h in hs]
    vk = [_mm_tn(v_h[h], k_til_f[:, sl[h]]) for h in hs]
    g_t = [gh[h][:HEAD_DIM] + jnp.where(eye_h, p_last[:, sl[h]], 0.0) for h in hs]
    h_t = [gh[h][HEAD_DIM:] + vk[h] for h in hs]

    s_prev = [s_scr[h] for h in hs]
    o_h = [_mm_nt(rx[h][:, :HEAD_DIM], s_prev[h]) + rx[h][:, HEAD_DIM:] for h in hs]
    s_new = [_mm(s_prev[h], g_t[h]) + h_t[h] for h in hs]
    for h in hs:
        s_scr[h] = s_new[h]

    o = jnp.concatenate(o_h, axis=1)
    cen = o - _head_sums(o, ones_bd) * (1.0 / HEAD_DIM)
    var = _head_sums(cen * cen, ones_bd) * (1.0 / HEAD_DIM)
    o = cen * lax.rsqrt(var + LNX_EPS) * per_row(lnw_ref) + per_row(lnb_ref)
    o = (o + bonus) * side_by_side(gate_ref)
    for b in range(rows):
        o_ref[b] = o[:, b * mix:(b + 1) * mix]

    @pl.when(c == pl.num_programs(1) - 1)
    def _():
        for b in range(rows):
            sout_ref[b] = s_scr[b * heads:(b + 1) * heads]


def _wkv(r, lw, k, v, a, gate, s0, W, li, batch, seq_len, chunk):
    mix = r.shape[1]
    heads = mix // HEAD_DIM
    if seq_len % chunk:
        raise ValueError("sequence length must be a multiple of the chunk")
    if mix % COL_TILE:
        raise ValueError("mixer width must be a whole number of 4-head tiles")
    n_chunks = seq_len // chunk
    seq = lambda t: t.reshape(batch, seq_len, mix)
    rows = 2 if batch % 2 == 0 else 1
    blk = pl.BlockSpec((rows, chunk, mix), lambda b, c: (b, c, 0))
    st = pl.BlockSpec((rows, heads, HEAD_DIM, HEAD_DIM), lambda b, c: (b, 0, 0, 0))
    par = pl.BlockSpec((1, mix), lambda b, c: (0, 0))
    params = [W['a_k_k'][li], W['a_k_a'][li], W['a_r_k'][li], W['a_lnx_w'][li], W['a_lnx_b'][li]]
    o, s_out = pl.pallas_call(
        functools.partial(_wkv_kernel, chunk=chunk, heads=heads),
        grid=(batch // rows, n_chunks),
        in_specs=[blk] * 6 + [st] + [par] * 5,
        out_specs=[blk, st],
        out_shape=[jax.ShapeDtypeStruct((batch, seq_len, mix), F32),
                   jax.ShapeDtypeStruct(s0.shape, F32)],
        scratch_shapes=[pltpu.VMEM((rows * heads, HEAD_DIM, HEAD_DIM), F32)],
        compiler_params=_cparams("parallel", "arbitrary"),
        name="wkv7",
    )(seq(r), seq(lw), seq(k), seq(v), seq(a), seq(gate), s0, *[p.reshape(1, mix) for p in params])
    return o.reshape(batch * seq_len, mix), s_out


def _mem_attn_kernel(q_ref, kv_ref, o_ref):
    q = q_ref[0]
    tq = q.shape[0]
    mk = kv_ref[0, :, :MEM_WIDTH].astype(BF16)
    mv = kv_ref[0, :, MEM_WIDTH:].astype(BF16)
    masks = _head_lane_masks(MEM_WIDTH)
    stacked = jnp.concatenate([jnp.where(m, q, 0.0) for m in masks], axis=0)
    s = _mm_nt(stacked, mk) * ATTN_SCALE
    p = jnp.exp(s - jnp.max(s, axis=-1, keepdims=True))
    p = p * (1.0 / jnp.sum(p, axis=-1, keepdims=True))
    pv = _mm(p, mv)
    out = jnp.zeros(q.shape, F32)
    for h, m in enumerate(masks):
        out = out + jnp.where(m, pv[h * tq:(h + 1) * tq, :], 0.0)
    o_ref[0] = out


def _mem_attn(q_mem, mem_kv, batch, seq_len, *, tq_want=512):
    tq = _row_tile(seq_len, tq_want)
    m = mem_kv.shape[1]
    out = pl.pallas_call(
        _mem_attn_kernel,
        grid=(batch, seq_len // tq),
        in_specs=[pl.BlockSpec((1, tq, MEM_WIDTH), lambda b, t: (b, t, 0)),
                  pl.BlockSpec((1, m, 2 * MEM_WIDTH), lambda b, t: (b, 0, 0))],
        out_specs=pl.BlockSpec((1, tq, MEM_WIDTH), lambda b, t: (b, t, 0)),
        out_shape=jax.ShapeDtypeStruct((batch, seq_len, MEM_WIDTH), F32),
        compiler_params=_cparams("parallel", "parallel"),
        name="mem_attn",
    )(q_mem.reshape(batch, seq_len, MEM_WIDTH), mem_kv)
    return out.reshape(batch * seq_len, MEM_WIDTH)


def _out_ffn_kernel(*refs, dils, mix_width, d_ff, tf):
    n_groups = len(dils)
    n_scratch = 2 * sum(dil > 1 for dil in dils)
    scratch = list(refs[len(refs) - n_scratch:])
    it = iter(refs[:len(refs) - n_scratch])
    if n_groups:
        read = lambda ref, dil: ref[...] if dil == 1 else _from_class_view(ref, scratch.pop(0), dil)
        og = [read(next(it), dil) for dil in dils]
        lse = [read(next(it), dil) for dil in dils]
        top = functools.reduce(jnp.maximum, lse)
        wgt = [jnp.exp(s - top) for s in lse]
        o_mix = sum(w * o for w, o in zip(wgt, og)) / sum(wgt)
    else:
        o_mix = next(it)[...]
    om_ref, wout_ref, gpm_ref, h_ref, gpre_ref, win_ref, wo_ref, gpost_ref, o_ref, acc_ref = it
    mixed = (jnp.dot(o_mix.astype(BF16), wout_ref[:mix_width, :], preferred_element_type=F32)
             + jnp.dot(om_ref[...].astype(BF16), wout_ref[mix_width:, :], preferred_element_type=F32))
    h1 = h_ref[...] + _rms(mixed, gpm_ref[...])
    xn = _rms(h1, gpre_ref[...]).astype(BF16)
    def gate_up(j):
        return (jnp.dot(xn, win_ref[:, j * tf:(j + 1) * tf], preferred_element_type=F32),
                jnp.dot(xn, win_ref[:, d_ff + j * tf:d_ff + (j + 1) * tf], preferred_element_type=F32))

    n_ff = d_ff // tf
    ahead = gate_up(0)
    for j in range(n_ff):
        gate, up = ahead
        if j + 1 < n_ff:
            ahead = gate_up(j + 1)
        act = (gate * _sigmoid(gate) * up).astype(BF16)
        part = jnp.dot(act, wo_ref[j * tf:(j + 1) * tf, :], preferred_element_type=F32)
        if j == 0:
            acc_ref[...] = part
        else:
            acc_ref[...] += part
    o_ref[...] = h1 + _rms(acc_ref[...], gpost_ref[...])


def _out_ffn(o_mix, lse, dils, o_mem, h, W, l, w_out_stack, w_out_layer, *, tm_want=512, tf=256):
    n, d = h.shape
    tm = _row_tile(n, tm_want)
    w_in, w_o = W['w_ffn_in_bf16'], W['w_ffn_out_bf16']
    d_ff = w_o.shape[1]
    assert d_ff % tf == 0
    mix_width = w_out_stack.shape[1] - MEM_WIDTH
    row = lambda w: pl.BlockSpec((tm, w), lambda i: (i, 0))
    if lse is not None:
        lead = list(o_mix) + list(lse)
        lead_specs = [pl.BlockSpec((tm // dil, dil * COL_TILE), lambda i: (i, 0)) for dil in dils] * 2
    else:
        lead, lead_specs, dils = [o_mix], [row(o_mix.shape[1])], ()
    vec = lambda g: g.reshape(1, d)
    return pl.pallas_call(
        functools.partial(_out_ffn_kernel, dils=tuple(dils), mix_width=mix_width, d_ff=d_ff, tf=tf),
        grid=(n // tm,),
        in_specs=lead_specs + [
            row(MEM_WIDTH), _layer_resident(w_out_stack, w_out_layer), _resident((1, d)), row(d),
            _resident((1, d)), _layer_resident(w_in, l), _layer_resident(w_o, l), _resident((1, d))],
        out_specs=row(d),
        out_shape=jax.ShapeDtypeStruct((n, d), F32),
        scratch_shapes=[pltpu.VMEM((tm, d), F32)] + [_class_view_scratch(tm) for dil in dils if dil > 1] * 2,
        compiler_params=_cparams("parallel"),
        name="out_ffn",
    )(*lead, o_mem, w_out_stack, vec(W['g_post_mix'][l]), h, vec(W['g_pre_ffn'][l]), w_in, w_o,
      vec(W['g_post_ffn'][l]))


def _dil_prompt_kernel(q_ref, kc_ref, kh_ref, vc_ref, vh_ref, o_ref, lse_ref, kx_ref, vx_ref, *, tile):
    t = pl.program_id(2)
    kx_ref[0:BAND, :] = kh_ref[0].astype(BF16)
    kx_ref[BAND:, :] = kc_ref[0].astype(BF16)
    vx_ref[0:BAND, :] = vh_ref[0].astype(BF16)
    vx_ref[BAND:, :] = vc_ref[0].astype(BF16)

    rows = HEADS_PER_TILE * BAND
    qi = lax.broadcasted_iota(jnp.int32, (rows, 2 * BAND), 0) % BAND
    kj = lax.broadcasted_iota(jnp.int32, (rows, 2 * BAND), 1)
    band = (kj >= qi) & (kj <= qi + BAND)
    masks = _head_lane_masks(COL_TILE)

    def block(blk, carry):
        start = pl.multiple_of(blk * BAND, BAND)
        q = q_ref[0, pl.ds(start, BAND), :] * ATTN_SCALE
        keys = kx_ref[pl.ds(start, 2 * BAND), :]
        vals = vx_ref[pl.ds(start, 2 * BAND), :]
        lo = jnp.where((t == 0) & (blk == 0), BAND, 0)
        valid = band & (kj >= lo)
        stacked = jnp.concatenate([jnp.where(m, q, 0.0) for m in masks], axis=0)
        s = jnp.where(valid, _mm_nt(stacked, keys), -jnp.inf)
        top = jnp.max(s, axis=-1, keepdims=True)
        p = jnp.exp(s - top)
        den = jnp.sum(p, axis=-1, keepdims=True)
        pv = _mm(p, vals) * (1.0 / den)
        lse_rows = top + jnp.log(den)
        out = jnp.zeros((BAND, COL_TILE), F32)
        lse = jnp.zeros((BAND, COL_TILE), F32)
        for h, m in enumerate(masks):
            out = out + jnp.where(m, pv[h * BAND:(h + 1) * BAND, :], 0.0)
            lse = lse + jnp.where(m, lse_rows[h * BAND:(h + 1) * BAND, :], 0.0)
        o_ref[0, pl.ds(start, BAND), :] = out
        lse_ref[0, pl.ds(start, BAND), :] = lse
        return carry

    lax.fori_loop(0, tile // BAND, block, 0, unroll=2)


def _dil_prompt_group(q, k, v, batch, seq_len, dil):
    cls_len = seq_len // dil
    tile = min(cls_len, 1024)
    assert cls_len % tile == 0 and tile % BAND == 0
    n_t = cls_len // tile
    view = lambda x: x.reshape(batch, cls_len, dil * COL_TILE)
    cur = pl.BlockSpec((1, tile, COL_TILE), lambda b, c, t: (b, t, c))
    halo = pl.BlockSpec((1, BAND, COL_TILE),
                        lambda b, c, t: (b, jnp.maximum(t * (tile // BAND) - 1, 0), c))
    out_shape = jax.ShapeDtypeStruct((batch, cls_len, dil * COL_TILE), F32)
    o, lse = pl.pallas_call(
        functools.partial(_dil_prompt_kernel, tile=tile),
        grid=(batch, dil, n_t),
        in_specs=[cur, cur, halo, cur, halo],
        out_specs=[cur, cur],
        out_shape=[out_shape, out_shape],
        scratch_shapes=[pltpu.VMEM((tile + BAND, COL_TILE), BF16)] * 2,
        compiler_params=_cparams("parallel", "parallel", "arbitrary"),
        name="dil_attn_prompt",
    )(view(q), view(k), view(k), view(v), view(v))
    flat = lambda x: x.reshape(batch * cls_len, dil * COL_TILE)
    return flat(o), flat(lse)


def _dil_sample_kernel(*refs, seq_len, write_cache):
    q_refs = refs[:N_GROUPS]
    k_refs = refs[N_GROUPS:2 * N_GROUPS]
    v_refs = refs[2 * N_GROUPS:3 * N_GROUPS]
    cache_refs = refs[3 * N_GROUPS:4 * N_GROUPS]
    o_ref = refs[4 * N_GROUPS]
    new_refs = refs[4 * N_GROUPS + 1:]
    masks = _head_lane_masks(COL_TILE)
    rows = HEADS_PER_TILE * seq_len

    run_top = jnp.full((rows, 1), -jnp.inf, F32)
    run_den = jnp.zeros((rows, 1), F32)
    run_acc = jnp.zeros((rows, COL_TILE), F32)
    for g, (win, dil) in enumerate(DIL_GROUPS):
        cache = cache_refs[g]
        length = cache.shape[1]
        q = q_refs[g][...] * ATTN_SCALE
        stacked = jnp.concatenate([jnp.where(m, q, 0.0) for m in masks], axis=0)
        k_new, v_new = k_refs[g][...], v_refs[g][...]
        ck = cache[0, :, :COL_TILE]
        cv = cache[0, :, COL_TILE:]
        s_c = _mm_nt(stacked, ck)
        s_n = _mm_nt(stacked, k_new)
        tq_c = lax.broadcasted_iota(jnp.int32, s_c.shape, 0) % seq_len
        dist_c = length + tq_c - lax.broadcasted_iota(jnp.int32, s_c.shape, 1)
        ok_c = (dist_c % dil == 0) & (dist_c <= win)
        tq_n = lax.broadcasted_iota(jnp.int32, s_n.shape, 0) % seq_len
        dist_n = tq_n - lax.broadcasted_iota(jnp.int32, s_n.shape, 1)
        ok_n = (dist_n >= 0) & (dist_n % dil == 0)
        s_c = jnp.where(ok_c, s_c, -jnp.inf)
        s_n = jnp.where(ok_n, s_n, -jnp.inf)
        top = jnp.maximum(jnp.maximum(jnp.max(s_c, axis=-1, keepdims=True),
                                      jnp.max(s_n, axis=-1, keepdims=True)), run_top)
        p_c = jnp.exp(s_c - top)
        p_n = jnp.exp(s_n - top)
        scale = jnp.exp(run_top - top)
        run_den = run_den * scale + jnp.sum(p_c, axis=-1, keepdims=True) + jnp.sum(p_n, axis=-1, keepdims=True)
        run_acc = run_acc * scale + _mm(p_c, cv) + _mm(p_n, v_new)
        run_top = top
        if write_cache:
            new = new_refs[g]
            if length > seq_len:
                new[0, 0:length - seq_len, :] = cache[0, seq_len:, :]
            new[0, length - seq_len:, :COL_TILE] = k_new
            new[0, length - seq_len:, COL_TILE:] = v_new
    res = run_acc / run_den
    out = jnp.zeros((seq_len, COL_TILE), F32)
    for h, m in enumerate(masks):
        out = out + jnp.where(m, res[h * seq_len:(h + 1) * seq_len, :], 0.0)
    o_ref[...] = out


def _dil_sample(q_slabs, kv_slabs, caches, batch, seq_len, write_cache):
    n = batch * seq_len
    in_specs = [pl.BlockSpec((seq_len, COL_TILE), lambda b: (b, 0))] * (3 * N_GROUPS)
    cache_specs = [pl.BlockSpec((1,) + c.shape[1:], lambda b: (b, 0, 0)) for c in caches]
    out_shape = [jax.ShapeDtypeStruct((n, COL_TILE), F32)]
    out_specs = [pl.BlockSpec((seq_len, COL_TILE), lambda b: (b, 0))]
    if write_cache:
        out_shape += [jax.ShapeDtypeStruct(c.shape, F32) for c in caches]
        out_specs += cache_specs
    res = pl.pallas_call(
        functools.partial(_dil_sample_kernel, seq_len=seq_len, write_cache=write_cache),
        grid=(batch,),
        in_specs=in_specs + cache_specs,
        out_specs=out_specs,
        out_shape=out_shape,
        compiler_params=_cparams("parallel"),
        name="dil_attn_sample",
    )(*q_slabs[:N_GROUPS], *kv_slabs, *caches)
    return res[0], list(res[1:])


def _rope_tables(pos):
    half = HEAD_DIM // 2
    inv = ROPE_THETA ** (-jnp.arange(half, dtype=F32) / half)
    ang = pos.astype(F32)[:, None] * inv[None, :]
    cos, sin = jnp.cos(ang), jnp.sin(ang)
    cos = jnp.tile(jnp.concatenate([cos, cos], axis=-1), (1, LANES // HEAD_DIM))
    sin_signed = jnp.tile(jnp.concatenate([-sin, sin], axis=-1), (1, LANES // HEAD_DIM))
    return cos, sin_signed


def _trunk(h, pos, mem_kv_all, wkv0, shift0, caches, W, batch, seq_len, chunk):
    depth = W['g_pre_mix'].shape[0]
    n_a = W['a_mu'].shape[0]
    cos, sin_signed = _rope_tables(pos)
    new_wkv, new_shift, new_caches = [], [], None
    v_first = None
    kv_tiles = None
    for l in range(depth):
        if l < n_a:
            r, lw, k, v, a, gate, q_mem, xn_last = _rwkv_proj(h, shift0[l], seq_len, W, l, v_first)
            if l == 0:
                v_first = v
            o_mix, s_out = _wkv(r, lw, k, v, a, gate, wkv0[l], W, l, batch, seq_len, chunk)
            new_wkv.append(s_out)
            new_shift.append(xn_last)
            lse = None
            w_out, w_out_layer = W['a_w_out_bf16'], l
        else:
            j = l - n_a
            dils = tuple(dil for _, dil in DIL_GROUPS) if caches is None else (1,) * N_GROUPS
            if kv_tiles is None:
                kv_tiles = _norm_proj(h, W['g_kv'], W['w_kv_bf16'], cos, sin_signed, seq_len, n_rot=N_GROUPS,
                                      dils=dils * 2)
            proj = _norm_proj(h, W['g_pre_mix'][l], W['b_w_in_bf16'][j], cos, sin_signed, seq_len,
                              n_rot=N_GROUPS, dils=dils + (1,))
            q_mem = proj[N_GROUPS]
            if caches is None:
                outs = [_dil_prompt_group(proj[g], kv_tiles[g], kv_tiles[N_GROUPS + g], batch, seq_len, dil)
                        for g, dil in enumerate(dils)]
                o_mix = [o for o, _ in outs]
                lse = [s for _, s in outs]
            else:
                o_mix, written = _dil_sample(proj, kv_tiles, caches, batch, seq_len, new_caches is None)
                if new_caches is None:
                    new_caches = written
                lse = None
            w_out, w_out_layer = W['b_w_out_bf16'], j
        o_mem = _mem_attn(q_mem, mem_kv_all[l], batch, seq_len)
        h = _out_ffn(o_mix, lse, dils if lse is not None else (), o_mem, h, W, l, w_out, w_out_layer)
    return h, jnp.stack(new_wkv), jnp.stack(new_shift), kv_tiles, new_caches


def _prompt_windows(kv_tiles, batch, seq_len):
    bufs = []
    for g, (win, dil) in enumerate(DIL_GROUPS):
        length = min(win, PAST_LEN)
        assert seq_len >= length and length % dil == 0
        rows = seq_len // dil

        def tail(t):
            t = t.reshape(batch, rows, dil, HEADS_PER_TILE, HEAD_DIM)[:, rows - length // dil:]
            return t.reshape(batch, length, HEADS_PER_TILE, HEAD_DIM)

        bufs.append(jnp.stack([tail(kv_tiles[g]), tail(kv_tiles[N_GROUPS + g])], axis=2))
    return bufs


def kernel(x_prompt, x_sample, state_wkv, state_shift, cache_win_g0, cache_win_g1, cache_win_g2, cache_mem, mem_prompt, g_pre_mix, g_post_mix, g_pre_ffn, g_post_ffn, g_mem, w_mem_kv, w_ffn_in, w_ffn_out, a_mu, a_w_in, a_w0, a_w1, a_w2, a_a0, a_a1, a_a2, a_v0, a_v1, a_v2, a_g1, a_g2, a_k_k, a_k_a, a_r_k, a_lnx_w, a_lnx_b, a_w_out, g_kv, w_kv, b_w_in, b_w_out):
    W = dict(g_pre_mix=g_pre_mix, g_post_mix=g_post_mix, g_pre_ffn=g_pre_ffn, g_post_ffn=g_post_ffn,
             a_mu=a_mu, a_w0=a_w0, a_a0=a_a0, a_v0=a_v0, a_k_k=a_k_k, a_k_a=a_k_a,
             a_r_k=a_r_k.reshape(a_r_k.shape[0], -1), a_lnx_w=a_lnx_w, a_lnx_b=a_lnx_b, g_kv=g_kv)
    for name, w in dict(w_ffn_in=w_ffn_in, w_ffn_out=w_ffn_out, a_w_in=a_w_in, a_w1=a_w1, a_w2=a_w2,
                        a_a1=a_a1, a_a2=a_a2, a_v1=a_v1, a_v2=a_v2, a_g1=a_g1, a_g2=a_g2,
                        a_w_out=a_w_out, w_kv=w_kv, b_w_in=b_w_in, b_w_out=b_w_out,
                        w_mem_kv=w_mem_kv).items():
        W[name + '_bf16'] = w.astype(BF16)

    bp, tp, d = x_prompt.shape
    bs, ts, _ = x_sample.shape
    depth = g_pre_mix.shape[0]
    n_a = a_mu.shape[0]
    heads = a_w0.shape[1] // HEAD_DIM
    m_tok = mem_prompt.shape[1]

    mem_rows = mem_prompt.reshape(bp * m_tok, d)
    no_tab = jnp.zeros((m_tok, LANES), F32)
    mem_kv_p = jnp.stack([
        _norm_proj(mem_rows, g_mem[l], W['w_mem_kv_bf16'][l], no_tab, no_tab, m_tok,
                   n_rot=0, out_width=2 * MEM_WIDTH)[0].reshape(bp, m_tok, 2 * MEM_WIDTH)
        for l in range(depth)])
    wkv_zero = jnp.zeros((n_a, bp, heads, HEAD_DIM, HEAD_DIM), F32)
    shift_zero = jnp.zeros((n_a, bp, d), F32)
    y_p, wkv_p, shift_p, kv_p, _ = _trunk(
        x_prompt.reshape(bp * tp, d), jnp.arange(tp, dtype=jnp.int32), mem_kv_p, wkv_zero, shift_zero,
        None, W, bp, tp, chunk=64)
    bufs_p = _prompt_windows(kv_p, bp, tp)

    caches = [c.reshape(bs, c.shape[1], 2 * COL_TILE) for c in (cache_win_g0, cache_win_g1, cache_win_g2)]
    mem_kv_s = cache_mem.reshape(depth, bs, m_tok, 2 * MEM_WIDTH)
    y_s, wkv_s, shift_s, _, bufs_s = _trunk(
        x_sample.reshape(bs * ts, d), PAST_LEN + jnp.arange(ts, dtype=jnp.int32), mem_kv_s, state_wkv,
        state_shift, caches, W, bs, ts, chunk=ts)
    bufs_s = [b.reshape(c.shape) for b, c in zip(bufs_s, (cache_win_g0, cache_win_g1, cache_win_g2))]

    return (y_p.reshape(bp, tp, d), y_s.reshape(bs, ts, d), wkv_p, shift_p, bufs_p[0], bufs_p[1], bufs_p[2],
            mem_kv_p.reshape(depth, bp, m_tok, 2, MEM_HEADS, HEAD_DIM),
            wkv_s, shift_s, bufs_s[0], bufs_s[1], bufs_s[2])
```

```python
import functools
import math

import jax
import jax.numpy as jnp
from jax import lax
from jax.experimental import pallas as pl
from jax.experimental.pallas import tpu as pltpu

F32 = jnp.float32
BF16 = jnp.bfloat16

HEAD_DIM = 64
LANES = 128
COL_TILE = 256
HEADS_PER_TILE = COL_TILE // HEAD_DIM
MEM_HEADS = 4
MEM_WIDTH = MEM_HEADS * HEAD_DIM
DIL_GROUPS = ((128, 1), (512, 4), (2048, 16))
N_GROUPS = len(DIL_GROUPS)
PAST_LEN = 8192
ROPE_THETA = 10000.0
NORM_EPS = 1e-6
LNX_EPS = HEAD_DIM * 1e-5
KK_EPS = 1e-12
ATTN_SCALE = HEAD_DIM ** -0.5
BAND = 128
VMEM_LIMIT = 56 * 1024 * 1024


def _cparams(*sem):
    return pltpu.CompilerParams(dimension_semantics=sem, vmem_limit_bytes=VMEM_LIMIT)


def _rms(x, g):
    return x * lax.rsqrt(jnp.mean(x * x, axis=-1, keepdims=True) + NORM_EPS) * g


def _mm(a, b):
    return jnp.dot(a.astype(BF16), b.astype(BF16), preferred_element_type=F32)


def _mm_nt(a, b):
    return lax.dot_general(a.astype(BF16), b.astype(BF16), (((1,), (1,)), ((), ())),
                           preferred_element_type=F32)


def _mm_tn(a, b):
    return lax.dot_general(a.astype(BF16), b.astype(BF16), (((0,), (0,)), ((), ())),
                           preferred_element_type=F32)


def _sigmoid(x):
    return 1.0 / (1.0 + jnp.exp(-x))


def _head_lane_masks(width):
    lane = lax.broadcasted_iota(jnp.int32, (1, width), 1)
    return [(lane >= h * HEAD_DIM) & (lane < (h + 1) * HEAD_DIM) for h in range(width // HEAD_DIM)]


def _row_tile(n, want):
    t = min(n, want)
    assert n % t == 0, (n, t)
    return t


def _rotate_heads(y, cos, sin_signed):
    width = y.shape[1]
    cos = jnp.concatenate([cos] * (width // LANES), axis=1)
    sin_signed = jnp.concatenate([sin_signed] * (width // LANES), axis=1)
    lane = lax.broadcasted_iota(jnp.int32, y.shape, 1)
    first_half = (lane % HEAD_DIM) < (HEAD_DIM // 2)
    partner = jnp.where(first_half, pltpu.roll(y, width - HEAD_DIM // 2, axis=1),
                        pltpu.roll(y, HEAD_DIM // 2, axis=1))
    return y * cos + partner * sin_signed


def _to_class_view(y, o_ref, scr, dil):
    rows = y.shape[0] // dil
    for half in range(COL_TILE // LANES):
        scr[half] = y[:, half * LANES:(half + 1) * LANES]
    for cls in range(dil):
        for half in range(COL_TILE // LANES):
            lane0 = cls * COL_TILE + half * LANES
            o_ref[:, lane0:lane0 + LANES] = scr[half, pl.ds(cls, rows, stride=dil), :]


def _from_class_view(ref, scr, dil):
    rows = ref.shape[0]
    for cls in range(dil):
        for half in range(COL_TILE // LANES):
            lane0 = cls * COL_TILE + half * LANES
            scr[half, pl.ds(cls, rows, stride=dil), :] = ref[:, lane0:lane0 + LANES]
    return jnp.concatenate([scr[half] for half in range(COL_TILE // LANES)], axis=1)


def _class_view_scratch(tm):
    return pltpu.VMEM((COL_TILE // LANES, tm, LANES), F32)


def _norm_proj_kernel(x_ref, g_ref, w_ref, cos_ref, sin_ref, *refs, n_rot, dils):
    o_refs, scratch = refs[:len(dils)], list(refs[len(dils):])
    xn = _rms(x_ref[...], g_ref[...]).astype(BF16)
    width = w_ref.shape[1] // len(dils)
    for j, (o_ref, dil) in enumerate(zip(o_refs, dils)):
        y = jnp.dot(xn, w_ref[:, j * width:(j + 1) * width], preferred_element_type=F32)
        if j < n_rot:
            y = _rotate_heads(y, cos_ref[...], sin_ref[...])
        if dil == 1:
            o_ref[...] = y
        else:
            _to_class_view(y, o_ref, scratch.pop(0), dil)


def _resident(shape):
    return pl.BlockSpec(shape, lambda *_: (0,) * len(shape), pipeline_mode=pl.Buffered(1))


def _layer_resident(stacked, layer):
    shape = stacked.shape[1:]
    return pl.BlockSpec((None,) + shape, lambda *_: (layer,) + (0,) * len(shape),
                        pipeline_mode=pl.Buffered(1))


def _norm_proj(x, g, w_bf16, cos, sin_signed, seq_len, *, n_rot, out_width=COL_TILE, dils=None, tm_want=1024):
    n, d = x.shape
    cols = w_bf16.shape[1]
    n_out = cols // out_width
    dils = tuple(dils) if dils is not None else (1,) * n_out
    assert len(dils) == n_out and (out_width == COL_TILE or set(dils) == {1})
    tm = _row_tile(n, tm_want)
    if tm <= seq_len:
        assert seq_len % tm == 0
        per_seq = seq_len // tm
        tab_map = lambda i: (i % per_seq, 0)
    else:
        assert tm % seq_len == 0
        cos = jnp.tile(cos, (tm // seq_len, 1))
        sin_signed = jnp.tile(sin_signed, (tm // seq_len, 1))
        tab_map = lambda i: (0, 0)
    assert all(tm % (8 * dil) == 0 or dil == 1 for dil in dils)
    return pl.pallas_call(
        functools.partial(_norm_proj_kernel, n_rot=n_rot, dils=dils),
        grid=(n // tm,),
        in_specs=[
            pl.BlockSpec((tm, d), lambda i: (i, 0)),
            _resident((1, d)),
            _resident(w_bf16.shape),
            pl.BlockSpec((tm, LANES), tab_map),
            pl.BlockSpec((tm, LANES), tab_map),
        ],
        out_specs=[pl.BlockSpec((tm // dil, dil * out_width), lambda i: (i, 0)) for dil in dils],
        out_shape=[jax.ShapeDtypeStruct((n // dil, dil * out_width), F32) for dil in dils],
        scratch_shapes=[_class_view_scratch(tm) for dil in dils if dil > 1],
        compiler_params=_cparams("parallel"),
        name="norm_proj",
    )(x, g.reshape(1, d), w_bf16, cos, sin_signed)


def _rwkv_proj_kernel(*refs, mix, has_vfirst, tm, seq_len, tail):
    it = iter(refs)
    h_ref = next(it)
    hp_ref = next(it) if seq_len >= tm else None
    sh_ref = next(it)
    g_ref, mu_ref, win_ref = next(it), next(it), next(it)
    w0_ref, w1_ref, w2_ref = next(it), next(it), next(it)
    a0_ref, a1_ref, a2_ref = next(it), next(it), next(it)
    g1_ref, g2_ref = next(it), next(it)
    if has_vfirst:
        vf_ref, v0_ref, v1_ref, v2_ref = next(it), next(it), next(it), next(it)
    r_ref, lw_ref, k_ref, v_ref, a_ref, gate_ref, qm_ref, tail_ref = (next(it) for _ in range(8))

    i = pl.program_id(0)
    gain = g_ref[...]
    xn = _rms(h_ref[...], gain)
    row = lax.broadcasted_iota(jnp.int32, (tm, 1), 0)
    rolled = pltpu.roll(xn, 1, axis=0)
    if seq_len >= tm:
        prev_tile_last = _rms(hp_ref[...], gain)[7:8, :]
        at_seq_start = (i * tm) % seq_len == 0
        first = jnp.where(at_seq_start, sh_ref[0][7:8, :], prev_tile_last)
        x_prev = jnp.where(row == 0, first, rolled)
    else:
        x_prev = jnp.where(row % seq_len == 0, sh_ref[...], rolled)
    xx = x_prev - xn
    mu = mu_ref[...]

    def mixed(idx):
        return (xn + xx * mu[idx:idx + 1, :]).astype(BF16)

    qm_ref[...] = jnp.dot(xn.astype(BF16), win_ref[:, 3 * mix:], preferred_element_type=F32)
    tail_ref[0] = xn[tm - tail:, :]
    r_ref[...] = jnp.dot(mixed(0), win_ref[:, 0:mix], preferred_element_type=F32)
    w_lin = w0_ref[...] + _mm(jnp.tanh(_mm(mixed(1), w1_ref[...])), w2_ref[...])
    lw_ref[...] = -math.exp(-0.5) * _sigmoid(w_lin)
    k_ref[...] = jnp.dot(mixed(2), win_ref[:, mix:2 * mix], preferred_element_type=F32)
    xv = mixed(3)
    v = jnp.dot(xv, win_ref[:, 2 * mix:3 * mix], preferred_element_type=F32)
    if has_vfirst:
        v = v + (vf_ref[...] - v) * _sigmoid(v0_ref[...] + _mm(_mm(xv, v1_ref[...]), v2_ref[...]))
    v_ref[...] = v
    a_ref[...] = _sigmoid(a0_ref[...] + _mm(_mm(mixed(4), a1_ref[...]), a2_ref[...]))
    gate_ref[...] = _mm(_sigmoid(_mm(mixed(5), g1_ref[...])), g2_ref[...])


def _rwkv_proj(h, shift, seq_len, W, li, v_first, *, tm_want=512):
    n, d = h.shape
    mix = W['a_w0'].shape[1]
    tm = _row_tile(n, tm_want)
    n_tiles = n // tm
    has_vfirst = v_first is not None
    row_spec = lambda w: pl.BlockSpec((tm, w), lambda i: (i, 0))
    full = lambda a: _resident(a.shape)

    args, specs = [h], [row_spec(d)]
    if seq_len >= tm:
        assert seq_len % tm == 0
        tail = 8
        args.append(h)
        specs.append(pl.BlockSpec((8, d), lambda i: (jnp.maximum(i * (tm // 8) - 1, 0), 0)))
        args.append(jnp.broadcast_to(shift[:, None, :], (shift.shape[0], 8, d)))
        specs.append(pl.BlockSpec((1, 8, d), lambda i: ((i * tm) // seq_len, 0, 0)))
    else:
        assert tm % seq_len == 0
        tail = tm
        args.append(jnp.repeat(shift, seq_len, axis=0))
        specs.append(row_spec(d))
    lead = [W['g_pre_mix'][li].reshape(1, d), W['a_mu'][li]]
    small = [W['a_w0'][li].reshape(1, mix), W['a_w1_bf16'][li], W['a_w2_bf16'][li],
             W['a_a0'][li].reshape(1, mix), W['a_a1_bf16'][li], W['a_a2_bf16'][li],
             W['a_g1_bf16'][li], W['a_g2_bf16'][li]]
    args += lead + [W['a_w_in_bf16']] + small
    specs += [full(a) for a in lead] + [_layer_resident(W['a_w_in_bf16'], li)] + [full(a) for a in small]
    if has_vfirst:
        vi = li - 1
        extra = [W['a_v0'][vi].reshape(1, mix), W['a_v1_bf16'][vi], W['a_v2_bf16'][vi]]
        args += [v_first] + extra
        specs += [row_spec(mix)] + [full(a) for a in extra]

    wide = jax.ShapeDtypeStruct((n, mix), F32)
    out_shape = [wide] * 6 + [jax.ShapeDtypeStruct((n, MEM_WIDTH), F32),
                              jax.ShapeDtypeStruct((n_tiles, tail, d), F32)]
    out_specs = [row_spec(mix)] * 6 + [row_spec(MEM_WIDTH),
                                       pl.BlockSpec((1, tail, d), lambda i: (i, 0, 0))]
    r, lw, k, v, a, gate, q_mem, xn_tail = pl.pallas_call(
        functools.partial(_rwkv_proj_kernel, mix=mix, has_vfirst=has_vfirst, tm=tm,
                          seq_len=seq_len, tail=tail),
        grid=(n_tiles,),
        in_specs=specs,
        out_specs=out_specs,
        out_shape=out_shape,
        compiler_params=_cparams("parallel"),
        name="rwkv_proj",
    )(*args)
    xn_rows = xn_tail.reshape(n_tiles * tail, d)
    last = xn_rows.reshape(-1, seq_len if seq_len < tm else tail * (seq_len // tm), d)[:, -1]
    return r, lw, k, v, a, gate, q_mem, last


def _head_sums(x, ones_bd):
    hi = x.astype(BF16)
    lo = (x - hi.astype(F32)).astype(BF16)
    tiles = []
    for t in range(x.shape[1] // COL_TILE):
        s = slice(t * COL_TILE, (t + 1) * COL_TILE)
        tiles.append(jnp.dot(hi[:, s], ones_bd, preferred_element_type=F32)
                     + jnp.dot(lo[:, s], ones_bd, preferred_element_type=F32))
    return jnp.concatenate(tiles, axis=1)


def _wkv_kernel(r_ref, lw_ref, k_ref, v_ref, a_ref, gate_ref, s0_ref, kkp_ref, kap_ref, rkp_ref,
                lnw_ref, lnb_ref, o_ref, sout_ref, s_scr, *, chunk, heads):
    c = pl.program_id(1)
    rows = r_ref.shape[0]
    mix = r_ref.shape[2]

    @pl.when(c == 0)
    def _():
        for b in range(rows):
            s_scr[b * heads:(b + 1) * heads] = s0_ref[b]

    side_by_side = lambda ref: jnp.concatenate([ref[b] for b in range(rows)], axis=1)
    per_row = lambda ref: jnp.concatenate([ref[...]] * rows, axis=1)
    r, lw, k, v, a = (side_by_side(ref) for ref in (r_ref, lw_ref, k_ref, v_ref, a_ref))
    row = lax.broadcasted_iota(jnp.int32, (chunk, chunk), 0)
    col = lax.broadcasted_iota(jnp.int32, (chunk, chunk), 1)
    strict = col < row
    incl = col <= row
    tri = incl.astype(BF16)
    lw_hi = lw.astype(BF16)
    lw_rest = lw - lw_hi.astype(F32)
    lw_mid = lw_rest.astype(BF16)
    lw_lo = (lw_rest - lw_mid.astype(F32)).astype(BF16)
    c_inc = (jnp.dot(tri, lw_hi, preferred_element_type=F32)
             + jnp.dot(tri, lw_mid, preferred_element_type=F32)
             + jnp.dot(tri, lw_lo, preferred_element_type=F32))
    c_last = c_inc[chunk - 1:chunk, :]
    e_inc = jnp.exp(c_inc)
    e_exc = jnp.exp(c_inc - lw)
    e_neg = jnp.exp(-c_inc)
    e_tail = jnp.exp(c_last - c_inc)
    p_last = jnp.exp(c_last)

    bd_r = lax.broadcasted_iota(jnp.int32, (COL_TILE, COL_TILE), 0) // HEAD_DIM
    bd_c = lax.broadcasted_iota(jnp.int32, (COL_TILE, COL_TILE), 1) // HEAD_DIM
    ones_bd = (bd_r == bd_c).astype(BF16)

    kk_raw = k * per_row(kkp_ref)
    kk = kk_raw * (1.0 / jnp.maximum(jnp.sqrt(_head_sums(kk_raw * kk_raw, ones_bd)), KK_EPS))
    kka = kk * a
    k2 = k * (1.0 + (a - 1.0) * per_row(kap_ref))
    bonus = _head_sums(r * k2 * per_row(rkp_ref), ones_bd) * v
    a_hat_f = -kk * e_exc
    r_hat_f = r * e_inc
    b_chk_f = kka * e_neg
    k_chk_f = k2 * e_neg
    b_til_f = kka * e_tail
    k_til_f = k2 * e_tail

    eye_h = (lax.broadcasted_iota(jnp.int32, (HEAD_DIM, HEAD_DIM), 0)
             == lax.broadcasted_iota(jnp.int32, (HEAD_DIM, HEAD_DIM), 1))
    n_levels = max(1, int(math.log2(chunk)))
    hs = range(rows * heads)
    sl = [slice(h * HEAD_DIM, (h + 1) * HEAD_DIM) for h in hs]
    a_hat = [a_hat_f[:, s] for s in sl]
    r_hat = [r_hat_f[:, s] for s in sl]
    v_h = [v[:, s] for s in sl]

    amat = [_mm_nt(jnp.concatenate([a_hat[h], r_hat[h]], axis=0),
                   jnp.concatenate([b_chk_f[:, sl[h]], k_chk_f[:, sl[h]]], axis=0)) for h in hs]
    a_ab = [jnp.where(strict, m[:chunk, :chunk], 0.0) for m in amat]
    a_ak = [jnp.where(strict, m[:chunk, chunk:], 0.0) for m in amat]
    a_rb = [jnp.where(incl, m[chunk:, :chunk], 0.0) for m in amat]
    a_rk = [jnp.where(incl, m[chunk:, chunk:], 0.0) for m in amat]
    av = [_mm(jnp.concatenate([a_ak[h], a_rk[h]], axis=0), v_h[h]) for h in hs]

    w12 = [jnp.concatenate([a_hat[h], av[h][:chunk]], axis=1) for h in hs]
    power = a_ab
    for level in range(n_levels):
        w12 = [w12[h] + _mm(power[h], w12[h]) for h in hs]
        if level + 1 < n_levels:
            power = [_mm(p, p) for p in power]

    rx = [_mm(a_rb[h], w12[h]) + jnp.concatenate([r_hat[h], av[h][chunk:]], axis=1) for h in hs]
    gh = [_mm_tn(w12[h], b_til_f[:, sl[h]]) for h in hs]
    vk = [_mm_tn(v_h[h], k_til_f[:, sl[h]]) for h in hs]
    g_t = [gh[h][:HEAD_DIM] + jnp.where(eye_h, p_last[:, sl[h]], 0.0) for h in hs]
    h_t = [gh[h][HEAD_DIM:] + vk[h] for h in hs]

    s_prev = [s_scr[h] for h in hs]
    o_h = [_mm_nt(rx[h][:, :HEAD_DIM], s_prev[h]) + rx[h][:, HEAD_DIM:] for h in hs]
    s_new = [_mm(s_prev[h], g_t[h]) + h_t[h] for h in hs]
    for h in hs:
        s_scr[h] = s_new[h]

    o = jnp.concatenate(o_h, axis=1)
    cen = o - _head_sums(o, ones_bd) * (1.0 / HEAD_DIM)
    var = _head_sums(cen * cen, ones_bd) * (1.0 / HEAD_DIM)
    o = cen * lax.rsqrt(var + LNX_EPS) * per_row(lnw_ref) + per_row(lnb_ref)
    o = (o + bonus) * side_by_side(gate_ref)
    for b in range(rows):
        o_ref[b] = o[:, b * mix:(b + 1) * mix]

    @pl.when(c == pl.num_programs(1) - 1)
    def _():
        for b in range(rows):
            sout_ref[b] = s_scr[b * heads:(b + 1) * heads]


def _wkv(r, lw, k, v, a, gate, s0, W, li, batch, seq_len, chunk):
    mix = r.shape[1]
    heads = mix // HEAD_DIM
    if seq_len % chunk:
        raise ValueError("sequence length must be a multiple of the chunk")
    if mix % COL_TILE:
        raise ValueError("mixer width must be a whole number of 4-head tiles")
    n_chunks = seq_len // chunk
    seq = lambda t: t.reshape(batch, seq_len, mix)
    rows = 2 if batch % 2 == 0 else 1
    blk = pl.BlockSpec((rows, chunk, mix), lambda b, c: (b, c, 0))
    st = pl.BlockSpec((rows, heads, HEAD_DIM, HEAD_DIM), lambda b, c: (b, 0, 0, 0))
    par = pl.BlockSpec((1, mix), lambda b, c: (0, 0))
    params = [W['a_k_k'][li], W['a_k_a'][li], W['a_r_k'][li], W['a_lnx_w'][li], W['a_lnx_b'][li]]
    o, s_out = pl.pallas_call(
        functools.partial(_wkv_kernel, chunk=chunk, heads=heads),
        grid=(batch // rows, n_chunks),
        in_specs=[blk] * 6 + [st] + [par] * 5,
        out_specs=[blk, st],
        out_shape=[jax.ShapeDtypeStruct((batch, seq_len, mix), F32),
                   jax.ShapeDtypeStruct(s0.shape, F32)],
        scratch_shapes=[pltpu.VMEM((rows * heads, HEAD_DIM, HEAD_DIM), F32)],
        compiler_params=_cparams("parallel", "arbitrary"),
        name="wkv7",
    )(seq(r), seq(lw), seq(k), seq(v), seq(a), seq(gate), s0, *[p.reshape(1, mix) for p in params])
    return o.reshape(batch * seq_len, mix), s_out


def _mem_attn_kernel(q_ref, kv_ref, o_ref):
    q = q_ref[0]
    tq = q.shape[0]
    mk = kv_ref[0, :, :MEM_WIDTH].astype(BF16)
    mv = kv_ref[0, :, MEM_WIDTH:].astype(BF16)
    masks = _head_lane_masks(MEM_WIDTH)
    stacked = jnp.concatenate([jnp.where(m, q, 0.0) for m in masks], axis=0)
    s = _mm_nt(stacked, mk) * ATTN_SCALE
    p = jnp.exp(s - jnp.max(s, axis=-1, keepdims=True))
    p = p * (1.0 / jnp.sum(p, axis=-1, keepdims=True))
    pv = _mm(p, mv)
    out = jnp.zeros(q.shape, F32)
    for h, m in enumerate(masks):
        out = out + jnp.where(m, pv[h * tq:(h + 1) * tq, :], 0.0)
    o_ref[0] = out


def _mem_attn(q_mem, mem_kv, batch, seq_len, *, tq_want=512):
    tq = _row_tile(seq_len, tq_want)
    m = mem_kv.shape[1]
    out = pl.pallas_call(
        _mem_attn_kernel,
        grid=(batch, seq_len // tq),
        in_specs=[pl.BlockSpec((1, tq, MEM_WIDTH), lambda b, t: (b, t, 0)),
                  pl.BlockSpec((1, m, 2 * MEM_WIDTH), lambda b, t: (b, 0, 0))],
        out_specs=pl.BlockSpec((1, tq, MEM_WIDTH), lambda b, t: (b, t, 0)),
        out_shape=jax.ShapeDtypeStruct((batch, seq_len, MEM_WIDTH), F32),
        compiler_params=_cparams("parallel", "parallel"),
        name="mem_attn",
    )(q_mem.reshape(batch, seq_len, MEM_WIDTH), mem_kv)
    return out.reshape(batch * seq_len, MEM_WIDTH)


def _out_ffn_kernel(*refs, dils, mix_width, d_ff, tf):
    n_groups = len(dils)
    n_scratch = 2 * sum(dil > 1 for dil in dils)
    scratch = list(refs[len(refs) - n_scratch:])
    it = iter(refs[:len(refs) - n_scratch])
    if n_groups:
        read = lambda ref, dil: ref[...] if dil == 1 else _from_class_view(ref, scratch.pop(0), dil)
        og = [read(next(it), dil) for dil in dils]
        lse = [read(next(it), dil) for dil in dils]
        top = functools.reduce(jnp.maximum, lse)
        wgt = [jnp.exp(s - top) for s in lse]
        o_mix = sum(w * o for w, o in zip(wgt, og)) / sum(wgt)
    else:
        o_mix = next(it)[...]
    om_ref, wout_ref, gpm_ref, h_ref, gpre_ref, win_ref, wo_ref, gpost_ref, o_ref, acc_ref = it
    mixed = (jnp.dot(o_mix.astype(BF16), wout_ref[:mix_width, :], preferred_element_type=F32)
             + jnp.dot(om_ref[...].astype(BF16), wout_ref[mix_width:, :], preferred_element_type=F32))
    h1 = h_ref[...] + _rms(mixed, gpm_ref[...])
    xn = _rms(h1, gpre_ref[...]).astype(BF16)
    def gate_up(j):
        return (jnp.dot(xn, win_ref[:, j * tf:(j + 1) * tf], preferred_element_type=F32),
                jnp.dot(xn, win_ref[:, d_ff + j * tf:d_ff + (j + 1) * tf], preferred_element_type=F32))

    n_ff = d_ff // tf
    ahead = gate_up(0)
    for j in range(n_ff):
        gate, up = ahead
        if j + 1 < n_ff:
            ahead = gate_up(j + 1)
        act = (gate * _sigmoid(gate) * up).astype(BF16)
        part = jnp.dot(act, wo_ref[j * tf:(j + 1) * tf, :], preferred_element_type=F32)
        if j == 0:
            acc_ref[...] = part
        else:
            acc_ref[...] += part
    o_ref[...] = h1 + _rms(acc_ref[...], gpost_ref[...])


def _out_ffn(o_mix, lse, dils, o_mem, h, W, l, w_out_stack, w_out_layer, *, tm_want=512, tf=256):
    n, d = h.shape
    tm = _row_tile(n, tm_want)
    w_in, w_o = W['w_ffn_in_bf16'], W['w_ffn_out_bf16']
    d_ff = w_o.shape[1]
    assert d_ff % tf == 0
    mix_width = w_out_stack.shape[1] - MEM_WIDTH
    row = lambda w: pl.BlockSpec((tm, w), lambda i: (i, 0))
    if lse is not None:
        lead = list(o_mix) + list(lse)
        lead_specs = [pl.BlockSpec((tm // dil, dil * COL_TILE), lambda i: (i, 0)) for dil in dils] * 2
    else:
        lead, lead_specs, dils = [o_mix], [row(o_mix.shape[1])], ()
    vec = lambda g: g.reshape(1, d)
    return pl.pallas_call(
        functools.partial(_out_ffn_kernel, dils=tuple(dils), mix_width=mix_width, d_ff=d_ff, tf=tf),
        grid=(n // tm,),
        in_specs=lead_specs + [
            row(MEM_WIDTH), _layer_resident(w_out_stack, w_out_layer), _resident((1, d)), row(d),
            _resident((1, d)), _layer_resident(w_in, l), _layer_resident(w_o, l), _resident((1, d))],
        out_specs=row(d),
        out_shape=jax.ShapeDtypeStruct((n, d), F32),
        scratch_shapes=[pltpu.VMEM((tm, d), F32)] + [_class_view_scratch(tm) for dil in dils if dil > 1] * 2,
        compiler_params=_cparams("parallel"),
        name="out_ffn",
    )(*lead, o_mem, w_out_stack, vec(W['g_post_mix'][l]), h, vec(W['g_pre_ffn'][l]), w_in, w_o,
      vec(W['g_post_ffn'][l]))


def _dil_prompt_kernel(q_ref, kc_ref, kh_ref, vc_ref, vh_ref, o_ref, lse_ref, kx_ref, vx_ref, *, tile):
    t = pl.program_id(2)
    kx_ref[0:BAND, :] = kh_ref[0].astype(BF16)
    kx_ref[BAND:, :] = kc_ref[0].astype(BF16)
    vx_ref[0:BAND, :] = vh_ref[0].astype(BF16)
    vx_ref[BAND:, :] = vc_ref[0].astype(BF16)

    rows = HEADS_PER_TILE * BAND
    qi = lax.broadcasted_iota(jnp.int32, (rows, 2 * BAND), 0) % BAND
    kj = lax.broadcasted_iota(jnp.int32, (rows, 2 * BAND), 1)
    band = (kj >= qi) & (kj <= qi + BAND)
    masks = _head_lane_masks(COL_TILE)

    def block(blk, carry):
        start = pl.multiple_of(blk * BAND, BAND)
        q = q_ref[0, pl.ds(start, BAND), :] * ATTN_SCALE
        keys = kx_ref[pl.ds(start, 2 * BAND), :]
        vals = vx_ref[pl.ds(start, 2 * BAND), :]
        lo = jnp.where((t == 0) & (blk == 0), BAND, 0)
        valid = band & (kj >= lo)
        stacked = jnp.concatenate([jnp.where(m, q, 0.0) for m in masks], axis=0)
        s = jnp.where(valid, _mm_nt(stacked, keys), -jnp.inf)
        top = jnp.max(s, axis=-1, keepdims=True)
        p = jnp.exp(s - top)
        den = jnp.sum(p, axis=-1, keepdims=True)
        pv = _mm(p, vals) * (1.0 / den)
        lse_rows = top + jnp.log(den)
        out = jnp.zeros((BAND, COL_TILE), F32)
        lse = jnp.zeros((BAND, COL_TILE), F32)
        for h, m in enumerate(masks):
            out = out + jnp.where(m, pv[h * BAND:(h + 1) * BAND, :], 0.0)
            lse = lse + jnp.where(m, lse_rows[h * BAND:(h + 1) * BAND, :], 0.0)
        o_ref[0, pl.ds(start, BAND), :] = out
        lse_ref[0, pl.ds(start, BAND), :] = lse
        return carry

    lax.fori_loop(0, tile // BAND, block, 0, unroll=2)


def _dil_prompt_group(q, k, v, batch, seq_len, dil):
    cls_len = seq_len // dil
    tile = min(cls_len, 1024)
    assert cls_len % tile == 0 and tile % BAND == 0
    n_t = cls_len // tile
    view = lambda x: x.reshape(batch, cls_len, dil * COL_TILE)
    cur = pl.BlockSpec((1, tile, COL_TILE), lambda b, c, t: (b, t, c))
    halo = pl.BlockSpec((1, BAND, COL_TILE),
                        lambda b, c, t: (b, jnp.maximum(t * (tile // BAND) - 1, 0), c))
    out_shape = jax.ShapeDtypeStruct((batch, cls_len, dil * COL_TILE), F32)
    o, lse = pl.pallas_call(
        functools.partial(_dil_prompt_kernel, tile=tile),
        grid=(batch, dil, n_t),
        in_specs=[cur, cur, halo, cur, halo],
        out_specs=[cur, cur],
        out_shape=[out_shape, out_shape],
        scratch_shapes=[pltpu.VMEM((tile + BAND, COL_TILE), BF16)] * 2,
        compiler_params=_cparams("parallel", "parallel", "arbitrary"),
        name="dil_attn_prompt",
    )(view(q), view(k), view(k), view(v), view(v))
    flat = lambda x: x.reshape(batch * cls_len, dil * COL_TILE)
    return flat(o), flat(lse)


def _dil_sample_kernel(*refs, seq_len, write_cache):
    q_refs = refs[:N_GROUPS]
    k_refs = refs[N_GROUPS:2 * N_GROUPS]
    v_refs = refs[2 * N_GROUPS:3 * N_GROUPS]
    cache_refs = refs[3 * N_GROUPS:4 * N_GROUPS]
    o_ref = refs[4 * N_GROUPS]
    new_refs = refs[4 * N_GROUPS + 1:]
    masks = _head_lane_masks(COL_TILE)
    rows = HEADS_PER_TILE * seq_len

    run_top = jnp.full((rows, 1), -jnp.inf, F32)
    run_den = jnp.zeros((rows, 1), F32)
    run_acc = jnp.zeros((rows, COL_TILE), F32)
    for g, (win, dil) in enumerate(DIL_GROUPS):
        cache = cache_refs[g]
        length = cache.shape[1]
        q = q_refs[g][...] * ATTN_SCALE
        stacked = jnp.concatenate([jnp.where(m, q, 0.0) for m in masks], axis=0)
        k_new, v_new = k_refs[g][...], v_refs[g][...]
        ck = cache[0, :, :COL_TILE]
        cv = cache[0, :, COL_TILE:]
        s_c = _mm_nt(stacked, ck)
        s_n = _mm_nt(stacked, k_new)
        tq_c = lax.broadcasted_iota(jnp.int32, s_c.shape, 0) % seq_len
        dist_c = length + tq_c - lax.broadcasted_iota(jnp.int32, s_c.shape, 1)
        ok_c = (dist_c % dil == 0) & (dist_c <= win)
        tq_n = lax.broadcasted_iota(jnp.int32, s_n.shape, 0) % seq_len
        dist_n = tq_n - lax.broadcasted_iota(jnp.int32, s_n.shape, 1)
        ok_n = (dist_n >= 0) & (dist_n % dil == 0)
        s_c = jnp.where(ok_c, s_c, -jnp.inf)
        s_n = jnp.where(ok_n, s_n, -jnp.inf)
        top = jnp.maximum(jnp.maximum(jnp.max(s_c, axis=-1, keepdims=True),
                                      jnp.max(s_n, axis=-1, keepdims=True)), run_top)
        p_c = jnp.exp(s_c - top)
        p_n = jnp.exp(s_n - top)
        scale = jnp.exp(run_top - top)
        run_den = run_den * scale + jnp.sum(p_c, axis=-1, keepdims=True) + jnp.sum(p_n, axis=-1, keepdims=True)
        run_acc = run_acc * scale + _mm(p_c, cv) + _mm(p_n, v_new)
        run_top = top
        if write_cache:
            new = new_refs[g]
            if length > seq_len:
                new[0, 0:length - seq_len, :] = cache[0, seq_len:, :]
            new[0, length - seq_len:, :COL_TILE] = k_new
            new[0, length - seq_len:, COL_TILE:] = v_new
    res = run_acc / run_den
    out = jnp.zeros((seq_len, COL_TILE), F32)
    for h, m in enumerate(masks):
        out = out + jnp.where(m, res[h * seq_len:(h + 1) * seq_len, :], 0.0)
    o_ref[...] = out


def _dil_sample(q_slabs, kv_slabs, caches, batch, seq_len, write_cache):
    n = batch * seq_len
    in_specs = [pl.BlockSpec((seq_len, COL_TILE), lambda b: (b, 0))] * (3 * N_GROUPS)
    cache_specs = [pl.BlockSpec((1,) + c.shape[1:], lambda b: (b, 0, 0)) for c in caches]
    out_shape = [jax.ShapeDtypeStruct((n, COL_TILE), F32)]
    out_specs = [pl.BlockSpec((seq_len, COL_TILE), lambda b: (b, 0))]
    if write_cache:
        out_shape += [jax.ShapeDtypeStruct(c.shape, F32) for c in caches]
        out_specs += cache_specs
    res = pl.pallas_call(
        functools.partial(_dil_sample_kernel, seq_len=seq_len, write_cache=write_cache),
        grid=(batch,),
        in_specs=in_specs + cache_specs,
        out_specs=out_specs,
        out_shape=out_shape,
        compiler_params=_cparams("parallel"),
        name="dil_attn_sample",
    )(*q_slabs[:N_GROUPS], *kv_slabs, *caches)
    return res[0], list(res[1:])


def _rope_tables(pos):
    half = HEAD_DIM // 2
    inv = ROPE_THETA ** (-jnp.arange(half, dtype=F32) / half)
    ang = pos.astype(F32)[:, None] * inv[None, :]
    cos, sin = jnp.cos(ang), jnp.sin(ang)
    cos = jnp.tile(jnp.concatenate([cos, cos], axis=-1), (1, LANES // HEAD_DIM))
    sin_signed = jnp.tile(jnp.concatenate([-sin, sin], axis=-1), (1, LANES // HEAD_DIM))
    return cos, sin_signed


def _trunk(h, pos, mem_kv_all, wkv0, shift0, caches, W, batch, seq_len, chunk):
    depth = W['g_pre_mix'].shape[0]
    n_a = W['a_mu'].shape[0]
    cos, sin_signed = _rope_tables(pos)
    new_wkv, new_shift, new_caches = [], [], None
    v_first = None
    kv_tiles = None
    for l in range(depth):
        if l < n_a:
            r, lw, k, v, a, gate, q_mem, xn_last = _rwkv_proj(h, shift0[l], seq_len, W, l, v_first)
            if l == 0:
                v_first = v
            o_mix, s_out = _wkv(r, lw, k, v, a, gate, wkv0[l], W, l, batch, seq_len, chunk)
            new_wkv.append(s_out)
            new_shift.append(xn_last)
            lse = None
            w_out, w_out_layer = W['a_w_out_bf16'], l
        else:
            j = l - n_a
            dils = tuple(dil for _, dil in DIL_GROUPS) if caches is None else (1,) * N_GROUPS
            if kv_tiles is None:
                kv_tiles = _norm_proj(h, W['g_kv'], W['w_kv_bf16'], cos, sin_signed, seq_len, n_rot=N_GROUPS,
                                      dils=dils * 2)
            proj = _norm_proj(h, W['g_pre_mix'][l], W['b_w_in_bf16'][j], cos, sin_signed, seq_len,
                              n_rot=N_GROUPS, dils=dils + (1,))
            q_mem = proj[N_GROUPS]
            if caches is None:
                outs = [_dil_prompt_group(proj[g], kv_tiles[g], kv_tiles[N_GROUPS + g], batch, seq_len, dil)
                        for g, dil in enumerate(dils)]
                o_mix = [o for o, _ in outs]
                lse = [s for _, s in outs]
            else:
                o_mix, written = _dil_sample(proj, kv_tiles, caches, batch, seq_len, new_caches is None)
                if new_caches is None:
                    new_caches = written
                lse = None
            w_out, w_out_layer = W['b_w_out_bf16'], j
        o_mem = _mem_attn(q_mem, mem_kv_all[l], batch, seq_len)
        h = _out_ffn(o_mix, lse, dils if lse is not None else (), o_mem, h, W, l, w_out, w_out_layer)
    return h, jnp.stack(new_wkv), jnp.stack(new_shift), kv_tiles, new_caches


def _prompt_windows(kv_tiles, batch, seq_len):
    bufs = []
    for g, (win, dil) in enumerate(DIL_GROUPS):
        length = min(win, PAST_LEN)
        assert seq_len >= length and length % dil == 0
        rows = seq_len // dil

        def tail(t):
            t = t.reshape(batch, rows, dil * COL_TILE)[:, rows - length // dil:]
            return t.reshape(batch, length, HEADS_PER_TILE, HEAD_DIM)

        bufs.append(jnp.stack([tail(kv_tiles[g]), tail(kv_tiles[N_GROUPS + g])], axis=2))
    return bufs


def kernel(x_prompt, x_sample, state_wkv, state_shift, cache_win_g0, cache_win_g1, cache_win_g2, cache_mem, mem_prompt, g_pre_mix, g_post_mix, g_pre_ffn, g_post_ffn, g_mem, w_mem_kv, w_ffn_in, w_ffn_out, a_mu, a_w_in, a_w0, a_w1, a_w2, a_a0, a_a1, a_a2, a_v0, a_v1, a_v2, a_g1, a_g2, a_k_k, a_k_a, a_r_k, a_lnx_w, a_lnx_b, a_w_out, g_kv, w_kv, b_w_in, b_w_out):
    W = dict(g_pre_mix=g_pre_mix, g_post_mix=g_post_mix, g_pre_ffn=g_pre_ffn, g_post_ffn=g_post_ffn,
             a_mu=a_mu, a_w0=a_w0, a_a0=a_a0, a_v0=a_v0, a_k_k=a_k_k, a_k_a=a_k_a,
             a_r_k=a_r_k.reshape(a_r_k.shape[0], -1), a_lnx_w=a_lnx_w, a_lnx_b=a_lnx_b, g_kv=g_kv)
    for name, w in dict(w_ffn_in=w_ffn_in, w_ffn_out=w_ffn_out, a_w_in=a_w_in, a_w1=a_w1, a_w2=a_w2,
                        a_a1=a_a1, a_a2=a_a2, a_v1=a_v1, a_v2=a_v2, a_g1=a_g1, a_g2=a_g2,
                        a_w_out=a_w_out, w_kv=w_kv, b_w_in=b_w_in, b_w_out=b_w_out,
                        w_mem_kv=w_mem_kv).items():
        W[name + '_bf16'] = w.astype(BF16)

    bp, tp, d = x_prompt.shape
    bs, ts, _ = x_sample.shape
    depth = g_pre_mix.shape[0]
    n_a = a_mu.shape[0]
    heads = a_w0.shape[1] // HEAD_DIM
    m_tok = mem_prompt.shape[1]

    mem_rows = mem_prompt.reshape(bp * m_tok, d)
    no_tab = jnp.zeros((m_tok, LANES), F32)
    mem_kv_p = jnp.stack([
        _norm_proj(mem_rows, g_mem[l], W['w_mem_kv_bf16'][l], no_tab, no_tab, m_tok,
                   n_rot=0, out_width=2 * MEM_WIDTH)[0].reshape(bp, m_tok, 2 * MEM_WIDTH)
        for l in range(depth)])
    wkv_zero = jnp.zeros((n_a, bp, heads, HEAD_DIM, HEAD_DIM), F32)
    shift_zero = jnp.zeros((n_a, bp, d), F32)
    y_p, wkv_p, shift_p, kv_p, _ = _trunk(
        x_prompt.reshape(bp * tp, d), jnp.arange(tp, dtype=jnp.int32), mem_kv_p, wkv_zero, shift_zero,
        None, W, bp, tp, chunk=64)
    bufs_p = _prompt_windows(kv_p, bp, tp)

    caches = [c.reshape(bs, c.shape[1], 2 * COL_TILE) for c in (cache_win_g0, cache_win_g1, cache_win_g2)]
    mem_kv_s = cache_mem.reshape(depth, bs, m_tok, 2 * MEM_WIDTH)
    y_s, wkv_s, shift_s, _, bufs_s = _trunk(
        x_sample.reshape(bs * ts, d), PAST_LEN + jnp.arange(ts, dtype=jnp.int32), mem_kv_s, state_wkv,
        state_shift, caches, W, bs, ts, chunk=ts)
    bufs_s = [b.reshape(c.shape) for b, c in zip(bufs_s, (cache_win_g0, cache_win_g1, cache_win_g2))]

    return (y_p.reshape(bp, tp, d), y_s.reshape(bs, ts, d), wkv_p, shift_p, bufs_p[0], bufs_p[1], bufs_p[2],
            mem_kv_p.reshape(depth, bp, m_tok, 2, MEM_HEADS, HEAD_DIM),
            wkv_s, shift_s, bufs_s[0], bufs_s[1], bufs_s[2])
```

```python
import functools
import math

import jax
import jax.numpy as jnp
from jax import lax
from jax.experimental import pallas as pl
from jax.experimental.pallas import tpu as pltpu

F32 = jnp.float32
BF16 = jnp.bfloat16

HEAD_DIM = 64
LANES = 128
COL_TILE = 256
HEADS_PER_TILE = COL_TILE // HEAD_DIM
MEM_HEADS = 4
MEM_WIDTH = MEM_HEADS * HEAD_DIM
DIL_GROUPS = ((128, 1), (512, 4), (2048, 16))
N_GROUPS = len(DIL_GROUPS)
PAST_LEN = 8192
ROPE_THETA = 10000.0
NORM_EPS = 1e-6
LNX_EPS = HEAD_DIM * 1e-5
KK_EPS = 1e-12
ATTN_SCALE = HEAD_DIM ** -0.5
BAND = 128
VMEM_LIMIT = 56 * 1024 * 1024


def _cparams(*sem):
    return pltpu.CompilerParams(dimension_semantics=sem, vmem_limit_bytes=VMEM_LIMIT)


def _rms(x, g):
    return x * lax.rsqrt(jnp.mean(x * x, axis=-1, keepdims=True) + NORM_EPS) * g


def _mm(a, b):
    return jnp.dot(a.astype(BF16), b.astype(BF16), preferred_element_type=F32)


def _mm_nt(a, b):
    return lax.dot_general(a.astype(BF16), b.astype(BF16), (((1,), (1,)), ((), ())),
                           preferred_element_type=F32)


def _mm_tn(a, b):
    return lax.dot_general(a.astype(BF16), b.astype(BF16), (((0,), (0,)), ((), ())),
                           preferred_element_type=F32)


def _sigmoid(x):
    return 1.0 / (1.0 + jnp.exp(-x))


def _head_lane_masks(width):
    lane = lax.broadcasted_iota(jnp.int32, (1, width), 1)
    return [(lane >= h * HEAD_DIM) & (lane < (h + 1) * HEAD_DIM) for h in range(width // HEAD_DIM)]


def _row_tile(n, want):
    t = min(n, want)
    assert n % t == 0, (n, t)
    return t


def _rotate_heads(y, cos, sin_signed):
    width = y.shape[1]
    cos = jnp.concatenate([cos] * (width // LANES), axis=1)
    sin_signed = jnp.concatenate([sin_signed] * (width // LANES), axis=1)
    lane = lax.broadcasted_iota(jnp.int32, y.shape, 1)
    first_half = (lane % HEAD_DIM) < (HEAD_DIM // 2)
    partner = jnp.where(first_half, pltpu.roll(y, width - HEAD_DIM // 2, axis=1),
                        pltpu.roll(y, HEAD_DIM // 2, axis=1))
    return y * cos + partner * sin_signed


def _to_class_view(y, o_ref, scr, dil):
    rows = y.shape[0] // dil
    for half in range(COL_TILE // LANES):
        scr[half] = y[:, half * LANES:(half + 1) * LANES]
    for cls in range(dil):
        for half in range(COL_TILE // LANES):
            lane0 = cls * COL_TILE + half * LANES
            o_ref[:, lane0:lane0 + LANES] = scr[half, pl.ds(cls, rows, stride=dil), :]


def _from_class_view(ref, scr, dil):
    rows = ref.shape[0]
    for cls in range(dil):
        for half in range(COL_TILE // LANES):
            lane0 = cls * COL_TILE + half * LANES
            scr[half, pl.ds(cls, rows, stride=dil), :] = ref[:, lane0:lane0 + LANES]
    return jnp.concatenate([scr[half] for half in range(COL_TILE // LANES)], axis=1)


def _class_view_scratch(tm):
    return pltpu.VMEM((COL_TILE // LANES, tm, LANES), F32)


def _norm_proj_kernel(x_ref, cos_ref, sin_ref, *refs, plan):
    n_proj = len(plan)
    n_outputs = sum(len(dils) for _, dils in plan)
    out_refs = list(refs[2 * n_proj:2 * n_proj + n_outputs])
    scratch = list(refs[2 * n_proj + n_outputs:])
    x = x_ref[...]
    inv_rms = lax.rsqrt(jnp.mean(x * x, axis=-1, keepdims=True) + NORM_EPS)
    for p, (n_rot, dils) in enumerate(plan):
        g_ref, w_ref = refs[2 * p], refs[2 * p + 1]
        xn = (x * inv_rms * g_ref[...]).astype(BF16)
        width = w_ref.shape[1] // len(dils)
        for j, dil in enumerate(dils):
            o_ref = out_refs.pop(0)
            y = jnp.dot(xn, w_ref[:, j * width:(j + 1) * width], preferred_element_type=F32)
            if j < n_rot:
                y = _rotate_heads(y, cos_ref[...], sin_ref[...])
            if dil == 1:
                o_ref[...] = y
            else:
                _to_class_view(y, o_ref, scratch.pop(0), dil)


def _resident(shape):
    return pl.BlockSpec(shape, lambda *_: (0,) * len(shape), pipeline_mode=pl.Buffered(1))


def _layer_resident(stacked, layer):
    shape = stacked.shape[1:]
    return pl.BlockSpec((None,) + shape, lambda *_: (layer,) + (0,) * len(shape),
                        pipeline_mode=pl.Buffered(1))


def _norm_proj(x, projections, cos, sin_signed, seq_len, *, tm_want=1024):
    n, d = x.shape
    plan = tuple((n_rot, tuple(dils)) for _, _, n_rot, dils in projections)
    widths = [w.shape[1] // len(dils) for _, w, _, dils in projections for _ in dils]
    all_dils = [dil for _, dils in plan for dil in dils]
    assert all(width == COL_TILE or dil == 1 for width, dil in zip(widths, all_dils))
    tm = _row_tile(n, tm_want)
    if tm <= seq_len:
        assert seq_len % tm == 0
        per_seq = seq_len // tm
        tab_map = lambda i: (i % per_seq, 0)
    else:
        assert tm % seq_len == 0
        cos = jnp.tile(cos, (tm // seq_len, 1))
        sin_signed = jnp.tile(sin_signed, (tm // seq_len, 1))
        tab_map = lambda i: (0, 0)
    assert all(tm % (8 * dil) == 0 or dil == 1 for dil in all_dils)
    weights, weight_specs = [], []
    for g, w_bf16, _, _ in projections:
        weights += [g.reshape(1, d), w_bf16]
        weight_specs += [_resident((1, d)), _resident(w_bf16.shape)]
    outs = pl.pallas_call(
        functools.partial(_norm_proj_kernel, plan=plan),
        grid=(n // tm,),
        in_specs=[pl.BlockSpec((tm, d), lambda i: (i, 0)),
                  pl.BlockSpec((tm, LANES), tab_map),
                  pl.BlockSpec((tm, LANES), tab_map)] + weight_specs,
        out_specs=[pl.BlockSpec((tm // dil, dil * width), lambda i: (i, 0))
                   for width, dil in zip(widths, all_dils)],
        out_shape=[jax.ShapeDtypeStruct((n // dil, dil * width), F32) for width, dil in zip(widths, all_dils)],
        scratch_shapes=[_class_view_scratch(tm) for dil in all_dils if dil > 1],
        compiler_params=_cparams("parallel"),
        name="norm_proj",
    )(x, cos, sin_signed, *weights)
    outs = list(outs)
    return [[outs.pop(0) for _ in dils] for _, dils in plan]


def _rwkv_proj_kernel(*refs, mix, has_vfirst, tm, seq_len, tail):
    it = iter(refs)
    h_ref = next(it)
    hp_ref = next(it) if seq_len >= tm else None
    sh_ref = next(it)
    g_ref, mu_ref, win_ref = next(it), next(it), next(it)
    w0_ref, w1_ref, w2_ref = next(it), next(it), next(it)
    a0_ref, a1_ref, a2_ref = next(it), next(it), next(it)
    g1_ref, g2_ref = next(it), next(it)
    if has_vfirst:
        vf_ref, v0_ref, v1_ref, v2_ref = next(it), next(it), next(it), next(it)
    r_ref, lw_ref, k_ref, v_ref, a_ref, gate_ref, qm_ref, tail_ref = (next(it) for _ in range(8))

    i = pl.program_id(0)
    gain = g_ref[...]
    xn = _rms(h_ref[...], gain)
    row = lax.broadcasted_iota(jnp.int32, (tm, 1), 0)
    rolled = pltpu.roll(xn, 1, axis=0)
    if seq_len >= tm:
        prev_tile_last = _rms(hp_ref[...], gain)[7:8, :]
        at_seq_start = (i * tm) % seq_len == 0
        first = jnp.where(at_seq_start, sh_ref[0][7:8, :], prev_tile_last)
        x_prev = jnp.where(row == 0, first, rolled)
    else:
        x_prev = jnp.where(row % seq_len == 0, sh_ref[...], rolled)
    xx = x_prev - xn
    mu = mu_ref[...]

    def mixed(idx):
        return (xn + xx * mu[idx:idx + 1, :]).astype(BF16)

    qm_ref[...] = jnp.dot(xn.astype(BF16), win_ref[:, 3 * mix:], preferred_element_type=F32)
    tail_ref[0] = xn[tm - tail:, :]
    r_ref[...] = jnp.dot(mixed(0), win_ref[:, 0:mix], preferred_element_type=F32)
    w_lin = w0_ref[...] + _mm(jnp.tanh(_mm(mixed(1), w1_ref[...])), w2_ref[...])
    lw_ref[...] = -math.exp(-0.5) * _sigmoid(w_lin)
    k_ref[...] = jnp.dot(mixed(2), win_ref[:, mix:2 * mix], preferred_element_type=F32)
    xv = mixed(3)
    v = jnp.dot(xv, win_ref[:, 2 * mix:3 * mix], preferred_element_type=F32)
    if has_vfirst:
        v = v + (vf_ref[...] - v) * _sigmoid(v0_ref[...] + _mm(_mm(xv, v1_ref[...]), v2_ref[...]))
    v_ref[...] = v
    a_ref[...] = _sigmoid(a0_ref[...] + _mm(_mm(mixed(4), a1_ref[...]), a2_ref[...]))
    gate_ref[...] = _mm(_sigmoid(_mm(mixed(5), g1_ref[...])), g2_ref[...])


def _rwkv_proj(h, shift, seq_len, W, li, v_first, *, tm_want=512):
    n, d = h.shape
    mix = W['a_w0'].shape[1]
    tm = _row_tile(n, tm_want)
    n_tiles = n // tm
    has_vfirst = v_first is not None
    row_spec = lambda w: pl.BlockSpec((tm, w), lambda i: (i, 0))
    full = lambda a: _resident(a.shape)

    args, specs = [h], [row_spec(d)]
    if seq_len >= tm:
        assert seq_len % tm == 0
        tail = 8
        args.append(h)
        specs.append(pl.BlockSpec((8, d), lambda i: (jnp.maximum(i * (tm // 8) - 1, 0), 0)))
        args.append(jnp.broadcast_to(shift[:, None, :], (shift.shape[0], 8, d)))
        specs.append(pl.BlockSpec((1, 8, d), lambda i: ((i * tm) // seq_len, 0, 0)))
    else:
        assert tm % seq_len == 0
        tail = tm
        args.append(jnp.repeat(shift, seq_len, axis=0))
        specs.append(row_spec(d))
    lead = [W['g_pre_mix'][li].reshape(1, d), W['a_mu'][li]]
    small = [W['a_w0'][li].reshape(1, mix), W['a_w1_bf16'][li], W['a_w2_bf16'][li],
             W['a_a0'][li].reshape(1, mix), W['a_a1_bf16'][li], W['a_a2_bf16'][li],
             W['a_g1_bf16'][li], W['a_g2_bf16'][li]]
    args += lead + [W['a_w_in_bf16']] + small
    specs += [full(a) for a in lead] + [_layer_resident(W['a_w_in_bf16'], li)] + [full(a) for a in small]
    if has_vfirst:
        vi = li - 1
        extra = [W['a_v0'][vi].reshape(1, mix), W['a_v1_bf16'][vi], W['a_v2_bf16'][vi]]
        args += [v_first] + extra
        specs += [row_spec(mix)] + [full(a) for a in extra]

    wide = jax.ShapeDtypeStruct((n, mix), F32)
    out_shape = [wide] * 6 + [jax.ShapeDtypeStruct((n, MEM_WIDTH), F32),
                              jax.ShapeDtypeStruct((n_tiles, tail, d), F32)]
    out_specs = [row_spec(mix)] * 6 + [row_spec(MEM_WIDTH),
                                       pl.BlockSpec((1, tail, d), lambda i: (i, 0, 0))]
    r, lw, k, v, a, gate, q_mem, xn_tail = pl.pallas_call(
        functools.partial(_rwkv_proj_kernel, mix=mix, has_vfirst=has_vfirst, tm=tm,
                          seq_len=seq_len, tail=tail),
        grid=(n_tiles,),
        in_specs=specs,
        out_specs=out_specs,
        out_shape=out_shape,
        compiler_params=_cparams("parallel"),
        name="rwkv_proj",
    )(*args)
    xn_rows = xn_tail.reshape(n_tiles * tail, d)
    last = xn_rows.reshape(-1, seq_len if seq_len < tm else tail * (seq_len // tm), d)[:, -1]
    return r, lw, k, v, a, gate, q_mem, last


def _head_sums(x, ones_bd):
    hi = x.astype(BF16)
    lo = (x - hi.astype(F32)).astype(BF16)
    tiles = []
    for t in range(x.shape[1] // COL_TILE):
        s = slice(t * COL_TILE, (t + 1) * COL_TILE)
        tiles.append(jnp.dot(hi[:, s], ones_bd, preferred_element_type=F32)
                     + jnp.dot(lo[:, s], ones_bd, preferred_element_type=F32))
    return jnp.concatenate(tiles, axis=1)


def _wkv_kernel(r_ref, lw_ref, k_ref, v_ref, a_ref, gate_ref, s0_ref, kkp_ref, kap_ref, rkp_ref,
                lnw_ref, lnb_ref, o_ref, sout_ref, s_scr, *, chunk, heads):
    c = pl.program_id(1)
    rows = r_ref.shape[0]
    mix = r_ref.shape[2]

    @pl.when(c == 0)
    def _():
        for b in range(rows):
            s_scr[b * heads:(b + 1) * heads] = s0_ref[b]

    side_by_side = lambda ref: jnp.concatenate([ref[b] for b in range(rows)], axis=1)
    per_row = lambda ref: jnp.concatenate([ref[...]] * rows, axis=1)
    r, lw, k, v, a = (side_by_side(ref) for ref in (r_ref, lw_ref, k_ref, v_ref, a_ref))
    row = lax.broadcasted_iota(jnp.int32, (chunk, chunk), 0)
    col = lax.broadcasted_iota(jnp.int32, (chunk, chunk), 1)
    strict = col < row
    incl = col <= row
    tri = incl.astype(BF16)
    lw_hi = lw.astype(BF16)
    lw_rest = lw - lw_hi.astype(F32)
    lw_mid = lw_rest.astype(BF16)
    lw_lo = (lw_rest - lw_mid.astype(F32)).astype(BF16)
    c_inc = (jnp.dot(tri, lw_hi, preferred_element_type=F32)
             + jnp.dot(tri, lw_mid, preferred_element_type=F32)
             + jnp.dot(tri, lw_lo, preferred_element_type=F32))
    c_last = c_inc[chunk - 1:chunk, :]
    e_inc = jnp.exp(c_inc)
    e_exc = jnp.exp(c_inc - lw)
    e_neg = jnp.exp(-c_inc)
    e_tail = jnp.exp(c_last - c_inc)
    p_last = jnp.exp(c_last)

    bd_r = lax.broadcasted_iota(jnp.int32, (COL_TILE, COL_TILE), 0) // HEAD_DIM
    bd_c = lax.broadcasted_iota(jnp.int32, (COL_TILE, COL_TILE), 1) // HEAD_DIM
    ones_bd = (bd_r == bd_c).astype(BF16)

    kk_raw = k * per_row(kkp_ref)
    kk = kk_raw * (1.0 / jnp.maximum(jnp.sqrt(_head_sums(kk_raw * kk_raw, ones_bd)), KK_EPS))
    kka = kk * a
    k2 = k * (1.0 + (a - 1.0) * per_row(kap_ref))
    bonus = _head_sums(r * k2 * per_row(rkp_ref), ones_bd) * v
    a_hat_f = -kk * e_exc
    r_hat_f = r * e_inc
    b_chk_f = kka * e_neg
    k_chk_f = k2 * e_neg
    b_til_f = kka * e_tail
    k_til_f = k2 * e_tail

    eye_h = (lax.broadcasted_iota(jnp.int32, (HEAD_DIM, HEAD_DIM), 0)
             == lax.broadcasted_iota(jnp.int32, (HEAD_DIM, HEAD_DIM), 1))
    n_levels = max(1, int(math.log2(chunk)))
    hs = range(rows * heads)
    sl = [slice(h * HEAD_DIM, (h + 1) * HEAD_DIM) for h in hs]
    a_hat = [a_hat_f[:, s] for s in sl]
    r_hat = [r_hat_f[:, s] for s in sl]
    v_h = [v[:, s] for s in sl]

    amat = [_mm_nt(jnp.concatenate([a_hat[h], r_hat[h]], axis=0),
                   jnp.concatenate([b_chk_f[:, sl[h]], k_chk_f[:, sl[h]]], axis=0)) for h in hs]
    a_ab = [jnp.where(strict, m[:chunk, :chunk], 0.0) for m in amat]
    a_ak = [jnp.where(strict, m[:chunk, chunk:], 0.0) for m in amat]
    a_rb = [jnp.where(incl, m[chunk:, :chunk], 0.0) for m in amat]
    a_rk = [jnp.where(incl, m[chunk:, chunk:], 0.0) for m in amat]
    av = [_mm(jnp.concatenate([a_ak[h], a_rk[h]], axis=0), v_h[h]) for h in hs]

    w12 = [jnp.concatenate([a_hat[h], av[h][:chunk]], axis=1) for h in hs]
    power = a_ab
    for level in range(n_levels):
        w12 = [w12[h] + _mm(power[h], w12[h]) for h in hs]
        if level + 1 < n_levels:
            power = [_mm(p, p) for p in power]

    rx = [_mm(a_rb[h], w12[h]) + jnp.concatenate([r_hat[h], av[h][chunk:]], axis=1) for h in hs]
    gh = [_mm_tn(w12[h], b_til_f[:, sl[h]]) for h in hs]
    vk = [_mm_tn(v_h[h], k_til_f[:, sl[h]]) for h in hs]
    g_t = [gh[h][:HEAD_DIM] + jnp.where(eye_h, p_last[:, sl[h]], 0.0) for h in hs]
    h_t = [gh[h][HEAD_DIM:] + vk[h] for h in hs]

    s_prev = [s_scr[h] for h in hs]
    o_h = [_mm_nt(rx[h][:, :HEAD_DIM], s_prev[h]) + rx[h][:, HEAD_DIM:] for h in hs]
    s_new = [_mm(s_prev[h], g_t[h]) + h_t[h] for h in hs]
    for h in hs:
        s_scr[h] = s_new[h]

    o = jnp.concatenate(o_h, axis=1)
    cen = o - _head_sums(o, ones_bd) * (1.0 / HEAD_DIM)
    var = _head_sums(cen * cen, ones_bd) * (1.0 / HEAD_DIM)
    o = cen * lax.rsqrt(var + LNX_EPS) * per_row(lnw_ref) + per_row(lnb_ref)
    o = (o + bonus) * side_by_side(gate_ref)
    for b in range(rows):
        o_ref[b] = o[:, b * mix:(b + 1) * mix]

    @pl.when(c == pl.num_programs(1) - 1)
    def _():
        for b in range(rows):
            sout_ref[b] = s_scr[b * heads:(b + 1) * heads]


def _wkv(r, lw, k, v, a, gate, s0, W, li, batch, seq_len, chunk):
    mix = r.shape[1]
    heads = mix // HEAD_DIM
    if seq_len % chunk:
        raise ValueError("sequence length must be a multiple of the chunk")
    if mix % COL_TILE:
        raise ValueError("mixer width must be a whole number of 4-head tiles")
    n_chunks = seq_len // chunk
    seq = lambda t: t.reshape(batch, seq_len, mix)
    rows = 2 if batch % 2 == 0 else 1
    blk = pl.BlockSpec((rows, chunk, mix), lambda b, c: (b, c, 0))
    st = pl.BlockSpec((rows, heads, HEAD_DIM, HEAD_DIM), lambda b, c: (b, 0, 0, 0))
    par = pl.BlockSpec((1, mix), lambda b, c: (0, 0))
    params = [W['a_k_k'][li], W['a_k_a'][li], W['a_r_k'][li], W['a_lnx_w'][li], W['a_lnx_b'][li]]
    o, s_out = pl.pallas_call(
        functools.partial(_wkv_kernel, chunk=chunk, heads=heads),
        grid=(batch // rows, n_chunks),
        in_specs=[blk] * 6 + [st] + [par] * 5,
        out_specs=[blk, st],
        out_shape=[jax.ShapeDtypeStruct((batch, seq_len, mix), F32),
                   jax.ShapeDtypeStruct(s0.shape, F32)],
        scratch_shapes=[pltpu.VMEM((rows * heads, HEAD_DIM, HEAD_DIM), F32)],
        compiler_params=_cparams("parallel", "arbitrary"),
        name="wkv7",
    )(seq(r), seq(lw), seq(k), seq(v), seq(a), seq(gate), s0, *[p.reshape(1, mix) for p in params])
    return o.reshape(batch * seq_len, mix), s_out


def _mem_attn_kernel(q_ref, kv_ref, o_ref):
    q = q_ref[0]
    tq = q.shape[0]
    mk = kv_ref[0, :, :MEM_WIDTH].astype(BF16)
    mv = kv_ref[0, :, MEM_WIDTH:].astype(BF16)
    masks = _head_lane_masks(MEM_WIDTH)
    stacked = jnp.concatenate([jnp.where(m, q, 0.0) for m in masks], axis=0)
    s = _mm_nt(stacked, mk) * ATTN_SCALE
    p = jnp.exp(s - jnp.max(s, axis=-1, keepdims=True))
    p = p * (1.0 / jnp.sum(p, axis=-1, keepdims=True))
    pv = _mm(p, mv)
    out = jnp.zeros(q.shape, F32)
    for h, m in enumerate(masks):
        out = out + jnp.where(m, pv[h * tq:(h + 1) * tq, :], 0.0)
    o_ref[0] = out


def _mem_attn(q_mem, mem_kv, batch, seq_len, *, tq_want=1024):
    tq = _row_tile(seq_len, tq_want)
    m = mem_kv.shape[1]
    out = pl.pallas_call(
        _mem_attn_kernel,
        grid=(batch, seq_len // tq),
        in_specs=[pl.BlockSpec((1, tq, MEM_WIDTH), lambda b, t: (b, t, 0)),
                  pl.BlockSpec((1, m, 2 * MEM_WIDTH), lambda b, t: (b, 0, 0))],
        out_specs=pl.BlockSpec((1, tq, MEM_WIDTH), lambda b, t: (b, t, 0)),
        out_shape=jax.ShapeDtypeStruct((batch, seq_len, MEM_WIDTH), F32),
        compiler_params=_cparams("parallel", "parallel"),
        name="mem_attn",
    )(q_mem.reshape(batch, seq_len, MEM_WIDTH), mem_kv)
    return out.reshape(batch * seq_len, MEM_WIDTH)


def _out_ffn_kernel(*refs, dils, mix_width, d_ff, tf):
    n_groups = len(dils)
    n_scratch = 2 * sum(dil > 1 for dil in dils)
    scratch = list(refs[len(refs) - n_scratch:])
    it = iter(refs[:len(refs) - n_scratch])
    if n_groups:
        read = lambda ref, dil: ref[...] if dil == 1 else _from_class_view(ref, scratch.pop(0), dil)
        og = [read(next(it), dil) for dil in dils]
        lse = [read(next(it), dil) for dil in dils]
        top = functools.reduce(jnp.maximum, lse)
        wgt = [jnp.exp(s - top) for s in lse]
        o_mix = sum(w * o for w, o in zip(wgt, og)) / sum(wgt)
    else:
        o_mix = next(it)[...]
    om_ref, wout_ref, gpm_ref, h_ref, gpre_ref, win_ref, wo_ref, gpost_ref, o_ref, acc_ref = it
    mixed = (jnp.dot(o_mix.astype(BF16), wout_ref[:mix_width, :], preferred_element_type=F32)
             + jnp.dot(om_ref[...].astype(BF16), wout_ref[mix_width:, :], preferred_element_type=F32))
    h1 = h_ref[...] + _rms(mixed, gpm_ref[...])
    xn = _rms(h1, gpre_ref[...]).astype(BF16)
    for j in range(d_ff // tf):
        gate = jnp.dot(xn, win_ref[:, j * tf:(j + 1) * tf], preferred_element_type=F32)
        up = jnp.dot(xn, win_ref[:, d_ff + j * tf:d_ff + (j + 1) * tf], preferred_element_type=F32)
        act = (gate * _sigmoid(gate) * up).astype(BF16)
        part = jnp.dot(act, wo_ref[j * tf:(j + 1) * tf, :], preferred_element_type=F32)
        if j == 0:
            acc_ref[...] = part
        else:
            acc_ref[...] += part
    o_ref[...] = h1 + _rms(acc_ref[...], gpost_ref[...])


def _out_ffn(o_mix, lse, dils, o_mem, h, W, l, w_out_stack, w_out_layer, *, tm_want=512, tf=256):
    n, d = h.shape
    tm = _row_tile(n, tm_want)
    w_in, w_o = W['w_ffn_in_bf16'], W['w_ffn_out_bf16']
    d_ff = w_o.shape[1]
    assert d_ff % tf == 0
    mix_width = w_out_stack.shape[1] - MEM_WIDTH
    row = lambda w: pl.BlockSpec((tm, w), lambda i: (i, 0))
    if lse is not None:
        lead = list(o_mix) + list(lse)
        lead_specs = [pl.BlockSpec((tm // dil, dil * COL_TILE), lambda i: (i, 0)) for dil in dils] * 2
    else:
        lead, lead_specs, dils = [o_mix], [row(o_mix.shape[1])], ()
    vec = lambda g: g.reshape(1, d)
    return pl.pallas_call(
        functools.partial(_out_ffn_kernel, dils=tuple(dils), mix_width=mix_width, d_ff=d_ff, tf=tf),
        grid=(n // tm,),
        in_specs=lead_specs + [
            row(MEM_WIDTH), _layer_resident(w_out_stack, w_out_layer), _resident((1, d)), row(d),
            _resident((1, d)), _layer_resident(w_in, l), _layer_resident(w_o, l), _resident((1, d))],
        out_specs=row(d),
        out_shape=jax.ShapeDtypeStruct((n, d), F32),
        scratch_shapes=[pltpu.VMEM((tm, d), F32)] + [_class_view_scratch(tm) for dil in dils if dil > 1] * 2,
        compiler_params=_cparams("parallel"),
        name="out_ffn",
    )(*lead, o_mem, w_out_stack, vec(W['g_post_mix'][l]), h, vec(W['g_pre_ffn'][l]), w_in, w_o,
      vec(W['g_post_ffn'][l]))


def _dil_prompt_kernel(q_ref, kc_ref, kh_ref, vc_ref, vh_ref, o_ref, lse_ref, kx_ref, vx_ref, *, tile):
    t = pl.program_id(2)
    kx_ref[0:BAND, :] = kh_ref[0].astype(BF16)
    kx_ref[BAND:, :] = kc_ref[0].astype(BF16)
    vx_ref[0:BAND, :] = vh_ref[0].astype(BF16)
    vx_ref[BAND:, :] = vc_ref[0].astype(BF16)

    rows = HEADS_PER_TILE * BAND
    qi = lax.broadcasted_iota(jnp.int32, (rows, 2 * BAND), 0) % BAND
    kj = lax.broadcasted_iota(jnp.int32, (rows, 2 * BAND), 1)
    band = (kj >= qi) & (kj <= qi + BAND)
    masks = _head_lane_masks(COL_TILE)

    def block(blk, carry):
        start = pl.multiple_of(blk * BAND, BAND)
        q = q_ref[0, pl.ds(start, BAND), :] * ATTN_SCALE
        keys = kx_ref[pl.ds(start, 2 * BAND), :]
        vals = vx_ref[pl.ds(start, 2 * BAND), :]
        lo = jnp.where((t == 0) & (blk == 0), BAND, 0)
        valid = band & (kj >= lo)
        stacked = jnp.concatenate([jnp.where(m, q, 0.0) for m in masks], axis=0)
        s = jnp.where(valid, _mm_nt(stacked, keys), -jnp.inf)
        top = jnp.max(s, axis=-1, keepdims=True)
        p = jnp.exp(s - top)
        den = jnp.sum(p, axis=-1, keepdims=True)
        pv = _mm(p, vals) * (1.0 / den)
        lse_rows = top + jnp.log(den)
        out = jnp.zeros((BAND, COL_TILE), F32)
        lse = jnp.zeros((BAND, COL_TILE), F32)
        for h, m in enumerate(masks):
            out = out + jnp.where(m, pv[h * BAND:(h + 1) * BAND, :], 0.0)
            lse = lse + jnp.where(m, lse_rows[h * BAND:(h + 1) * BAND, :], 0.0)
        o_ref[0, pl.ds(start, BAND), :] = out
        lse_ref[0, pl.ds(start, BAND), :] = lse
        return carry

    lax.fori_loop(0, tile // BAND, block, 0, unroll=2)


def _dil_prompt_group(q, k, v, batch, seq_len, dil):
    cls_len = seq_len // dil
    tile = min(cls_len, 1024)
    assert cls_len % tile == 0 and tile % BAND == 0
    n_t = cls_len // tile
    view = lambda x: x.reshape(batch, cls_len, dil * COL_TILE)
    cur = pl.BlockSpec((1, tile, COL_TILE), lambda b, c, t: (b, t, c))
    halo = pl.BlockSpec((1, BAND, COL_TILE),
                        lambda b, c, t: (b, jnp.maximum(t * (tile // BAND) - 1, 0), c))
    out_shape = jax.ShapeDtypeStruct((batch, cls_len, dil * COL_TILE), F32)
    o, lse = pl.pallas_call(
        functools.partial(_dil_prompt_kernel, tile=tile),
        grid=(batch, dil, n_t),
        in_specs=[cur, cur, halo, cur, halo],
        out_specs=[cur, cur],
        out_shape=[out_shape, out_shape],
        scratch_shapes=[pltpu.VMEM((tile + BAND, COL_TILE), BF16)] * 2,
        compiler_params=_cparams("parallel", "parallel", "arbitrary"),
        name="dil_attn_prompt",
    )(view(q), view(k), view(k), view(v), view(v))
    flat = lambda x: x.reshape(batch * cls_len, dil * COL_TILE)
    return flat(o), flat(lse)


def _dil_sample_kernel(*refs, seq_len, write_cache):
    q_refs = refs[:N_GROUPS]
    k_refs = refs[N_GROUPS:2 * N_GROUPS]
    v_refs = refs[2 * N_GROUPS:3 * N_GROUPS]
    cache_refs = refs[3 * N_GROUPS:4 * N_GROUPS]
    o_ref = refs[4 * N_GROUPS]
    new_refs = refs[4 * N_GROUPS + 1:]
    masks = _head_lane_masks(COL_TILE)
    rows = HEADS_PER_TILE * seq_len

    run_top = jnp.full((rows, 1), -jnp.inf, F32)
    run_den = jnp.zeros((rows, 1), F32)
    run_acc = jnp.zeros((rows, COL_TILE), F32)
    for g, (win, dil) in enumerate(DIL_GROUPS):
        cache = cache_refs[g]
        length = cache.shape[1]
        q = q_refs[g][...] * ATTN_SCALE
        stacked = jnp.concatenate([jnp.where(m, q, 0.0) for m in masks], axis=0)
        k_new, v_new = k_refs[g][...], v_refs[g][...]
        ck = cache[0, :, :COL_TILE]
        cv = cache[0, :, COL_TILE:]
        s_c = _mm_nt(stacked, ck)
        s_n = _mm_nt(stacked, k_new)
        tq_c = lax.broadcasted_iota(jnp.int32, s_c.shape, 0) % seq_len
        dist_c = length + tq_c - lax.broadcasted_iota(jnp.int32, s_c.shape, 1)
        ok_c = (dist_c % dil == 0) & (dist_c <= win)
        tq_n = lax.broadcasted_iota(jnp.int32, s_n.shape, 0) % seq_len
        dist_n = tq_n - lax.broadcasted_iota(jnp.int32, s_n.shape, 1)
        ok_n = (dist_n >= 0) & (dist_n % dil == 0)
        s_c = jnp.where(ok_c, s_c, -jnp.inf)
        s_n = jnp.where(ok_n, s_n, -jnp.inf)
        top = jnp.maximum(jnp.maximum(jnp.max(s_c, axis=-1, keepdims=True),
                                      jnp.max(s_n, axis=-1, keepdims=True)), run_top)
        p_c = jnp.exp(s_c - top)
        p_n = jnp.exp(s_n - top)
        scale = jnp.exp(run_top - top)
        run_den = run_den * scale + jnp.sum(p_c, axis=-1, keepdims=True) + jnp.sum(p_n, axis=-1, keepdims=True)
        run_acc = run_acc * scale + _mm(p_c, cv) + _mm(p_n, v_new)
        run_top = top
        if write_cache:
            new = new_refs[g]
            if length > seq_len:
                new[0, 0:length - seq_len, :] = cache[0, seq_len:, :]
            new[0, length - seq_len:, :COL_TILE] = k_new
            new[0, length - seq_len:, COL_TILE:] = v_new
    res = run_acc / run_den
    out = jnp.zeros((seq_len, COL_TILE), F32)
    for h, m in enumerate(masks):
        out = out + jnp.where(m, res[h * seq_len:(h + 1) * seq_len, :], 0.0)
    o_ref[...] = out


def _dil_sample(q_slabs, kv_slabs, caches, batch, seq_len, write_cache):
    n = batch * seq_len
    in_specs = [pl.BlockSpec((seq_len, COL_TILE), lambda b: (b, 0))] * (3 * N_GROUPS)
    cache_specs = [pl.BlockSpec((1,) + c.shape[1:], lambda b: (b, 0, 0)) for c in caches]
    out_shape = [jax.ShapeDtypeStruct((n, COL_TILE), F32)]
    out_specs = [pl.BlockSpec((seq_len, COL_TILE), lambda b: (b, 0))]
    if write_cache:
        out_shape += [jax.ShapeDtypeStruct(c.shape, F32) for c in caches]
        out_specs += cache_specs
    res = pl.pallas_call(
        functools.partial(_dil_sample_kernel, seq_len=seq_len, write_cache=write_cache),
        grid=(batch,),
        in_specs=in_specs + cache_specs,
        out_specs=out_specs,
        out_shape=out_shape,
        compiler_params=_cparams("parallel"),
        name="dil_attn_sample",
    )(*q_slabs[:N_GROUPS], *kv_slabs, *caches)
    return res[0], list(res[1:])


def _rope_tables(pos):
    half = HEAD_DIM // 2
    inv = ROPE_THETA ** (-jnp.arange(half, dtype=F32) / half)
    ang = pos.astype(F32)[:, None] * inv[None, :]
    cos, sin = jnp.cos(ang), jnp.sin(ang)
    cos = jnp.tile(jnp.concatenate([cos, cos], axis=-1), (1, LANES // HEAD_DIM))
    sin_signed = jnp.tile(jnp.concatenate([-sin, sin], axis=-1), (1, LANES // HEAD_DIM))
    return cos, sin_signed


def _trunk(h, pos, mem_kv_all, wkv0, shift0, caches, W, batch, seq_len, chunk):
    depth = W['g_pre_mix'].shape[0]
    n_a = W['a_mu'].shape[0]
    cos, sin_signed = _rope_tables(pos)
    new_wkv, new_shift, new_caches = [], [], None
    v_first = None
    kv_tiles = None
    for l in range(depth):
        if l < n_a:
            r, lw, k, v, a, gate, q_mem, xn_last = _rwkv_proj(h, shift0[l], seq_len, W, l, v_first)
            if l == 0:
                v_first = v
            o_mix, s_out = _wkv(r, lw, k, v, a, gate, wkv0[l], W, l, batch, seq_len, chunk)
            new_wkv.append(s_out)
            new_shift.append(xn_last)
            lse = None
            w_out, w_out_layer = W['a_w_out_bf16'], l
        else:
            j = l - n_a
            dils = tuple(dil for _, dil in DIL_GROUPS) if caches is None else (1,) * N_GROUPS
            q_proj = (W['g_pre_mix'][l], W['b_w_in_bf16'][j], N_GROUPS, dils + (1,))
            if kv_tiles is None:
                kv_proj = (W['g_kv'], W['w_kv_bf16'], N_GROUPS, dils * 2)
                kv_tiles, proj = _norm_proj(h, [kv_proj, q_proj], cos, sin_signed, seq_len)
            else:
                proj, = _norm_proj(h, [q_proj], cos, sin_signed, seq_len)
            q_mem = proj[N_GROUPS]
            if caches is None:
                outs = [_dil_prompt_group(proj[g], kv_tiles[g], kv_tiles[N_GROUPS + g], batch, seq_len, dil)
                        for g, dil in enumerate(dils)]
                o_mix = [o for o, _ in outs]
                lse = [s for _, s in outs]
            else:
                o_mix, written = _dil_sample(proj, kv_tiles, caches, batch, seq_len, new_caches is None)
                if new_caches is None:
                    new_caches = written
                lse = None
            w_out, w_out_layer = W['b_w_out_bf16'], j
        o_mem = _mem_attn(q_mem, mem_kv_all[l], batch, seq_len)
        h = _out_ffn(o_mix, lse, dils if lse is not None else (), o_mem, h, W, l, w_out, w_out_layer)
    return h, jnp.stack(new_wkv), jnp.stack(new_shift), kv_tiles, new_caches


def _prompt_windows(kv_tiles, batch, seq_len):
    bufs = []
    for g, (win, dil) in enumerate(DIL_GROUPS):
        length = min(win, PAST_LEN)
        assert seq_len >= length and length % dil == 0
        rows = seq_len // dil

        def tail(t):
            t = t.reshape(batch, rows, dil * COL_TILE)[:, rows - length // dil:]
            return t.reshape(batch, length, HEADS_PER_TILE, HEAD_DIM)

        bufs.append(jnp.stack([tail(kv_tiles[g]), tail(kv_tiles[N_GROUPS + g])], axis=2))
    return bufs


def kernel(x_prompt, x_sample, state_wkv, state_shift, cache_win_g0, cache_win_g1, cache_win_g2, cache_mem, mem_prompt, g_pre_mix, g_post_mix, g_pre_ffn, g_post_ffn, g_mem, w_mem_kv, w_ffn_in, w_ffn_out, a_mu, a_w_in, a_w0, a_w1, a_w2, a_a0, a_a1, a_a2, a_v0, a_v1, a_v2, a_g1, a_g2, a_k_k, a_k_a, a_r_k, a_lnx_w, a_lnx_b, a_w_out, g_kv, w_kv, b_w_in, b_w_out):
    W = dict(g_pre_mix=g_pre_mix, g_post_mix=g_post_mix, g_pre_ffn=g_pre_ffn, g_post_ffn=g_post_ffn,
             a_mu=a_mu, a_w0=a_w0, a_a0=a_a0, a_v0=a_v0, a_k_k=a_k_k, a_k_a=a_k_a,
             a_r_k=a_r_k.reshape(a_r_k.shape[0], -1), a_lnx_w=a_lnx_w, a_lnx_b=a_lnx_b, g_kv=g_kv)
    for name, w in dict(w_ffn_in=w_ffn_in, w_ffn_out=w_ffn_out, a_w_in=a_w_in, a_w1=a_w1, a_w2=a_w2,
                        a_a1=a_a1, a_a2=a_a2, a_v1=a_v1, a_v2=a_v2, a_g1=a_g1, a_g2=a_g2,
                        a_w_out=a_w_out, w_kv=w_kv, b_w_in=b_w_in, b_w_out=b_w_out,
                        w_mem_kv=w_mem_kv).items():
        W[name + '_bf16'] = w.astype(BF16)

    bp, tp, d = x_prompt.shape
    bs, ts, _ = x_sample.shape
    depth = g_pre_mix.shape[0]
    n_a = a_mu.shape[0]
    heads = a_w0.shape[1] // HEAD_DIM
    m_tok = mem_prompt.shape[1]

    mem_rows = mem_prompt.reshape(bp * m_tok, d)
    no_tab = jnp.zeros((m_tok, LANES), F32)
    mem_kv_p = _norm_proj(mem_rows, [(g_mem[l], W['w_mem_kv_bf16'][l], 0, (1,)) for l in range(depth)],
                          no_tab, no_tab, m_tok)
    mem_kv_p = jnp.stack([slabs[0].reshape(bp, m_tok, 2 * MEM_WIDTH) for slabs in mem_kv_p])
    wkv_zero = jnp.zeros((n_a, bp, heads, HEAD_DIM, HEAD_DIM), F32)
    shift_zero = jnp.zeros((n_a, bp, d), F32)
    y_p, wkv_p, shift_p, kv_p, _ = _trunk(
        x_prompt.reshape(bp * tp, d), jnp.arange(tp, dtype=jnp.int32), mem_kv_p, wkv_zero, shift_zero,
        None, W, bp, tp, chunk=64)
    bufs_p = _prompt_windows(kv_p, bp, tp)

    caches = [c.reshape(bs, c.shape[1], 2 * COL_TILE) for c in (cache_win_g0, cache_win_g1, cache_win_g2)]
    mem_kv_s = cache_mem.reshape(depth, bs, m_tok, 2 * MEM_WIDTH)
    y_s, wkv_s, shift_s, _, bufs_s = _trunk(
        x_sample.reshape(bs * ts, d), PAST_LEN + jnp.arange(ts, dtype=jnp.int32), mem_kv_s, state_wkv,
        state_shift, caches, W, bs, ts, chunk=ts)
    bufs_s = [b.reshape(c.shape) for b, c in zip(bufs_s, (cache_win_g0, cache_win_g1, cache_win_g2))]

    return (y_p.reshape(bp, tp, d), y_s.reshape(bs, ts, d), wkv_p, shift_p, bufs_p[0], bufs_p[1], bufs_p[2],
            mem_kv_p.reshape(depth, bp, m_tok, 2, MEM_HEADS, HEAD_DIM),
            wkv_s, shift_s, bufs_s[0], bufs_s[1], bufs_s[2])
```

```python
import functools
import math

import jax
import jax.numpy as jnp
from jax import lax
from jax.experimental import pallas as pl
from jax.experimental.pallas import tpu as pltpu

F32 = jnp.float32
BF16 = jnp.bfloat16

HEAD_DIM = 64
LANES = 128
COL_TILE = 256
HEADS_PER_TILE = COL_TILE // HEAD_DIM
MEM_HEADS = 4
MEM_WIDTH = MEM_HEADS * HEAD_DIM
DIL_GROUPS = ((128, 1), (512, 4), (2048, 16))
N_GROUPS = len(DIL_GROUPS)
PAST_LEN = 8192
ROPE_THETA = 10000.0
NORM_EPS = 1e-6
LNX_EPS = HEAD_DIM * 1e-5
KK_EPS = 1e-12
ATTN_SCALE = HEAD_DIM ** -0.5
BAND = 128
VMEM_LIMIT = 56 * 1024 * 1024


def _cparams(*sem):
    return pltpu.CompilerParams(dimension_semantics=sem, vmem_limit_bytes=VMEM_LIMIT)


def _rms(x, g):
    return x * lax.rsqrt(jnp.mean(x * x, axis=-1, keepdims=True) + NORM_EPS) * g


def _mm(a, b):
    return jnp.dot(a.astype(BF16), b.astype(BF16), preferred_element_type=F32)


def _mm_nt(a, b):
    return lax.dot_general(a.astype(BF16), b.astype(BF16), (((1,), (1,)), ((), ())),
                           preferred_element_type=F32)


def _mm_tn(a, b):
    return lax.dot_general(a.astype(BF16), b.astype(BF16), (((0,), (0,)), ((), ())),
                           preferred_element_type=F32)


def _sigmoid(x):
    return 1.0 / (1.0 + jnp.exp(-x))


def _head_lane_masks(width):
    lane = lax.broadcasted_iota(jnp.int32, (1, width), 1)
    return [(lane >= h * HEAD_DIM) & (lane < (h + 1) * HEAD_DIM) for h in range(width // HEAD_DIM)]


def _row_tile(n, want):
    t = min(n, want)
    assert n % t == 0, (n, t)
    return t


def _rotate_heads(y, cos, sin_signed):
    width = y.shape[1]
    cos = jnp.concatenate([cos] * (width // LANES), axis=1)
    sin_signed = jnp.concatenate([sin_signed] * (width // LANES), axis=1)
    lane = lax.broadcasted_iota(jnp.int32, y.shape, 1)
    first_half = (lane % HEAD_DIM) < (HEAD_DIM // 2)
    partner = jnp.where(first_half, pltpu.roll(y, width - HEAD_DIM // 2, axis=1),
                        pltpu.roll(y, HEAD_DIM // 2, axis=1))
    return y * cos + partner * sin_signed


def _to_class_view(y, o_ref, scr, dil):
    rows = y.shape[0] // dil
    for half in range(COL_TILE // LANES):
        scr[half] = y[:, half * LANES:(half + 1) * LANES]
    for cls in range(dil):
        for half in range(COL_TILE // LANES):
            lane0 = cls * COL_TILE + half * LANES
            o_ref[:, lane0:lane0 + LANES] = scr[half, pl.ds(cls, rows, stride=dil), :]


def _from_class_view(ref, scr, dil):
    rows = ref.shape[0]
    for cls in range(dil):
        for half in range(COL_TILE // LANES):
            lane0 = cls * COL_TILE + half * LANES
            scr[half, pl.ds(cls, rows, stride=dil), :] = ref[:, lane0:lane0 + LANES]
    return jnp.concatenate([scr[half] for half in range(COL_TILE // LANES)], axis=1)


def _class_view_scratch(tm):
    return pltpu.VMEM((COL_TILE // LANES, tm, LANES), F32)


def _norm_proj_kernel(x_ref, cos_ref, sin_ref, *refs, plan):
    n_proj = len(plan)
    n_outputs = sum(len(dils) for _, dils in plan)
    out_refs = list(refs[2 * n_proj:2 * n_proj + n_outputs])
    scratch = list(refs[2 * n_proj + n_outputs:])
    x = x_ref[...]
    inv_rms = lax.rsqrt(jnp.mean(x * x, axis=-1, keepdims=True) + NORM_EPS)
    for p, (n_rot, dils) in enumerate(plan):
        g_ref, w_ref = refs[2 * p], refs[2 * p + 1]
        xn = (x * inv_rms * g_ref[...]).astype(BF16)
        width = w_ref.shape[1] // len(dils)
        for j, dil in enumerate(dils):
            o_ref = out_refs.pop(0)
            y = jnp.dot(xn, w_ref[:, j * width:(j + 1) * width], preferred_element_type=F32)
            if j < n_rot:
                y = _rotate_heads(y, cos_ref[...], sin_ref[...])
            if dil == 1:
                o_ref[...] = y
            else:
                _to_class_view(y, o_ref, scratch.pop(0), dil)


def _resident(shape):
    return pl.BlockSpec(shape, lambda *_: (0,) * len(shape), pipeline_mode=pl.Buffered(1))


def _layer_resident(stacked, layer):
    shape = stacked.shape[1:]
    return pl.BlockSpec((None,) + shape, lambda *_: (layer,) + (0,) * len(shape),
                        pipeline_mode=pl.Buffered(1))


def _norm_proj(x, projections, cos, sin_signed, seq_len, *, tm_want=1024):
    n, d = x.shape
    plan = tuple((n_rot, tuple(dils)) for _, _, n_rot, dils in projections)
    widths = [w.shape[1] // len(dils) for _, w, _, dils in projections for _ in dils]
    all_dils = [dil for _, dils in plan for dil in dils]
    assert all(width == COL_TILE or dil == 1 for width, dil in zip(widths, all_dils))
    tm = _row_tile(n, tm_want)
    if tm <= seq_len:
        assert seq_len % tm == 0
        per_seq = seq_len // tm
        tab_map = lambda i: (i % per_seq, 0)
    else:
        assert tm % seq_len == 0
        cos = jnp.tile(cos, (tm // seq_len, 1))
        sin_signed = jnp.tile(sin_signed, (tm // seq_len, 1))
        tab_map = lambda i: (0, 0)
    assert all(tm % (8 * dil) == 0 or dil == 1 for dil in all_dils)
    weights, weight_specs = [], []
    for g, w_bf16, _, _ in projections:
        weights += [g.reshape(1, d), w_bf16]
        weight_specs += [_resident((1, d)), _resident(w_bf16.shape)]
    outs = pl.pallas_call(
        functools.partial(_norm_proj_kernel, plan=plan),
        grid=(n // tm,),
        in_specs=[pl.BlockSpec((tm, d), lambda i: (i, 0)),
                  pl.BlockSpec((tm, LANES), tab_map),
                  pl.BlockSpec((tm, LANES), tab_map)] + weight_specs,
        out_specs=[pl.BlockSpec((tm // dil, dil * width), lambda i: (i, 0))
                   for width, dil in zip(widths, all_dils)],
        out_shape=[jax.ShapeDtypeStruct((n // dil, dil * width), F32) for width, dil in zip(widths, all_dils)],
        scratch_shapes=[_class_view_scratch(tm) for dil in all_dils if dil > 1],
        compiler_params=_cparams("parallel"),
        name="norm_proj",
    )(x, cos, sin_signed, *weights)
    outs = list(outs)
    return [[outs.pop(0) for _ in dils] for _, dils in plan]


def _rwkv_proj_kernel(*refs, mix, has_vfirst, tm, seq_len, tail):
    it = iter(refs)
    h_ref = next(it)
    hp_ref = next(it) if seq_len >= tm else None
    sh_ref = next(it)
    g_ref, mu_ref, win_ref = next(it), next(it), next(it)
    w0_ref, w1_ref, w2_ref = next(it), next(it), next(it)
    a0_ref, a1_ref, a2_ref = next(it), next(it), next(it)
    g1_ref, g2_ref = next(it), next(it)
    if has_vfirst:
        vf_ref, v0_ref, v1_ref, v2_ref = next(it), next(it), next(it), next(it)
    r_ref, lw_ref, k_ref, v_ref, a_ref, gate_ref, qm_ref, tail_ref = (next(it) for _ in range(8))

    i = pl.program_id(0)
    gain = g_ref[...]
    xn = _rms(h_ref[...], gain)
    row = lax.broadcasted_iota(jnp.int32, (tm, 1), 0)
    rolled = pltpu.roll(xn, 1, axis=0)
    if seq_len >= tm:
        prev_tile_last = _rms(hp_ref[...], gain)[7:8, :]
        at_seq_start = (i * tm) % seq_len == 0
        first = jnp.where(at_seq_start, sh_ref[0][7:8, :], prev_tile_last)
        x_prev = jnp.where(row == 0, first, rolled)
    else:
        x_prev = jnp.where(row % seq_len == 0, sh_ref[...], rolled)
    xx = x_prev - xn
    mu = mu_ref[...]

    def mixed(idx):
        return (xn + xx * mu[idx:idx + 1, :]).astype(BF16)

    qm_ref[...] = jnp.dot(xn.astype(BF16), win_ref[:, 3 * mix:], preferred_element_type=F32)
    tail_ref[0] = xn[tm - tail:, :]
    r_ref[...] = jnp.dot(mixed(0), win_ref[:, 0:mix], preferred_element_type=F32)
    w_lin = w0_ref[...] + _mm(jnp.tanh(_mm(mixed(1), w1_ref[...])), w2_ref[...])
    lw_ref[...] = -math.exp(-0.5) * _sigmoid(w_lin)
    k_ref[...] = jnp.dot(mixed(2), win_ref[:, mix:2 * mix], preferred_element_type=F32)
    xv = mixed(3)
    v = jnp.dot(xv, win_ref[:, 2 * mix:3 * mix], preferred_element_type=F32)
    if has_vfirst:
        v = v + (vf_ref[...] - v) * _sigmoid(v0_ref[...] + _mm(_mm(xv, v1_ref[...]), v2_ref[...]))
    v_ref[...] = v
    a_ref[...] = _sigmoid(a0_ref[...] + _mm(_mm(mixed(4), a1_ref[...]), a2_ref[...]))
    gate_ref[...] = _mm(_sigmoid(_mm(mixed(5), g1_ref[...])), g2_ref[...])


def _rwkv_proj(h, shift, seq_len, W, li, v_first, *, tm_want=512):
    n, d = h.shape
    mix = W['a_w0'].shape[1]
    tm = _row_tile(n, tm_want)
    n_tiles = n // tm
    has_vfirst = v_first is not None
    row_spec = lambda w: pl.BlockSpec((tm, w), lambda i: (i, 0))
    full = lambda a: _resident(a.shape)

    args, specs = [h], [row_spec(d)]
    if seq_len >= tm:
        assert seq_len % tm == 0
        tail = 8
        args.append(h)
        specs.append(pl.BlockSpec((8, d), lambda i: (jnp.maximum(i * (tm // 8) - 1, 0), 0)))
        args.append(jnp.broadcast_to(shift[:, None, :], (shift.shape[0], 8, d)))
        specs.append(pl.BlockSpec((1, 8, d), lambda i: ((i * tm) // seq_len, 0, 0)))
    else:
        assert tm % seq_len == 0
        tail = tm
        args.append(jnp.repeat(shift, seq_len, axis=0))
        specs.append(row_spec(d))
    lead = [W['g_pre_mix'][li].reshape(1, d), W['a_mu'][li]]
    small = [W['a_w0'][li].reshape(1, mix), W['a_w1_bf16'][li], W['a_w2_bf16'][li],
             W['a_a0'][li].reshape(1, mix), W['a_a1_bf16'][li], W['a_a2_bf16'][li],
             W['a_g1_bf16'][li], W['a_g2_bf16'][li]]
    args += lead + [W['a_w_in_bf16']] + small
    specs += [full(a) for a in lead] + [_layer_resident(W['a_w_in_bf16'], li)] + [full(a) for a in small]
    if has_vfirst:
        vi = li - 1
        extra = [W['a_v0'][vi].reshape(1, mix), W['a_v1_bf16'][vi], W['a_v2_bf16'][vi]]
        args += [v_first] + extra
        specs += [row_spec(mix)] + [full(a) for a in extra]

    wide = jax.ShapeDtypeStruct((n, mix), F32)
    out_shape = [wide] * 6 + [jax.ShapeDtypeStruct((n, MEM_WIDTH), F32),
                              jax.ShapeDtypeStruct((n_tiles, tail, d), F32)]
    out_specs = [row_spec(mix)] * 6 + [row_spec(MEM_WIDTH),
                                       pl.BlockSpec((1, tail, d), lambda i: (i, 0, 0))]
    r, lw, k, v, a, gate, q_mem, xn_tail = pl.pallas_call(
        functools.partial(_rwkv_proj_kernel, mix=mix, has_vfirst=has_vfirst, tm=tm,
                          seq_len=seq_len, tail=tail),
        grid=(n_tiles,),
        in_specs=specs,
        out_specs=out_specs,
        out_shape=out_shape,
        compiler_params=_cparams("parallel"),
        name="rwkv_proj",
    )(*args)
    xn_rows = xn_tail.reshape(n_tiles * tail, d)
    last = xn_rows.reshape(-1, seq_len if seq_len < tm else tail * (seq_len // tm), d)[:, -1]
    return r, lw, k, v, a, gate, q_mem, last


def _head_sums(x, ones_bd):
    hi = x.astype(BF16)
    lo = (x - hi.astype(F32)).astype(BF16)
    tiles = []
    for t in range(x.shape[1] // COL_TILE):
        s = slice(t * COL_TILE, (t + 1) * COL_TILE)
        tiles.append(jnp.dot(hi[:, s], ones_bd, preferred_element_type=F32)
                     + jnp.dot(lo[:, s], ones_bd, preferred_element_type=F32))
    return jnp.concatenate(tiles, axis=1)


def _wkv_kernel(r_ref, lw_ref, k_ref, v_ref, a_ref, gate_ref, s0_ref, kkp_ref, kap_ref, rkp_ref,
                lnw_ref, lnb_ref, o_ref, sout_ref, s_scr, *, chunk, heads):
    c = pl.program_id(1)
    rows = r_ref.shape[0]
    mix = r_ref.shape[2]

    @pl.when(c == 0)
    def _():
        for b in range(rows):
            s_scr[b * heads:(b + 1) * heads] = s0_ref[b]

    side_by_side = lambda ref: jnp.concatenate([ref[b] for b in range(rows)], axis=1)
    per_row = lambda ref: jnp.concatenate([ref[...]] * rows, axis=1)
    r, lw, k, v, a = (side_by_side(ref) for ref in (r_ref, lw_ref, k_ref, v_ref, a_ref))
    row = lax.broadcasted_iota(jnp.int32, (chunk, chunk), 0)
    col = lax.broadcasted_iota(jnp.int32, (chunk, chunk), 1)
    strict = col < row
    incl = col <= row
    tri = incl.astype(BF16)
    lw_hi = lw.astype(BF16)
    lw_rest = lw - lw_hi.astype(F32)
    lw_mid = lw_rest.astype(BF16)
    lw_lo = (lw_rest - lw_mid.astype(F32)).astype(BF16)
    c_inc = (jnp.dot(tri, lw_hi, preferred_element_type=F32)
             + jnp.dot(tri, lw_mid, preferred_element_type=F32)
             + jnp.dot(tri, lw_lo, preferred_element_type=F32))
    c_last = c_inc[chunk - 1:chunk, :]
    e_inc = jnp.exp(c_inc)
    e_exc = jnp.exp(c_inc - lw)
    e_neg = jnp.exp(-c_inc)
    e_tail = jnp.exp(c_last - c_inc)
    p_last = jnp.exp(c_last)

    bd_r = lax.broadcasted_iota(jnp.int32, (COL_TILE, COL_TILE), 0) // HEAD_DIM
    bd_c = lax.broadcasted_iota(jnp.int32, (COL_TILE, COL_TILE), 1) // HEAD_DIM
    ones_bd = (bd_r == bd_c).astype(BF16)

    kk_raw = k * per_row(kkp_ref)
    kk = kk_raw * (1.0 / jnp.maximum(jnp.sqrt(_head_sums(kk_raw * kk_raw, ones_bd)), KK_EPS))
    kka = kk * a
    k2 = k * (1.0 + (a - 1.0) * per_row(kap_ref))
    bonus = _head_sums(r * k2 * per_row(rkp_ref), ones_bd) * v
    a_hat_f = -kk * e_exc
    r_hat_f = r * e_inc
    b_chk_f = kka * e_neg
    k_chk_f = k2 * e_neg
    b_til_f = kka * e_tail
    k_til_f = k2 * e_tail

    eye_h = (lax.broadcasted_iota(jnp.int32, (HEAD_DIM, HEAD_DIM), 0)
             == lax.broadcasted_iota(jnp.int32, (HEAD_DIM, HEAD_DIM), 1))
    n_levels = max(1, int(math.log2(chunk)))
    hs = range(rows * heads)
    sl = [slice(h * HEAD_DIM, (h + 1) * HEAD_DIM) for h in hs]
    a_hat = [a_hat_f[:, s] for s in sl]
    r_hat = [r_hat_f[:, s] for s in sl]
    v_h = [v[:, s] for s in sl]

    amat = [_mm_nt(jnp.concatenate([a_hat[h], r_hat[h]], axis=0),
                   jnp.concatenate([b_chk_f[:, sl[h]], k_chk_f[:, sl[h]]], axis=0)) for h in hs]
    a_ab = [jnp.where(strict, m[:chunk, :chunk], 0.0) for m in amat]
    a_ak = [jnp.where(strict, m[:chunk, chunk:], 0.0) for m in amat]
    a_rb = [jnp.where(incl, m[chunk:, :chunk], 0.0) for m in amat]
    a_rk = [jnp.where(incl, m[chunk:, chunk:], 0.0) for m in amat]
    av = [_mm(jnp.concatenate([a_ak[h], a_rk[h]], axis=0), v_h[h]) for h in hs]

    w12 = [jnp.concatenate([a_hat[h], av[h][:chunk]], axis=1) for h in hs]
    power = a_ab
    for level in range(n_levels):
        w12 = [w12[h] + _mm(power[h], w12[h]) for h in hs]
        if level + 1 < n_levels:
            power = [_mm(p, p) for p in power]

    rx = [_mm(a_rb[h], w12[h]) + jnp.concatenate([r_hat[h], av[h][chunk:]], axis=1) for h in hs]
    gh = [_mm_tn(w12[h], b_til_f[:, sl[h]]) for h in hs]
    vk = [_mm_tn(v_h[h], k_til_f[:, sl[h]]) for h in hs]
    g_t = [gh[h][:HEAD_DIM] + jnp.where(eye_h, p_last[:, sl[h]], 0.0) for h in hs]
    h_t = [gh[h][HEAD_DIM:] + vk[h] for h in hs]

    s_prev = [s_scr[h] for h in hs]
    o_h = [_mm_nt(rx[h][:, :HEAD_DIM], s_prev[h]) + rx[h][:, HEAD_DIM:] for h in hs]
    s_new = [_mm(s_prev[h], g_t[h]) + h_t[h] for h in hs]
    for h in hs:
        s_scr[h] = s_new[h]

    o = jnp.concatenate(o_h, axis=1)
    cen = o - _head_sums(o, ones_bd) * (1.0 / HEAD_DIM)
    var = _head_sums(cen * cen, ones_bd) * (1.0 / HEAD_DIM)
    o = cen * lax.rsqrt(var + LNX_EPS) * per_row(lnw_ref) + per_row(lnb_ref)
    o = (o + bonus) * side_by_side(gate_ref)
    for b in range(rows):
        o_ref[b] = o[:, b * mix:(b + 1) * mix]

    @pl.when(c == pl.num_programs(1) - 1)
    def _():
        for b in range(rows):
            sout_ref[b] = s_scr[b * heads:(b + 1) * heads]


def _wkv(r, lw, k, v, a, gate, s0, W, li, batch, seq_len, chunk):
    mix = r.shape[1]
    heads = mix // HEAD_DIM
    if seq_len % chunk:
        raise ValueError("sequence length must be a multiple of the chunk")
    if mix % COL_TILE:
        raise ValueError("mixer width must be a whole number of 4-head tiles")
    n_chunks = seq_len // chunk
    seq = lambda t: t.reshape(batch, seq_len, mix)
    rows = 2 if batch % 2 == 0 else 1
    blk = pl.BlockSpec((rows, chunk, mix), lambda b, c: (b, c, 0))
    st = pl.BlockSpec((rows, heads, HEAD_DIM, HEAD_DIM), lambda b, c: (b, 0, 0, 0))
    par = pl.BlockSpec((1, mix), lambda b, c: (0, 0))
    params = [W['a_k_k'][li], W['a_k_a'][li], W['a_r_k'][li], W['a_lnx_w'][li], W['a_lnx_b'][li]]
    o, s_out = pl.pallas_call(
        functools.partial(_wkv_kernel, chunk=chunk, heads=heads),
        grid=(batch // rows, n_chunks),
        in_specs=[blk] * 6 + [st] + [par] * 5,
        out_specs=[blk, st],
        out_shape=[jax.ShapeDtypeStruct((batch, seq_len, mix), F32),
                   jax.ShapeDtypeStruct(s0.shape, F32)],
        scratch_shapes=[pltpu.VMEM((rows * heads, HEAD_DIM, HEAD_DIM), F32)],
        compiler_params=_cparams("parallel", "arbitrary"),
        name="wkv7",
    )(seq(r), seq(lw), seq(k), seq(v), seq(a), seq(gate), s0, *[p.reshape(1, mix) for p in params])
    return o.reshape(batch * seq_len, mix), s_out


def _mem_attn_kernel(q_ref, kv_ref, o_ref):
    masks = _head_lane_masks(MEM_WIDTH)
    for b in range(q_ref.shape[0]):
        q = q_ref[b]
        tq = q.shape[0]
        mk = kv_ref[b, :, :MEM_WIDTH].astype(BF16)
        mv = kv_ref[b, :, MEM_WIDTH:].astype(BF16)
        stacked = jnp.concatenate([jnp.where(m, q, 0.0) for m in masks], axis=0)
        s = _mm_nt(stacked, mk) * ATTN_SCALE
        p = jnp.exp(s - jnp.max(s, axis=-1, keepdims=True))
        p = p * (1.0 / jnp.sum(p, axis=-1, keepdims=True))
        pv = _mm(p, mv)
        out = jnp.zeros(q.shape, F32)
        for h, m in enumerate(masks):
            out = out + jnp.where(m, pv[h * tq:(h + 1) * tq, :], 0.0)
        o_ref[b] = out


def _mem_attn(q_mem, mem_kv, batch, seq_len, *, tq_want=1024):
    tq = _row_tile(seq_len, tq_want)
    m = mem_kv.shape[1]
    nb = max(n for n in (8, 4, 2, 1) if batch % n == 0 and n * tq <= tq_want)
    out = pl.pallas_call(
        _mem_attn_kernel,
        grid=(batch // nb, seq_len // tq),
        in_specs=[pl.BlockSpec((nb, tq, MEM_WIDTH), lambda b, t: (b, t, 0)),
                  pl.BlockSpec((nb, m, 2 * MEM_WIDTH), lambda b, t: (b, 0, 0))],
        out_specs=pl.BlockSpec((nb, tq, MEM_WIDTH), lambda b, t: (b, t, 0)),
        out_shape=jax.ShapeDtypeStruct((batch, seq_len, MEM_WIDTH), F32),
        compiler_params=_cparams("parallel", "parallel"),
        name="mem_attn",
    )(q_mem.reshape(batch, seq_len, MEM_WIDTH), mem_kv)
    return out.reshape(batch * seq_len, MEM_WIDTH)


def _out_ffn_kernel(*refs, dils, mix_width, d_ff, tf):
    n_groups = len(dils)
    n_scratch = 2 * sum(dil > 1 for dil in dils)
    scratch = list(refs[len(refs) - n_scratch:])
    it = iter(refs[:len(refs) - n_scratch])
    if n_groups:
        read = lambda ref, dil: ref[...] if dil == 1 else _from_class_view(ref, scratch.pop(0), dil)
        og = [read(next(it), dil) for dil in dils]
        lse = [read(next(it), dil) for dil in dils]
        top = functools.reduce(jnp.maximum, lse)
        wgt = [jnp.exp(s - top) for s in lse]
        o_mix = sum(w * o for w, o in zip(wgt, og)) / sum(wgt)
    else:
        o_mix = next(it)[...]
    om_ref, wout_ref, gpm_ref, h_ref, gpre_ref, win_ref, wo_ref, gpost_ref, o_ref, acc_ref = it
    mixed = (jnp.dot(o_mix.astype(BF16), wout_ref[:mix_width, :], preferred_element_type=F32)
             + jnp.dot(om_ref[...].astype(BF16), wout_ref[mix_width:, :], preferred_element_type=F32))
    h1 = h_ref[...] + _rms(mixed, gpm_ref[...])
    xn = _rms(h1, gpre_ref[...]).astype(BF16)
    for j in range(d_ff // tf):
        gate = jnp.dot(xn, win_ref[:, j * tf:(j + 1) * tf], preferred_element_type=F32)
        up = jnp.dot(xn, win_ref[:, d_ff + j * tf:d_ff + (j + 1) * tf], preferred_element_type=F32)
        act = (gate * _sigmoid(gate) * up).astype(BF16)
        part = jnp.dot(act, wo_ref[j * tf:(j + 1) * tf, :], preferred_element_type=F32)
        if j == 0:
            acc_ref[...] = part
        else:
            acc_ref[...] += part
    o_ref[...] = h1 + _rms(acc_ref[...], gpost_ref[...])


def _out_ffn(o_mix, lse, dils, o_mem, h, W, l, w_out_stack, w_out_layer, *, tm_want=512, tf=256):
    n, d = h.shape
    tm = _row_tile(n, tm_want)
    w_in, w_o = W['w_ffn_in_bf16'], W['w_ffn_out_bf16']
    d_ff = w_o.shape[1]
    assert d_ff % tf == 0
    mix_width = w_out_stack.shape[1] - MEM_WIDTH
    row = lambda w: pl.BlockSpec((tm, w), lambda i: (i, 0))
    if lse is not None:
        lead = list(o_mix) + list(lse)
        lead_specs = [pl.BlockSpec((tm // dil, dil * COL_TILE), lambda i: (i, 0)) for dil in dils] * 2
    else:
        lead, lead_specs, dils = [o_mix], [row(o_mix.shape[1])], ()
    vec = lambda g: g.reshape(1, d)
    return pl.pallas_call(
        functools.partial(_out_ffn_kernel, dils=tuple(dils), mix_width=mix_width, d_ff=d_ff, tf=tf),
        grid=(n // tm,),
        in_specs=lead_specs + [
            row(MEM_WIDTH), _layer_resident(w_out_stack, w_out_layer), _resident((1, d)), row(d),
            _resident((1, d)), _layer_resident(w_in, l), _layer_resident(w_o, l), _resident((1, d))],
        out_specs=row(d),
        out_shape=jax.ShapeDtypeStruct((n, d), F32),
        scratch_shapes=[pltpu.VMEM((tm, d), F32)] + [_class_view_scratch(tm) for dil in dils if dil > 1] * 2,
        compiler_params=_cparams("parallel"),
        name="out_ffn",
    )(*lead, o_mem, w_out_stack, vec(W['g_post_mix'][l]), h, vec(W['g_pre_ffn'][l]), w_in, w_o,
      vec(W['g_post_ffn'][l]))


def _dil_prompt_kernel(q_ref, kc_ref, kh_ref, vc_ref, vh_ref, o_ref, lse_ref, kx_ref, vx_ref, *, tile, n_cls):
    t = pl.program_id(2)
    rows = HEADS_PER_TILE * BAND
    qi = lax.broadcasted_iota(jnp.int32, (rows, 2 * BAND), 0) % BAND
    kj = lax.broadcasted_iota(jnp.int32, (rows, 2 * BAND), 1)
    band = (kj >= qi) & (kj <= qi + BAND)
    band_first = band & (kj >= jnp.where(t == 0, BAND, 0))
    masks = _head_lane_masks(COL_TILE)

    for ci in range(n_cls):
        lanes = slice(ci * COL_TILE, (ci + 1) * COL_TILE)
        kx_ref[ci, 0:BAND, :] = kh_ref[0, :, lanes].astype(BF16)
        kx_ref[ci, BAND:, :] = kc_ref[0, :, lanes].astype(BF16)
        vx_ref[ci, 0:BAND, :] = vh_ref[0, :, lanes].astype(BF16)
        vx_ref[ci, BAND:, :] = vc_ref[0, :, lanes].astype(BF16)
        for blk in range(tile // BAND):
            start = blk * BAND
            q = q_ref[0, start:start + BAND, lanes] * ATTN_SCALE
            keys = kx_ref[ci, start:start + 2 * BAND, :]
            vals = vx_ref[ci, start:start + 2 * BAND, :]
            stacked = jnp.concatenate([jnp.where(m, q, 0.0) for m in masks], axis=0)
            s = jnp.where(band_first if blk == 0 else band, _mm_nt(stacked, keys), -jnp.inf)
            top = jnp.max(s, axis=-1, keepdims=True)
            p = jnp.exp(s - top)
            den = jnp.sum(p, axis=-1, keepdims=True)
            pv = _mm(p, vals) * (1.0 / den)
            lse_rows = top + jnp.log(den)
            out = jnp.zeros((BAND, COL_TILE), F32)
            lse = jnp.zeros((BAND, COL_TILE), F32)
            for h, m in enumerate(masks):
                out = out + jnp.where(m, pv[h * BAND:(h + 1) * BAND, :], 0.0)
                lse = lse + jnp.where(m, lse_rows[h * BAND:(h + 1) * BAND, :], 0.0)
            o_ref[0, start:start + BAND, lanes] = out
            lse_ref[0, start:start + BAND, lanes] = lse


DIL_BLOCKS_PER_STEP = 8


def _dil_prompt_group(q, k, v, batch, seq_len, dil):
    cls_len = seq_len // dil
    tile = min(cls_len, DIL_BLOCKS_PER_STEP * BAND)
    assert cls_len % tile == 0 and tile % BAND == 0
    n_t = cls_len // tile
    n_cls = math.gcd(dil, DIL_BLOCKS_PER_STEP * BAND // tile)
    width = n_cls * COL_TILE
    view = lambda x: x.reshape(batch, cls_len, dil * COL_TILE)
    cur = pl.BlockSpec((1, tile, width), lambda b, c, t: (b, t, c))
    halo = pl.BlockSpec((1, BAND, width),
                        lambda b, c, t: (b, jnp.maximum(t * (tile // BAND) - 1, 0), c))
    out_shape = jax.ShapeDtypeStruct((batch, cls_len, dil * COL_TILE), F32)
    o, lse = pl.pallas_call(
        functools.partial(_dil_prompt_kernel, tile=tile, n_cls=n_cls),
        grid=(batch, dil // n_cls, n_t),
        in_specs=[cur, cur, halo, cur, halo],
        out_specs=[cur, cur],
        out_shape=[out_shape, out_shape],
        scratch_shapes=[pltpu.VMEM((n_cls, tile + BAND, COL_TILE), BF16)] * 2,
        compiler_params=_cparams("parallel", "parallel", "arbitrary"),
        name="dil_attn_prompt",
    )(view(q), view(k), view(k), view(v), view(v))
    flat = lambda x: x.reshape(batch * cls_len, dil * COL_TILE)
    return flat(o), flat(lse)


def _dil_sample_kernel(*refs, seq_len, write_cache):
    q_refs = refs[:N_GROUPS]
    k_refs = refs[N_GROUPS:2 * N_GROUPS]
    v_refs = refs[2 * N_GROUPS:3 * N_GROUPS]
    cache_refs = refs[3 * N_GROUPS:4 * N_GROUPS]
    o_ref = refs[4 * N_GROUPS]
    new_refs = refs[4 * N_GROUPS + 1:]
    masks = _head_lane_masks(COL_TILE)
    rows = HEADS_PER_TILE * seq_len

    run_top = jnp.full((rows, 1), -jnp.inf, F32)
    run_den = jnp.zeros((rows, 1), F32)
    run_acc = jnp.zeros((rows, COL_TILE), F32)
    for g, (win, dil) in enumerate(DIL_GROUPS):
        cache = cache_refs[g]
        length = cache.shape[1]
        q = q_refs[g][...] * ATTN_SCALE
        stacked = jnp.concatenate([jnp.where(m, q, 0.0) for m in masks], axis=0)
        k_new, v_new = k_refs[g][...], v_refs[g][...]
        ck = cache[0, :, :COL_TILE]
        cv = cache[0, :, COL_TILE:]
        s_c = _mm_nt(stacked, ck)
        s_n = _mm_nt(stacked, k_new)
        tq_c = lax.broadcasted_iota(jnp.int32, s_c.shape, 0) % seq_len
        dist_c = length + tq_c - lax.broadcasted_iota(jnp.int32, s_c.shape, 1)
        ok_c = (dist_c % dil == 0) & (dist_c <= win)
        tq_n = lax.broadcasted_iota(jnp.int32, s_n.shape, 0) % seq_len
        dist_n = tq_n - lax.broadcasted_iota(jnp.int32, s_n.shape, 1)
        ok_n = (dist_n >= 0) & (dist_n % dil == 0)
        s_c = jnp.where(ok_c, s_c, -jnp.inf)
        s_n = jnp.where(ok_n, s_n, -jnp.inf)
        top = jnp.maximum(jnp.maximum(jnp.max(s_c, axis=-1, keepdims=True),
                                      jnp.max(s_n, axis=-1, keepdims=True)), run_top)
        p_c = jnp.exp(s_c - top)
        p_n = jnp.exp(s_n - top)
        scale = jnp.exp(run_top - top)
        run_den = run_den * scale + jnp.sum(p_c, axis=-1, keepdims=True) + jnp.sum(p_n, axis=-1, keepdims=True)
        run_acc = run_acc * scale + _mm(p_c, cv) + _mm(p_n, v_new)
        run_top = top
        if write_cache:
            new = new_refs[g]
            if length > seq_len:
                new[0, 0:length - seq_len, :] = cache[0, seq_len:, :]
            new[0, length - seq_len:, :COL_TILE] = k_new
            new[0, length - seq_len:, COL_TILE:] = v_new
    res = run_acc / run_den
    out = jnp.zeros((seq_len, COL_TILE), F32)
    for h, m in enumerate(masks):
        out = out + jnp.where(m, res[h * seq_len:(h + 1) * seq_len, :], 0.0)
    o_ref[...] = out


def _dil_sample(q_slabs, kv_slabs, caches, batch, seq_len, write_cache):
    n = batch * seq_len
    in_specs = [pl.BlockSpec((seq_len, COL_TILE), lambda b: (b, 0))] * (3 * N_GROUPS)
    cache_specs = [pl.BlockSpec((1,) + c.shape[1:], lambda b: (b, 0, 0)) for c in caches]
    out_shape = [jax.ShapeDtypeStruct((n, COL_TILE), F32)]
    out_specs = [pl.BlockSpec((seq_len, COL_TILE), lambda b: (b, 0))]
    if write_cache:
        out_shape += [jax.ShapeDtypeStruct(c.shape, F32) for c in caches]
        out_specs += cache_specs
    res = pl.pallas_call(
        functools.partial(_dil_sample_kernel, seq_len=seq_len, write_cache=write_cache),
        grid=(batch,),
        in_specs=in_specs + cache_specs,
        out_specs=out_specs,
        out_shape=out_shape,
        compiler_params=_cparams("parallel"),
        name="dil_attn_sample",
    )(*q_slabs[:N_GROUPS], *kv_slabs, *caches)
    return res[0], list(res[1:])


def _rope_tables(pos):
    half = HEAD_DIM // 2
    inv = ROPE_THETA ** (-jnp.arange(half, dtype=F32) / half)
    ang = pos.astype(F32)[:, None] * inv[None, :]
    cos, sin = jnp.cos(ang), jnp.sin(ang)
    cos = jnp.tile(jnp.concatenate([cos, cos], axis=-1), (1, LANES // HEAD_DIM))
    sin_signed = jnp.tile(jnp.concatenate([-sin, sin], axis=-1), (1, LANES // HEAD_DIM))
    return cos, sin_signed


def _trunk(h, pos, mem_kv_all, wkv0, shift0, caches, W, batch, seq_len, chunk):
    depth = W['g_pre_mix'].shape[0]
    n_a = W['a_mu'].shape[0]
    cos, sin_signed = _rope_tables(pos)
    new_wkv, new_shift, new_caches = [], [], None
    v_first = None
    kv_tiles = None
    for l in range(depth):
        if l < n_a:
            r, lw, k, v, a, gate, q_mem, xn_last = _rwkv_proj(h, shift0[l], seq_len, W, l, v_first)
            if l == 0:
                v_first = v
            o_mix, s_out = _wkv(r, lw, k, v, a, gate, wkv0[l], W, l, batch, seq_len, chunk)
            new_wkv.append(s_out)
            new_shift.append(xn_last)
            lse = None
            w_out, w_out_layer = W['a_w_out_bf16'], l
        else:
            j = l - n_a
            dils = tuple(dil for _, dil in DIL_GROUPS) if caches is None else (1,) * N_GROUPS
            q_proj = (W['g_pre_mix'][l], W['b_w_in_bf16'][j], N_GROUPS, dils + (1,))
            if kv_tiles is None:
                kv_proj = (W['g_kv'], W['w_kv_bf16'], N_GROUPS, dils * 2)
                kv_tiles, proj = _norm_proj(h, [kv_proj, q_proj], cos, sin_signed, seq_len)
            else:
                proj, = _norm_proj(h, [q_proj], cos, sin_signed, seq_len)
            q_mem = proj[N_GROUPS]
            if caches is None:
                outs = [_dil_prompt_group(proj[g], kv_tiles[g], kv_tiles[N_GROUPS + g], batch, seq_len, dil)
                        for g, dil in enumerate(dils)]
                o_mix = [o for o, _ in outs]
                lse = [s for _, s in outs]
            else:
                o_mix, written = _dil_sample(proj, kv_tiles, caches, batch, seq_len, new_caches is None)
                if new_caches is None:
                    new_caches = written
                lse = None
            w_out, w_out_layer = W['b_w_out_bf16'], j
        o_mem = _mem_attn(q_mem, mem_kv_all[l], batch, seq_len)
        h = _out_ffn(o_mix, lse, dils if lse is not None else (), o_mem, h, W, l, w_out, w_out_layer)
    return h, jnp.stack(new_wkv), jnp.stack(new_shift), kv_tiles, new_caches


def _prompt_windows(kv_tiles, batch, seq_len):
    bufs = []
    for g, (win, dil) in enumerate(DIL_GROUPS):
        length = min(win, PAST_LEN)
        assert seq_len >= length and length % dil == 0
        rows = seq_len // dil

        def tail(t):
            t = t.reshape(batch, rows, dil * COL_TILE)[:, rows - length // dil:]
            return t.reshape(batch, length, HEADS_PER_TILE, HEAD_DIM)

        bufs.append(jnp.stack([tail(kv_tiles[g]), tail(kv_tiles[N_GROUPS + g])], axis=2))
    return bufs


def kernel(x_prompt, x_sample, state_wkv, state_shift, cache_win_g0, cache_win_g1, cache_win_g2, cache_mem, mem_prompt, g_pre_mix, g_post_mix, g_pre_ffn, g_post_ffn, g_mem, w_mem_kv, w_ffn_in, w_ffn_out, a_mu, a_w_in, a_w0, a_w1, a_w2, a_a0, a_a1, a_a2, a_v0, a_v1, a_v2, a_g1, a_g2, a_k_k, a_k_a, a_r_k, a_lnx_w, a_lnx_b, a_w_out, g_kv, w_kv, b_w_in, b_w_out):
    W = dict(g_pre_mix=g_pre_mix, g_post_mix=g_post_mix, g_pre_ffn=g_pre_ffn, g_post_ffn=g_post_ffn,
             a_mu=a_mu, a_w0=a_w0, a_a0=a_a0, a_v0=a_v0, a_k_k=a_k_k, a_k_a=a_k_a,
             a_r_k=a_r_k.reshape(a_r_k.shape[0], -1), a_lnx_w=a_lnx_w, a_lnx_b=a_lnx_b, g_kv=g_kv)
    for name, w in dict(w_ffn_in=w_ffn_in, w_ffn_out=w_ffn_out, a_w_in=a_w_in, a_w1=a_w1, a_w2=a_w2,
                        a_a1=a_a1, a_a2=a_a2, a_v1=a_v1, a_v2=a_v2, a_g1=a_g1, a_g2=a_g2,
                        a_w_out=a_w_out, w_kv=w_kv, b_w_in=b_w_in, b_w_out=b_w_out,
                        w_mem_kv=w_mem_kv).items():
        W[name + '_bf16'] = w.astype(BF16)

    bp, tp, d = x_prompt.shape
    bs, ts, _ = x_sample.shape
    depth = g_pre_mix.shape[0]
    n_a = a_mu.shape[0]
    heads = a_w0.shape[1] // HEAD_DIM
    m_tok = mem_prompt.shape[1]

    mem_rows = mem_prompt.reshape(bp * m_tok, d)
    no_tab = jnp.zeros((m_tok, LANES), F32)
    mem_kv_p = _norm_proj(mem_rows, [(g_mem[l], W['w_mem_kv_bf16'][l], 0, (1,)) for l in range(depth)],
                          no_tab, no_tab, m_tok)
    mem_kv_p = jnp.stack([slabs[0].reshape(bp, m_tok, 2 * MEM_WIDTH) for slabs in mem_kv_p])
    wkv_zero = jnp.zeros((n_a, bp, heads, HEAD_DIM, HEAD_DIM), F32)
    shift_zero = jnp.zeros((n_a, bp, d), F32)
    y_p, wkv_p, shift_p, kv_p, _ = _trunk(
        x_prompt.reshape(bp * tp, d), jnp.arange(tp, dtype=jnp.int32), mem_kv_p, wkv_zero, shift_zero,
        None, W, bp, tp, chunk=64)
    bufs_p = _prompt_windows(kv_p, bp, tp)

    caches = [c.reshape(bs, c.shape[1], 2 * COL_TILE) for c in (cache_win_g0, cache_win_g1, cache_win_g2)]
    mem_kv_s = cache_mem.reshape(depth, bs, m_tok, 2 * MEM_WIDTH)
    y_s, wkv_s, shift_s, _, bufs_s = _trunk(
        x_sample.reshape(bs * ts, d), PAST_LEN + jnp.arange(ts, dtype=jnp.int32), mem_kv_s, state_wkv,
        state_shift, caches, W, bs, ts, chunk=ts)
    bufs_s = [b.reshape(c.shape) for b, c in zip(bufs_s, (cache_win_g0, cache_win_g1, cache_win_g2))]

    return (y_p.reshape(bp, tp, d), y_s.reshape(bs, ts, d), wkv_p, shift_p, bufs_p[0], bufs_p[1], bufs_p[2],
            mem_kv_p.reshape(depth, bp, m_tok, 2, MEM_HEADS, HEAD_DIM),
            wkv_s, shift_s, bufs_s[0], bufs_s[1], bufs_s[2])
```

```python
import functools
import math

import jax
import jax.numpy as jnp
from jax import lax
from jax.experimental import pallas as pl
from jax.experimental.pallas import tpu as pltpu

F32 = jnp.float32
BF16 = jnp.bfloat16

HEAD_DIM = 64
LANES = 128
COL_TILE = 256
HEADS_PER_TILE = COL_TILE // HEAD_DIM
MEM_HEADS = 4
MEM_WIDTH = MEM_HEADS * HEAD_DIM
DIL_GROUPS = ((128, 1), (512, 4), (2048, 16))
N_GROUPS = len(DIL_GROUPS)
PAST_LEN = 8192
ROPE_THETA = 10000.0
NORM_EPS = 1e-6
LNX_EPS = HEAD_DIM * 1e-5
KK_EPS = 1e-12
ATTN_SCALE = HEAD_DIM ** -0.5
BAND = 128
VMEM_LIMIT = 56 * 1024 * 1024


def _cparams(*sem):
    return pltpu.CompilerParams(dimension_semantics=sem, vmem_limit_bytes=VMEM_LIMIT)


def _rms(x, g):
    return x * lax.rsqrt(jnp.mean(x * x, axis=-1, keepdims=True) + NORM_EPS) * g


def _mm(a, b):
    return jnp.dot(a.astype(BF16), b.astype(BF16), preferred_element_type=F32)


def _mm_nt(a, b):
    return lax.dot_general(a.astype(BF16), b.astype(BF16), (((1,), (1,)), ((), ())),
                           preferred_element_type=F32)


def _mm_tn(a, b):
    return lax.dot_general(a.astype(BF16), b.astype(BF16), (((0,), (0,)), ((), ())),
                           preferred_element_type=F32)


def _sigmoid(x):
    return 1.0 / (1.0 + jnp.exp(-x))


def _head_lane_masks(width):
    lane = lax.broadcasted_iota(jnp.int32, (1, width), 1)
    return [(lane >= h * HEAD_DIM) & (lane < (h + 1) * HEAD_DIM) for h in range(width // HEAD_DIM)]


def _row_tile(n, want):
    t = min(n, want)
    assert n % t == 0, (n, t)
    return t


def _rotate_heads(y, cos, sin_signed):
    width = y.shape[1]
    cos = jnp.concatenate([cos] * (width // LANES), axis=1)
    sin_signed = jnp.concatenate([sin_signed] * (width // LANES), axis=1)
    lane = lax.broadcasted_iota(jnp.int32, y.shape, 1)
    first_half = (lane % HEAD_DIM) < (HEAD_DIM // 2)
    partner = jnp.where(first_half, pltpu.roll(y, width - HEAD_DIM // 2, axis=1),
                        pltpu.roll(y, HEAD_DIM // 2, axis=1))
    return y * cos + partner * sin_signed


def _to_class_view(y, o_ref, scr, dil):
    rows = y.shape[0] // dil
    for half in range(COL_TILE // LANES):
        scr[half] = y[:, half * LANES:(half + 1) * LANES]
    for cls in range(dil):
        for half in range(COL_TILE // LANES):
            lane0 = cls * COL_TILE + half * LANES
            o_ref[:, lane0:lane0 + LANES] = scr[half, pl.ds(cls, rows, stride=dil), :]


def _from_class_view(ref, scr, dil):
    rows = ref.shape[0]
    for cls in range(dil):
        for half in range(COL_TILE // LANES):
            lane0 = cls * COL_TILE + half * LANES
            scr[half, pl.ds(cls, rows, stride=dil), :] = ref[:, lane0:lane0 + LANES]
    return jnp.concatenate([scr[half] for half in range(COL_TILE // LANES)], axis=1)


def _class_view_scratch(tm):
    return pltpu.VMEM((COL_TILE // LANES, tm, LANES), F32)


def _norm_proj_kernel(x_ref, cos_ref, sin_ref, *refs, plan):
    n_proj = len(plan)
    n_outputs = sum(len(dils) for _, dils in plan)
    out_refs = list(refs[2 * n_proj:2 * n_proj + n_outputs])
    scratch = list(refs[2 * n_proj + n_outputs:])
    x = x_ref[...]
    inv_rms = lax.rsqrt(jnp.mean(x * x, axis=-1, keepdims=True) + NORM_EPS)
    for p, (n_rot, dils) in enumerate(plan):
        g_ref, w_ref = refs[2 * p], refs[2 * p + 1]
        xn = (x * inv_rms * g_ref[...]).astype(BF16)
        width = w_ref.shape[1] // len(dils)
        for j, dil in enumerate(dils):
            o_ref = out_refs.pop(0)
            y = jnp.dot(xn, w_ref[:, j * width:(j + 1) * width], preferred_element_type=F32)
            if j < n_rot:
                y = _rotate_heads(y, cos_ref[...], sin_ref[...])
            if dil == 1:
                o_ref[...] = y
            else:
                _to_class_view(y, o_ref, scratch.pop(0), dil)


def _resident(shape):
    return pl.BlockSpec(shape, lambda *_: (0,) * len(shape), pipeline_mode=pl.Buffered(1))


def _layer_resident(stacked, layer):
    shape = stacked.shape[1:]
    return pl.BlockSpec((None,) + shape, lambda *_: (layer,) + (0,) * len(shape),
                        pipeline_mode=pl.Buffered(1))


def _norm_proj(x, projections, cos, sin_signed, seq_len, *, tm_want=1024):
    n, d = x.shape
    plan = tuple((n_rot, tuple(dils)) for _, _, n_rot, dils in projections)
    widths = [w.shape[1] // len(dils) for _, w, _, dils in projections for _ in dils]
    all_dils = [dil for _, dils in plan for dil in dils]
    assert all(width == COL_TILE or dil == 1 for width, dil in zip(widths, all_dils))
    tm = _row_tile(n, tm_want)
    if tm <= seq_len:
        assert seq_len % tm == 0
        per_seq = seq_len // tm
        tab_map = lambda i: (i % per_seq, 0)
    else:
        assert tm % seq_len == 0
        cos = jnp.tile(cos, (tm // seq_len, 1))
        sin_signed = jnp.tile(sin_signed, (tm // seq_len, 1))
        tab_map = lambda i: (0, 0)
    assert all(tm % (8 * dil) == 0 or dil == 1 for dil in all_dils)
    weights, weight_specs = [], []
    for g, w_bf16, _, _ in projections:
        weights += [g.reshape(1, d), w_bf16]
        weight_specs += [_resident((1, d)), _resident(w_bf16.shape)]
    outs = pl.pallas_call(
        functools.partial(_norm_proj_kernel, plan=plan),
        grid=(n // tm,),
        in_specs=[pl.BlockSpec((tm, d), lambda i: (i, 0)),
                  pl.BlockSpec((tm, LANES), tab_map),
                  pl.BlockSpec((tm, LANES), tab_map)] + weight_specs,
        out_specs=[pl.BlockSpec((tm // dil, dil * width), lambda i: (i, 0))
                   for width, dil in zip(widths, all_dils)],
        out_shape=[jax.ShapeDtypeStruct((n // dil, dil * width), F32) for width, dil in zip(widths, all_dils)],
        scratch_shapes=[_class_view_scratch(tm) for dil in all_dils if dil > 1],
        compiler_params=_cparams("parallel"),
        name="norm_proj",
    )(x, cos, sin_signed, *weights)
    outs = list(outs)
    return [[outs.pop(0) for _ in dils] for _, dils in plan]


def _rwkv_proj_kernel(*refs, mix, has_vfirst, tm, seq_len, tail):
    it = iter(refs)
    h_ref = next(it)
    hp_ref = next(it) if seq_len >= tm else None
    sh_ref = next(it)
    g_ref, mu_ref, win_ref = next(it), next(it), next(it)
    w0_ref, w1_ref, w2_ref = next(it), next(it), next(it)
    a0_ref, a1_ref, a2_ref = next(it), next(it), next(it)
    g1_ref, g2_ref = next(it), next(it)
    if has_vfirst:
        vf_ref, v0_ref, v1_ref, v2_ref = next(it), next(it), next(it), next(it)
    r_ref, lw_ref, k_ref, v_ref, a_ref, gate_ref, qm_ref, tail_ref = (next(it) for _ in range(8))

    i = pl.program_id(0)
    gain = g_ref[...]
    xn = _rms(h_ref[...], gain)
    row = lax.broadcasted_iota(jnp.int32, (tm, 1), 0)
    rolled = pltpu.roll(xn, 1, axis=0)
    if seq_len >= tm:
        prev_tile_last = _rms(hp_ref[...], gain)[7:8, :]
        at_seq_start = (i * tm) % seq_len == 0
        first = jnp.where(at_seq_start, sh_ref[0][7:8, :], prev_tile_last)
        x_prev = jnp.where(row == 0, first, rolled)
    else:
        x_prev = jnp.where(row % seq_len == 0, sh_ref[...], rolled)
    xx = x_prev - xn
    mu = mu_ref[...]

    def mixed(idx):
        return (xn + xx * mu[idx:idx + 1, :]).astype(BF16)

    qm_ref[...] = jnp.dot(xn.astype(BF16), win_ref[:, 3 * mix:], preferred_element_type=F32)
    tail_ref[0] = xn[tm - tail:, :]
    r_ref[...] = jnp.dot(mixed(0), win_ref[:, 0:mix], preferred_element_type=F32)
    w_lin = w0_ref[...] + _mm(jnp.tanh(_mm(mixed(1), w1_ref[...])), w2_ref[...])
    lw_ref[...] = -math.exp(-0.5) * _sigmoid(w_lin)
    k_ref[...] = jnp.dot(mixed(2), win_ref[:, mix:2 * mix], preferred_element_type=F32)
    xv = mixed(3)
    v = jnp.dot(xv, win_ref[:, 2 * mix:3 * mix], preferred_element_type=F32)
    if has_vfirst:
        v = v + (vf_ref[...] - v) * _sigmoid(v0_ref[...] + _mm(_mm(xv, v1_ref[...]), v2_ref[...]))
    v_ref[...] = v
    a_ref[...] = _sigmoid(a0_ref[...] + _mm(_mm(mixed(4), a1_ref[...]), a2_ref[...]))
    gate_ref[...] = _mm(_sigmoid(_mm(mixed(5), g1_ref[...])), g2_ref[...])


def _rwkv_proj(h, shift, seq_len, W, li, v_first, *, tm_want=512):
    n, d = h.shape
    mix = W['a_w0'].shape[1]
    tm = _row_tile(n, tm_want)
    n_tiles = n // tm
    has_vfirst = v_first is not None
    row_spec = lambda w: pl.BlockSpec((tm, w), lambda i: (i, 0))
    full = lambda a: _resident(a.shape)

    args, specs = [h], [row_spec(d)]
    if seq_len >= tm:
        assert seq_len % tm == 0
        tail = 8
        args.append(h)
        specs.append(pl.BlockSpec((8, d), lambda i: (jnp.maximum(i * (tm // 8) - 1, 0), 0)))
        args.append(jnp.broadcast_to(shift[:, None, :], (shift.shape[0], 8, d)))
        specs.append(pl.BlockSpec((1, 8, d), lambda i: ((i * tm) // seq_len, 0, 0)))
    else:
        assert tm % seq_len == 0
        tail = tm
        args.append(jnp.repeat(shift, seq_len, axis=0))
        specs.append(row_spec(d))
    lead = [W['g_pre_mix'][li].reshape(1, d), W['a_mu'][li]]
    small = [W['a_w0'][li].reshape(1, mix), W['a_w1_bf16'][li], W['a_w2_bf16'][li],
             W['a_a0'][li].reshape(1, mix), W['a_a1_bf16'][li], W['a_a2_bf16'][li],
             W['a_g1_bf16'][li], W['a_g2_bf16'][li]]
    args += lead + [W['a_w_in_bf16']] + small
    specs += [full(a) for a in lead] + [_layer_resident(W['a_w_in_bf16'], li)] + [full(a) for a in small]
    if has_vfirst:
        vi = li - 1
        extra = [W['a_v0'][vi].reshape(1, mix), W['a_v1_bf16'][vi], W['a_v2_bf16'][vi]]
        args += [v_first] + extra
        specs += [row_spec(mix)] + [full(a) for a in extra]

    wide = jax.ShapeDtypeStruct((n, mix), F32)
    out_shape = [wide] * 6 + [jax.ShapeDtypeStruct((n, MEM_WIDTH), F32),
                              jax.ShapeDtypeStruct((n_tiles, tail, d), F32)]
    out_specs = [row_spec(mix)] * 6 + [row_spec(MEM_WIDTH),
                                       pl.BlockSpec((1, tail, d), lambda i: (i, 0, 0))]
    r, lw, k, v, a, gate, q_mem, xn_tail = pl.pallas_call(
        functools.partial(_rwkv_proj_kernel, mix=mix, has_vfirst=has_vfirst, tm=tm,
                          seq_len=seq_len, tail=tail),
        grid=(n_tiles,),
        in_specs=specs,
        out_specs=out_specs,
        out_shape=out_shape,
        compiler_params=_cparams("parallel"),
        name="rwkv_proj",
    )(*args)
    xn_rows = xn_tail.reshape(n_tiles * tail, d)
    last = xn_rows.reshape(-1, seq_len if seq_len < tm else tail * (seq_len // tm), d)[:, -1]
    return r, lw, k, v, a, gate, q_mem, last


def _head_sums(x, ones_bd):
    hi = x.astype(BF16)
    lo = (x - hi.astype(F32)).astype(BF16)
    tiles = []
    for t in range(x.shape[1] // COL_TILE):
        s = slice(t * COL_TILE, (t + 1) * COL_TILE)
        tiles.append(jnp.dot(hi[:, s], ones_bd, preferred_element_type=F32)
                     + jnp.dot(lo[:, s], ones_bd, preferred_element_type=F32))
    return jnp.concatenate(tiles, axis=1)


def _wkv_kernel(r_ref, lw_ref, k_ref, v_ref, a_ref, gate_ref, s0_ref, kkp_ref, kap_ref, rkp_ref,
                lnw_ref, lnb_ref, o_ref, sout_ref, s_scr, *, chunk, heads):
    c = pl.program_id(1)
    rows = r_ref.shape[0]
    mix = r_ref.shape[2]

    @pl.when(c == 0)
    def _():
        for b in range(rows):
            s_scr[b * heads:(b + 1) * heads] = s0_ref[b]

    side_by_side = lambda ref: jnp.concatenate([ref[b] for b in range(rows)], axis=1)
    per_row = lambda ref: jnp.concatenate([ref[...]] * rows, axis=1)
    r, lw, k, v, a = (side_by_side(ref) for ref in (r_ref, lw_ref, k_ref, v_ref, a_ref))
    row = lax.broadcasted_iota(jnp.int32, (chunk, chunk), 0)
    col = lax.broadcasted_iota(jnp.int32, (chunk, chunk), 1)
    strict = col < row
    incl = col <= row
    tri = incl.astype(BF16)
    lw_hi = lw.astype(BF16)
    lw_rest = lw - lw_hi.astype(F32)
    lw_mid = lw_rest.astype(BF16)
    lw_lo = (lw_rest - lw_mid.astype(F32)).astype(BF16)
    c_inc = (jnp.dot(tri, lw_hi, preferred_element_type=F32)
             + jnp.dot(tri, lw_mid, preferred_element_type=F32)
             + jnp.dot(tri, lw_lo, preferred_element_type=F32))
    c_last = c_inc[chunk - 1:chunk, :]
    e_inc = jnp.exp(c_inc)
    e_exc = jnp.exp(c_inc - lw)
    e_neg = jnp.exp(-c_inc)
    e_tail = jnp.exp(c_last - c_inc)
    p_last = jnp.exp(c_last)

    bd_r = lax.broadcasted_iota(jnp.int32, (COL_TILE, COL_TILE), 0) // HEAD_DIM
    bd_c = lax.broadcasted_iota(jnp.int32, (COL_TILE, COL_TILE), 1) // HEAD_DIM
    ones_bd = (bd_r == bd_c).astype(BF16)

    kk_raw = k * per_row(kkp_ref)
    kk = kk_raw * (1.0 / jnp.maximum(jnp.sqrt(_head_sums(kk_raw * kk_raw, ones_bd)), KK_EPS))
    kka = kk * a
    k2 = k * (1.0 + (a - 1.0) * per_row(kap_ref))
    bonus = _head_sums(r * k2 * per_row(rkp_ref), ones_bd) * v
    a_hat_f = -kk * e_exc
    r_hat_f = r * e_inc
    b_chk_f = kka * e_neg
    k_chk_f = k2 * e_neg
    b_til_f = kka * e_tail
    k_til_f = k2 * e_tail

    eye_h = (lax.broadcasted_iota(jnp.int32, (HEAD_DIM, HEAD_DIM), 0)
             == lax.broadcasted_iota(jnp.int32, (HEAD_DIM, HEAD_DIM), 1))
    n_levels = max(1, int(math.log2(chunk)))
    hs = range(rows * heads)
    sl = [slice(h * HEAD_DIM, (h + 1) * HEAD_DIM) for h in hs]
    a_hat = [a_hat_f[:, s] for s in sl]
    r_hat = [r_hat_f[:, s] for s in sl]
    v_h = [v[:, s] for s in sl]

    amat = [_mm_nt(jnp.concatenate([a_hat[h], r_hat[h]], axis=0),
                   jnp.concatenate([b_chk_f[:, sl[h]], k_chk_f[:, sl[h]]], axis=0)) for h in hs]
    a_ab = [jnp.where(strict, m[:chunk, :chunk], 0.0) for m in amat]
    a_ak = [jnp.where(strict, m[:chunk, chunk:], 0.0) for m in amat]
    a_rb = [jnp.where(incl, m[chunk:, :chunk], 0.0) for m in amat]
    a_rk = [jnp.where(incl, m[chunk:, chunk:], 0.0) for m in amat]
    av = [_mm(jnp.concatenate([a_ak[h], a_rk[h]], axis=0), v_h[h]) for h in hs]

    w12 = [jnp.concatenate([a_hat[h], av[h][:chunk]], axis=1) for h in hs]
    power = a_ab
    for level in range(n_levels):
        w12 = [w12[h] + _mm(power[h], w12[h]) for h in hs]
        if level + 1 < n_levels:
            power = [_mm(p, p) for p in power]

    rx = [_mm(a_rb[h], w12[h]) + jnp.concatenate([r_hat[h], av[h][chunk:]], axis=1) for h in hs]
    gh = [_mm_tn(w12[h], b_til_f[:, sl[h]]) for h in hs]
    vk = [_mm_tn(v_h[h], k_til_f[:, sl[h]]) for h in hs]
    g_t = [gh[h][:HEAD_DIM] + jnp.where(eye_h, p_last[:, sl[h]], 0.0) for h in hs]
    h_t = [gh[h][HEAD_DIM:] + vk[h] for h in hs]

    s_prev = [s_scr[h] for h in hs]
    o_h = [_mm_nt(rx[h][:, :HEAD_DIM], s_prev[h]) + rx[h][:, HEAD_DIM:] for h in hs]
    s_new = [_mm(s_prev[h], g_t[h]) + h_t[h] for h in hs]
    for h in hs:
        s_scr[h] = s_new[h]

    o = jnp.concatenate(o_h, axis=1)
    cen = o - _head_sums(o, ones_bd) * (1.0 / HEAD_DIM)
    var = _head_sums(cen * cen, ones_bd) * (1.0 / HEAD_DIM)
    o = cen * lax.rsqrt(var + LNX_EPS) * per_row(lnw_ref) + per_row(lnb_ref)
    o = (o + bonus) * side_by_side(gate_ref)
    for b in range(rows):
        o_ref[b] = o[:, b * mix:(b + 1) * mix]

    @pl.when(c == pl.num_programs(1) - 1)
    def _():
        for b in range(rows):
            sout_ref[b] = s_scr[b * heads:(b + 1) * heads]


def _wkv(r, lw, k, v, a, gate, s0, W, li, batch, seq_len, chunk):
    mix = r.shape[1]
    heads = mix // HEAD_DIM
    if seq_len % chunk:
        raise ValueError("sequence length must be a multiple of the chunk")
    if mix % COL_TILE:
        raise ValueError("mixer width must be a whole number of 4-head tiles")
    n_chunks = seq_len // chunk
    seq = lambda t: t.reshape(batch, seq_len, mix)
    rows = 2 if batch % 2 == 0 else 1
    blk = pl.BlockSpec((rows, chunk, mix), lambda b, c: (b, c, 0))
    st = pl.BlockSpec((rows, heads, HEAD_DIM, HEAD_DIM), lambda b, c: (b, 0, 0, 0))
    par = pl.BlockSpec((1, mix), lambda b, c: (0, 0))
    params = [W['a_k_k'][li], W['a_k_a'][li], W['a_r_k'][li], W['a_lnx_w'][li], W['a_lnx_b'][li]]
    o, s_out = pl.pallas_call(
        functools.partial(_wkv_kernel, chunk=chunk, heads=heads),
        grid=(batch // rows, n_chunks),
        in_specs=[blk] * 6 + [st] + [par] * 5,
        out_specs=[blk, st],
        out_shape=[jax.ShapeDtypeStruct((batch, seq_len, mix), F32),
                   jax.ShapeDtypeStruct(s0.shape, F32)],
        scratch_shapes=[pltpu.VMEM((rows * heads, HEAD_DIM, HEAD_DIM), F32)],
        compiler_params=_cparams("parallel", "arbitrary"),
        name="wkv7",
    )(seq(r), seq(lw), seq(k), seq(v), seq(a), seq(gate), s0, *[p.reshape(1, mix) for p in params])
    return o.reshape(batch * seq_len, mix), s_out


def _mem_attn_kernel(q_ref, kv_ref, o_ref):
    masks = _head_lane_masks(MEM_WIDTH)
    for b in range(q_ref.shape[0]):
        q = q_ref[b]
        tq = q.shape[0]
        mk_t = kv_ref[b, :MEM_WIDTH, :].astype(BF16)
        mv_t = kv_ref[b, MEM_WIDTH:, :].astype(BF16)
        stacked = jnp.concatenate([jnp.where(m, q, 0.0) for m in masks], axis=0)
        s = _mm(stacked, mk_t) * ATTN_SCALE
        p = jnp.exp(s - jnp.max(s, axis=-1, keepdims=True))
        p = p * (1.0 / jnp.sum(p, axis=-1, keepdims=True))
        pv = _mm_nt(p, mv_t)
        out = jnp.zeros(q.shape, F32)
        for h, m in enumerate(masks):
            out = out + jnp.where(m, pv[h * tq:(h + 1) * tq, :], 0.0)
        o_ref[b] = out


def _mem_attn(q_mem, mem_kv, batch, seq_len, *, tq_want=1024):
    tq = _row_tile(seq_len, tq_want)
    m = mem_kv.shape[2]
    nb = max(n for n in (8, 4, 2, 1) if batch % n == 0 and n * tq <= tq_want)
    out = pl.pallas_call(
        _mem_attn_kernel,
        grid=(batch // nb, seq_len // tq),
        in_specs=[pl.BlockSpec((nb, tq, MEM_WIDTH), lambda b, t: (b, t, 0)),
                  pl.BlockSpec((nb, 2 * MEM_WIDTH, m), lambda b, t: (b, 0, 0))],
        out_specs=pl.BlockSpec((nb, tq, MEM_WIDTH), lambda b, t: (b, t, 0)),
        out_shape=jax.ShapeDtypeStruct((batch, seq_len, MEM_WIDTH), F32),
        compiler_params=_cparams("parallel", "parallel"),
        name="mem_attn",
    )(q_mem.reshape(batch, seq_len, MEM_WIDTH), mem_kv)
    return out.reshape(batch * seq_len, MEM_WIDTH)


def _out_ffn_kernel(*refs, dils, mix_width, d_ff, tf):
    n_groups = len(dils)
    n_scratch = 2 * sum(dil > 1 for dil in dils)
    scratch = list(refs[len(refs) - n_scratch:])
    it = iter(refs[:len(refs) - n_scratch])
    if n_groups:
        read = lambda ref, dil: ref[...] if dil == 1 else _from_class_view(ref, scratch.pop(0), dil)
        og = [read(next(it), dil) for dil in dils]
        lse = [read(next(it), dil) for dil in dils]
        top = functools.reduce(jnp.maximum, lse)
        wgt = [jnp.exp(s - top) for s in lse]
        o_mix = sum(w * o for w, o in zip(wgt, og)) / sum(wgt)
    else:
        o_mix = next(it)[...]
    om_ref, wout_ref, gpm_ref, h_ref, gpre_ref, win_ref, wo_ref, gpost_ref, o_ref, acc_ref = it
    mixed = (jnp.dot(o_mix.astype(BF16), wout_ref[:mix_width, :], preferred_element_type=F32)
             + jnp.dot(om_ref[...].astype(BF16), wout_ref[mix_width:, :], preferred_element_type=F32))
    h1 = h_ref[...] + _rms(mixed, gpm_ref[...])
    xn = _rms(h1, gpre_ref[...]).astype(BF16)
    for j in range(d_ff // tf):
        gate = jnp.dot(xn, win_ref[:, j * tf:(j + 1) * tf], preferred_element_type=F32)
        up = jnp.dot(xn, win_ref[:, d_ff + j * tf:d_ff + (j + 1) * tf], preferred_element_type=F32)
        act = (gate * _sigmoid(gate) * up).astype(BF16)
        part = jnp.dot(act, wo_ref[j * tf:(j + 1) * tf, :], preferred_element_type=F32)
        if j == 0:
            acc_ref[...] = part
        else:
            acc_ref[...] += part
    o_ref[...] = h1 + _rms(acc_ref[...], gpost_ref[...])


def _out_ffn(o_mix, lse, dils, o_mem, h, W, l, w_out_stack, w_out_layer, *, tm_want=512, tf=256):
    n, d = h.shape
    tm = _row_tile(n, tm_want)
    w_in, w_o = W['w_ffn_in_bf16'], W['w_ffn_out_bf16']
    d_ff = w_o.shape[1]
    assert d_ff % tf == 0
    mix_width = w_out_stack.shape[1] - MEM_WIDTH
    row = lambda w: pl.BlockSpec((tm, w), lambda i: (i, 0))
    if lse is not None:
        lead = list(o_mix) + list(lse)
        lead_specs = [pl.BlockSpec((tm // dil, dil * COL_TILE), lambda i: (i, 0)) for dil in dils] * 2
    else:
        lead, lead_specs, dils = [o_mix], [row(o_mix.shape[1])], ()
    vec = lambda g: g.reshape(1, d)
    return pl.pallas_call(
        functools.partial(_out_ffn_kernel, dils=tuple(dils), mix_width=mix_width, d_ff=d_ff, tf=tf),
        grid=(n // tm,),
        in_specs=lead_specs + [
            row(MEM_WIDTH), _layer_resident(w_out_stack, w_out_layer), _resident((1, d)), row(d),
            _resident((1, d)), _layer_resident(w_in, l), _layer_resident(w_o, l), _resident((1, d))],
        out_specs=row(d),
        out_shape=jax.ShapeDtypeStruct((n, d), F32),
        scratch_shapes=[pltpu.VMEM((tm, d), F32)] + [_class_view_scratch(tm) for dil in dils if dil > 1] * 2,
        compiler_params=_cparams("parallel"),
        name="out_ffn",
    )(*lead, o_mem, w_out_stack, vec(W['g_post_mix'][l]), h, vec(W['g_pre_ffn'][l]), w_in, w_o,
      vec(W['g_post_ffn'][l]))


def _dil_prompt_kernel(q_ref, kc_ref, kh_ref, vc_ref, vh_ref, o_ref, lse_ref, kx_ref, vx_ref, *, tile, n_cls):
    t = pl.program_id(2)
    rows = HEADS_PER_TILE * BAND
    qi = lax.broadcasted_iota(jnp.int32, (rows, 2 * BAND), 0) % BAND
    kj = lax.broadcasted_iota(jnp.int32, (rows, 2 * BAND), 1)
    band = (kj >= qi) & (kj <= qi + BAND)
    band_first = band & (kj >= jnp.where(t == 0, BAND, 0))
    masks = _head_lane_masks(COL_TILE)

    for ci in range(n_cls):
        lanes = slice(ci * COL_TILE, (ci + 1) * COL_TILE)
        kx_ref[ci, 0:BAND, :] = kh_ref[0, :, lanes].astype(BF16)
        kx_ref[ci, BAND:, :] = kc_ref[0, :, lanes].astype(BF16)
        vx_ref[ci, 0:BAND, :] = vh_ref[0, :, lanes].astype(BF16)
        vx_ref[ci, BAND:, :] = vc_ref[0, :, lanes].astype(BF16)
        for blk in range(tile // BAND):
            start = blk * BAND
            q = q_ref[0, start:start + BAND, lanes] * ATTN_SCALE
            keys = kx_ref[ci, start:start + 2 * BAND, :]
            vals = vx_ref[ci, start:start + 2 * BAND, :]
            stacked = jnp.concatenate([jnp.where(m, q, 0.0) for m in masks], axis=0)
            s = jnp.where(band_first if blk == 0 else band, _mm_nt(stacked, keys), -jnp.inf)
            top = jnp.max(s, axis=-1, keepdims=True)
            p = jnp.exp(s - top)
            den = jnp.sum(p, axis=-1, keepdims=True)
            pv = _mm(p, vals) * (1.0 / den)
            lse_rows = top + jnp.log(den)
            out = jnp.zeros((BAND, COL_TILE), F32)
            lse = jnp.zeros((BAND, COL_TILE), F32)
            for h, m in enumerate(masks):
                out = out + jnp.where(m, pv[h * BAND:(h + 1) * BAND, :], 0.0)
                lse = lse + jnp.where(m, lse_rows[h * BAND:(h + 1) * BAND, :], 0.0)
            o_ref[0, start:start + BAND, lanes] = out
            lse_ref[0, start:start + BAND, lanes] = lse


DIL_BLOCKS_PER_STEP = 8


def _dil_prompt_group(q, k, v, batch, seq_len, dil):
    cls_len = seq_len // dil
    tile = min(cls_len, DIL_BLOCKS_PER_STEP * BAND)
    assert cls_len % tile == 0 and tile % BAND == 0
    n_t = cls_len // tile
    n_cls = math.gcd(dil, DIL_BLOCKS_PER_STEP * BAND // tile)
    width = n_cls * COL_TILE
    view = lambda x: x.reshape(batch, cls_len, dil * COL_TILE)
    cur = pl.BlockSpec((1, tile, width), lambda b, c, t: (b, t, c))
    halo = pl.BlockSpec((1, BAND, width),
                        lambda b, c, t: (b, jnp.maximum(t * (tile // BAND) - 1, 0), c))
    out_shape = jax.ShapeDtypeStruct((batch, cls_len, dil * COL_TILE), F32)
    o, lse = pl.pallas_call(
        functools.partial(_dil_prompt_kernel, tile=tile, n_cls=n_cls),
        grid=(batch, dil // n_cls, n_t),
        in_specs=[cur, cur, halo, cur, halo],
        out_specs=[cur, cur],
        out_shape=[out_shape, out_shape],
        scratch_shapes=[pltpu.VMEM((n_cls, tile + BAND, COL_TILE), BF16)] * 2,
        compiler_params=_cparams("parallel", "parallel", "arbitrary"),
        name="dil_attn_prompt",
    )(view(q), view(k), view(k), view(v), view(v))
    flat = lambda x: x.reshape(batch * cls_len, dil * COL_TILE)
    return flat(o), flat(lse)


def _dil_sample_kernel(*refs, seq_len, write_cache):
    q_refs = refs[:N_GROUPS]
    k_refs = refs[N_GROUPS:2 * N_GROUPS]
    v_refs = refs[2 * N_GROUPS:3 * N_GROUPS]
    cache_refs = refs[3 * N_GROUPS:4 * N_GROUPS]
    n_in = (6 if write_cache else 4) * N_GROUPS
    kt_refs, vt_refs = refs[4 * N_GROUPS:5 * N_GROUPS], refs[5 * N_GROUPS:n_in]
    o_ref = refs[n_in]
    new_refs = refs[n_in + 1:]
    masks = _head_lane_masks(COL_TILE)
    rows = HEADS_PER_TILE * seq_len

    run_top = jnp.full((rows, 1), -jnp.inf, F32)
    run_den = jnp.zeros((rows, 1), F32)
    run_acc = jnp.zeros((rows, COL_TILE), F32)
    for g, (win, dil) in enumerate(DIL_GROUPS):
        cache = cache_refs[g]
        length = cache.shape[2]
        q = q_refs[g][...] * ATTN_SCALE
        stacked = jnp.concatenate([jnp.where(m, q, 0.0) for m in masks], axis=0)
        k_new, v_new = k_refs[g][...], v_refs[g][...]
        ck_t = cache[0, :COL_TILE, :]
        cv_t = cache[0, COL_TILE:, :]
        s_c = _mm(stacked, ck_t)
        s_n = _mm_nt(stacked, k_new)
        tq_c = lax.broadcasted_iota(jnp.int32, s_c.shape, 0) % seq_len
        dist_c = length + tq_c - lax.broadcasted_iota(jnp.int32, s_c.shape, 1)
        ok_c = (dist_c % dil == 0) & (dist_c <= win)
        tq_n = lax.broadcasted_iota(jnp.int32, s_n.shape, 0) % seq_len
        dist_n = tq_n - lax.broadcasted_iota(jnp.int32, s_n.shape, 1)
        ok_n = (dist_n >= 0) & (dist_n % dil == 0)
        s_c = jnp.where(ok_c, s_c, -jnp.inf)
        s_n = jnp.where(ok_n, s_n, -jnp.inf)
        top = jnp.maximum(jnp.maximum(jnp.max(s_c, axis=-1, keepdims=True),
                                      jnp.max(s_n, axis=-1, keepdims=True)), run_top)
        p_c = jnp.exp(s_c - top)
        p_n = jnp.exp(s_n - top)
        scale = jnp.exp(run_top - top)
        run_den = run_den * scale + jnp.sum(p_c, axis=-1, keepdims=True) + jnp.sum(p_n, axis=-1, keepdims=True)
        run_acc = run_acc * scale + _mm_nt(p_c, cv_t) + _mm(p_n, v_new)
        run_top = top
        if write_cache:
            new = new_refs[g]
            if length > seq_len:
                new[0] = pltpu.roll(cache[0], length - seq_len, axis=1)
            new[0, :COL_TILE, length - seq_len:] = kt_refs[g][0]
            new[0, COL_TILE:, length - seq_len:] = vt_refs[g][0]
    res = run_acc / run_den
    out = jnp.zeros((seq_len, COL_TILE), F32)
    for h, m in enumerate(masks):
        out = out + jnp.where(m, res[h * seq_len:(h + 1) * seq_len, :], 0.0)
    o_ref[...] = out


def _dil_sample(q_slabs, kv_slabs, caches_t, batch, seq_len, write_cache):
    n = batch * seq_len
    in_specs = [pl.BlockSpec((seq_len, COL_TILE), lambda b: (b, 0))] * (3 * N_GROUPS)
    cache_specs = [pl.BlockSpec((1,) + c.shape[1:], lambda b: (b, 0, 0)) for c in caches_t]
    out_shape = [jax.ShapeDtypeStruct((n, COL_TILE), F32)]
    out_specs = [pl.BlockSpec((seq_len, COL_TILE), lambda b: (b, 0))]
    new_rows_t, new_specs = [], []
    if write_cache:
        out_shape += [jax.ShapeDtypeStruct(c.shape, F32) for c in caches_t]
        out_specs += cache_specs
        new_rows_t = [jnp.swapaxes(s.reshape(batch, seq_len, COL_TILE), 1, 2) for s in kv_slabs]
        new_specs = [pl.BlockSpec((1, COL_TILE, seq_len), lambda b: (b, 0, 0))] * len(kv_slabs)
    res = pl.pallas_call(
        functools.partial(_dil_sample_kernel, seq_len=seq_len, write_cache=write_cache),
        grid=(batch,),
        in_specs=in_specs + cache_specs + new_specs,
        out_specs=out_specs,
        out_shape=out_shape,
        compiler_params=_cparams("parallel"),
        name="dil_attn_sample",
    )(*q_slabs[:N_GROUPS], *kv_slabs, *caches_t, *new_rows_t)
    return res[0], list(res[1:])


def _rope_tables(pos):
    half = HEAD_DIM // 2
    inv = ROPE_THETA ** (-jnp.arange(half, dtype=F32) / half)
    ang = pos.astype(F32)[:, None] * inv[None, :]
    cos, sin = jnp.cos(ang), jnp.sin(ang)
    cos = jnp.tile(jnp.concatenate([cos, cos], axis=-1), (1, LANES // HEAD_DIM))
    sin_signed = jnp.tile(jnp.concatenate([-sin, sin], axis=-1), (1, LANES // HEAD_DIM))
    return cos, sin_signed


def _trunk(h, pos, mem_kv_all, wkv0, shift0, caches, W, batch, seq_len, chunk):
    depth = W['g_pre_mix'].shape[0]
    n_a = W['a_mu'].shape[0]
    cos, sin_signed = _rope_tables(pos)
    new_wkv, new_shift, new_caches = [], [], None
    v_first = None
    kv_tiles = None
    for l in range(depth):
        if l < n_a:
            r, lw, k, v, a, gate, q_mem, xn_last = _rwkv_proj(h, shift0[l], seq_len, W, l, v_first)
            if l == 0:
                v_first = v
            o_mix, s_out = _wkv(r, lw, k, v, a, gate, wkv0[l], W, l, batch, seq_len, chunk)
            new_wkv.append(s_out)
            new_shift.append(xn_last)
            lse = None
            w_out, w_out_layer = W['a_w_out_bf16'], l
        else:
            j = l - n_a
            dils = tuple(dil for _, dil in DIL_GROUPS) if caches is None else (1,) * N_GROUPS
            q_proj = (W['g_pre_mix'][l], W['b_w_in_bf16'][j], N_GROUPS, dils + (1,))
            if kv_tiles is None:
                kv_proj = (W['g_kv'], W['w_kv_bf16'], N_GROUPS, dils * 2)
                kv_tiles, proj = _norm_proj(h, [kv_proj, q_proj], cos, sin_signed, seq_len)
            else:
                proj, = _norm_proj(h, [q_proj], cos, sin_signed, seq_len)
            q_mem = proj[N_GROUPS]
            if caches is None:
                outs = [_dil_prompt_group(proj[g], kv_tiles[g], kv_tiles[N_GROUPS + g], batch, seq_len, dil)
                        for g, dil in enumerate(dils)]
                o_mix = [o for o, _ in outs]
                lse = [s for _, s in outs]
            else:
                o_mix, written = _dil_sample(proj, kv_tiles, caches, batch, seq_len, new_caches is None)
                if new_caches is None:
                    new_caches = written
                lse = None
            w_out, w_out_layer = W['b_w_out_bf16'], j
        o_mem = _mem_attn(q_mem, mem_kv_all[l], batch, seq_len)
        h = _out_ffn(o_mix, lse, dils if lse is not None else (), o_mem, h, W, l, w_out, w_out_layer)
    return h, jnp.stack(new_wkv), jnp.stack(new_shift), kv_tiles, new_caches


def _prompt_windows(kv_tiles, batch, seq_len):
    bufs = []
    for g, (win, dil) in enumerate(DIL_GROUPS):
        length = min(win, PAST_LEN)
        assert seq_len >= length and length % dil == 0
        rows = seq_len // dil

        def tail(t):
            t = t.reshape(batch, rows, dil * COL_TILE)[:, rows - length // dil:]
            return t.reshape(batch, length, HEADS_PER_TILE, HEAD_DIM)

        bufs.append(jnp.stack([tail(kv_tiles[g]), tail(kv_tiles[N_GROUPS + g])], axis=2))
    return bufs


def kernel(x_prompt, x_sample, state_wkv, state_shift, cache_win_g0, cache_win_g1, cache_win_g2, cache_mem, mem_prompt, g_pre_mix, g_post_mix, g_pre_ffn, g_post_ffn, g_mem, w_mem_kv, w_ffn_in, w_ffn_out, a_mu, a_w_in, a_w0, a_w1, a_w2, a_a0, a_a1, a_a2, a_v0, a_v1, a_v2, a_g1, a_g2, a_k_k, a_k_a, a_r_k, a_lnx_w, a_lnx_b, a_w_out, g_kv, w_kv, b_w_in, b_w_out):
    W = dict(g_pre_mix=g_pre_mix, g_post_mix=g_post_mix, g_pre_ffn=g_pre_ffn, g_post_ffn=g_post_ffn,
             a_mu=a_mu, a_w0=a_w0, a_a0=a_a0, a_v0=a_v0, a_k_k=a_k_k, a_k_a=a_k_a,
             a_r_k=a_r_k.reshape(a_r_k.shape[0], -1), a_lnx_w=a_lnx_w, a_lnx_b=a_lnx_b, g_kv=g_kv)
    for name, w in dict(w_ffn_in=w_ffn_in, w_ffn_out=w_ffn_out, a_w_in=a_w_in, a_w1=a_w1, a_w2=a_w2,
                        a_a1=a_a1, a_a2=a_a2, a_v1=a_v1, a_v2=a_v2, a_g1=a_g1, a_g2=a_g2,
                        a_w_out=a_w_out, w_kv=w_kv, b_w_in=b_w_in, b_w_out=b_w_out,
                        w_mem_kv=w_mem_kv).items():
        W[name + '_bf16'] = w.astype(BF16)

    bp, tp, d = x_prompt.shape
    bs, ts, _ = x_sample.shape
    depth = g_pre_mix.shape[0]
    n_a = a_mu.shape[0]
    heads = a_w0.shape[1] // HEAD_DIM
    m_tok = mem_prompt.shape[1]

    mem_rows = mem_prompt.reshape(bp * m_tok, d)
    no_tab = jnp.zeros((m_tok, LANES), F32)
    mem_kv_p = _norm_proj(mem_rows, [(g_mem[l], W['w_mem_kv_bf16'][l], 0, (1,)) for l in range(depth)],
                          no_tab, no_tab, m_tok)
    mem_kv_p = jnp.stack([slabs[0].reshape(bp, m_tok, 2 * MEM_WIDTH) for slabs in mem_kv_p])
    mem_kv_p_t = jnp.swapaxes(mem_kv_p, 2, 3)
    wkv_zero = jnp.zeros((n_a, bp, heads, HEAD_DIM, HEAD_DIM), F32)
    shift_zero = jnp.zeros((n_a, bp, d), F32)
    y_p, wkv_p, shift_p, kv_p, _ = _trunk(
        x_prompt.reshape(bp * tp, d), jnp.arange(tp, dtype=jnp.int32), mem_kv_p_t, wkv_zero, shift_zero,
        None, W, bp, tp, chunk=64)
    bufs_p = _prompt_windows(kv_p, bp, tp)

    windows = (cache_win_g0, cache_win_g1, cache_win_g2)
    caches_t = [jnp.transpose(c, (0, 2, 3, 4, 1)).reshape(bs, 2 * COL_TILE, c.shape[1]) for c in windows]
    mem_kv_s_t = jnp.transpose(cache_mem, (0, 1, 3, 4, 5, 2)).reshape(depth, bs, 2 * MEM_WIDTH, m_tok)
    y_s, wkv_s, shift_s, _, bufs_s_t = _trunk(
        x_sample.reshape(bs * ts, d), PAST_LEN + jnp.arange(ts, dtype=jnp.int32), mem_kv_s_t, state_wkv,
        state_shift, caches_t, W, bs, ts, chunk=ts)
    bufs_s = [jnp.transpose(b.reshape(bs, 2, HEADS_PER_TILE, HEAD_DIM, c.shape[1]), (0, 4, 1, 2, 3))
              for b, c in zip(bufs_s_t, windows)]

    return (y_p.reshape(bp, tp, d), y_s.reshape(bs, ts, d), wkv_p, shift_p, bufs_p[0], bufs_p[1], bufs_p[2],
            jnp.transpose(mem_kv_p_t.reshape(depth, bp, 2, MEM_HEADS, HEAD_DIM, m_tok), (0, 1, 5, 2, 3, 4)),
            wkv_s, shift_s, bufs_s[0], bufs_s[1], bufs_s[2])
```

```python
import functools
import math

import jax
import jax.numpy as jnp
from jax import lax
from jax.experimental import pallas as pl
from jax.experimental.pallas import tpu as pltpu

F32 = jnp.float32
BF16 = jnp.bfloat16

HEAD_DIM = 64
LANES = 128
COL_TILE = 256
HEADS_PER_TILE = COL_TILE // HEAD_DIM
MEM_HEADS = 4
MEM_WIDTH = MEM_HEADS * HEAD_DIM
DIL_GROUPS = ((128, 1), (512, 4), (2048, 16))
N_GROUPS = len(DIL_GROUPS)
PAST_LEN = 8192
ROPE_THETA = 10000.0
NORM_EPS = 1e-6
LNX_EPS = HEAD_DIM * 1e-5
KK_EPS = 1e-12
ATTN_SCALE = HEAD_DIM ** -0.5
BAND = 128
VMEM_LIMIT = 56 * 1024 * 1024


def _cparams(*sem):
    return pltpu.CompilerParams(dimension_semantics=sem, vmem_limit_bytes=VMEM_LIMIT)


def _rms(x, g):
    return x * lax.rsqrt(jnp.mean(x * x, axis=-1, keepdims=True) + NORM_EPS) * g


def _mm(a, b):
    return jnp.dot(a.astype(BF16), b.astype(BF16), preferred_element_type=F32)


def _mm_nt(a, b):
    return lax.dot_general(a.astype(BF16), b.astype(BF16), (((1,), (1,)), ((), ())),
                           preferred_element_type=F32)


def _mm_tn(a, b):
    return lax.dot_general(a.astype(BF16), b.astype(BF16), (((0,), (0,)), ((), ())),
                           preferred_element_type=F32)


def _sigmoid(x):
    return 1.0 / (1.0 + jnp.exp(-x))


def _head_lane_masks(width):
    lane = lax.broadcasted_iota(jnp.int32, (1, width), 1)
    return [(lane >= h * HEAD_DIM) & (lane < (h + 1) * HEAD_DIM) for h in range(width // HEAD_DIM)]


def _row_tile(n, want):
    t = min(n, want)
    assert n % t == 0, (n, t)
    return t


def _rotate_heads(y, cos, sin_signed):
    width = y.shape[1]
    cos = jnp.concatenate([cos] * (width // LANES), axis=1)
    sin_signed = jnp.concatenate([sin_signed] * (width // LANES), axis=1)
    lane = lax.broadcasted_iota(jnp.int32, y.shape, 1)
    first_half = (lane % HEAD_DIM) < (HEAD_DIM // 2)
    partner = jnp.where(first_half, pltpu.roll(y, width - HEAD_DIM // 2, axis=1),
                        pltpu.roll(y, HEAD_DIM // 2, axis=1))
    return y * cos + partner * sin_signed


def _to_class_view(y, o_ref, scr, dil):
    rows = y.shape[0] // dil
    for half in range(COL_TILE // LANES):
        scr[half] = y[:, half * LANES:(half + 1) * LANES]
    for cls in range(dil):
        for half in range(COL_TILE // LANES):
            lane0 = cls * COL_TILE + half * LANES
            o_ref[:, lane0:lane0 + LANES] = scr[half, pl.ds(cls, rows, stride=dil), :]


def _from_class_view(ref, scr, dil):
    rows = ref.shape[0]
    for cls in range(dil):
        for half in range(COL_TILE // LANES):
            lane0 = cls * COL_TILE + half * LANES
            scr[half, pl.ds(cls, rows, stride=dil), :] = ref[:, lane0:lane0 + LANES]
    return jnp.concatenate([scr[half] for half in range(COL_TILE // LANES)], axis=1)


def _class_view_scratch(tm):
    return pltpu.VMEM((COL_TILE // LANES, tm, LANES), F32)


def _norm_proj_kernel(x_ref, cos_ref, sin_ref, *refs, plan):
    n_proj = len(plan)
    n_outputs = sum(len(dils) for _, dils in plan)
    out_refs = list(refs[2 * n_proj:2 * n_proj + n_outputs])
    scratch = list(refs[2 * n_proj + n_outputs:])
    x = x_ref[...]
    inv_rms = lax.rsqrt(jnp.mean(x * x, axis=-1, keepdims=True) + NORM_EPS)
    for p, (n_rot, dils) in enumerate(plan):
        g_ref, w_ref = refs[2 * p], refs[2 * p + 1]
        xn = (x * inv_rms * g_ref[...]).astype(BF16)
        width = w_ref.shape[1] // len(dils)
        for j, dil in enumerate(dils):
            o_ref = out_refs.pop(0)
            y = jnp.dot(xn, w_ref[:, j * width:(j + 1) * width], preferred_element_type=F32)
            if j < n_rot:
                y = _rotate_heads(y, cos_ref[...], sin_ref[...])
            if dil == 1:
                o_ref[...] = y
            else:
                _to_class_view(y, o_ref, scratch.pop(0), dil)


def _resident(shape):
    return pl.BlockSpec(shape, lambda *_: (0,) * len(shape), pipeline_mode=pl.Buffered(1))


def _layer_resident(stacked, layer):
    shape = stacked.shape[1:]
    return pl.BlockSpec((None,) + shape, lambda *_: (layer,) + (0,) * len(shape),
                        pipeline_mode=pl.Buffered(1))


def _norm_proj(x, projections, cos, sin_signed, seq_len, *, tm_want=1024):
    n, d = x.shape
    plan = tuple((n_rot, tuple(dils)) for _, _, n_rot, dils in projections)
    widths = [w.shape[1] // len(dils) for _, w, _, dils in projections for _ in dils]
    all_dils = [dil for _, dils in plan for dil in dils]
    assert all(width == COL_TILE or dil == 1 for width, dil in zip(widths, all_dils))
    tm = _row_tile(n, tm_want)
    if tm <= seq_len:
        assert seq_len % tm == 0
        per_seq = seq_len // tm
        tab_map = lambda i: (i % per_seq, 0)
    else:
        assert tm % seq_len == 0
        cos = jnp.tile(cos, (tm // seq_len, 1))
        sin_signed = jnp.tile(sin_signed, (tm // seq_len, 1))
        tab_map = lambda i: (0, 0)
    assert all(tm % (8 * dil) == 0 or dil == 1 for dil in all_dils)
    weights, weight_specs = [], []
    for g, w_bf16, _, _ in projections:
        weights += [g.reshape(1, d), w_bf16]
        weight_specs += [_resident((1, d)), _resident(w_bf16.shape)]
    outs = pl.pallas_call(
        functools.partial(_norm_proj_kernel, plan=plan),
        grid=(n // tm,),
        in_specs=[pl.BlockSpec((tm, d), lambda i: (i, 0)),
                  pl.BlockSpec((tm, LANES), tab_map),
                  pl.BlockSpec((tm, LANES), tab_map)] + weight_specs,
        out_specs=[pl.BlockSpec((tm // dil, dil * width), lambda i: (i, 0))
                   for width, dil in zip(widths, all_dils)],
        out_shape=[jax.ShapeDtypeStruct((n // dil, dil * width), F32) for width, dil in zip(widths, all_dils)],
        scratch_shapes=[_class_view_scratch(tm) for dil in all_dils if dil > 1],
        compiler_params=_cparams("parallel"),
        name="norm_proj",
    )(x, cos, sin_signed, *weights)
    outs = list(outs)
    return [[outs.pop(0) for _ in dils] for _, dils in plan]


def _rwkv_proj_kernel(*refs, mix, has_vfirst, tm, seq_len, tail):
    it = iter(refs)
    h_ref = next(it)
    hp_ref = next(it) if seq_len >= tm else None
    sh_ref = next(it)
    g_ref, mu_ref, win_ref = next(it), next(it), next(it)
    w0_ref, w1_ref, w2_ref = next(it), next(it), next(it)
    a0_ref, a1_ref, a2_ref = next(it), next(it), next(it)
    g1_ref, g2_ref = next(it), next(it)
    if has_vfirst:
        vf_ref, v0_ref, v1_ref, v2_ref = next(it), next(it), next(it), next(it)
    r_ref, lw_ref, k_ref, v_ref, a_ref, gate_ref, qm_ref, tail_ref = (next(it) for _ in range(8))

    i = pl.program_id(0)
    gain = g_ref[...]
    xn = _rms(h_ref[...], gain)
    row = lax.broadcasted_iota(jnp.int32, (tm, 1), 0)
    rolled = pltpu.roll(xn, 1, axis=0)
    if seq_len >= tm:
        prev_tile_last = _rms(hp_ref[...], gain)[7:8, :]
        at_seq_start = (i * tm) % seq_len == 0
        first = jnp.where(at_seq_start, sh_ref[0][7:8, :], prev_tile_last)
        x_prev = jnp.where(row == 0, first, rolled)
    else:
        x_prev = jnp.where(row % seq_len == 0, sh_ref[...], rolled)
    xx = x_prev - xn
    mu = mu_ref[...]

    def mixed(idx):
        return (xn + xx * mu[idx:idx + 1, :]).astype(BF16)

    qm_ref[...] = jnp.dot(xn.astype(BF16), win_ref[:, 3 * mix:], preferred_element_type=F32)
    tail_ref[0] = xn[tm - tail:, :]
    r_ref[...] = jnp.dot(mixed(0), win_ref[:, 0:mix], preferred_element_type=F32)
    w_lin = w0_ref[...] + _mm(jnp.tanh(_mm(mixed(1), w1_ref[...])), w2_ref[...])
    lw_ref[...] = -math.exp(-0.5) * _sigmoid(w_lin)
    k_ref[...] = jnp.dot(mixed(2), win_ref[:, mix:2 * mix], preferred_element_type=F32)
    xv = mixed(3)
    v = jnp.dot(xv, win_ref[:, 2 * mix:3 * mix], preferred_element_type=F32)
    if has_vfirst:
        v = v + (vf_ref[...] - v) * _sigmoid(v0_ref[...] + _mm(_mm(xv, v1_ref[...]), v2_ref[...]))
    v_ref[...] = v
    a_ref[...] = _sigmoid(a0_ref[...] + _mm(_mm(mixed(4), a1_ref[...]), a2_ref[...]))
    gate_ref[...] = _mm(_sigmoid(_mm(mixed(5), g1_ref[...])), g2_ref[...])


def _rwkv_proj(h, shift, seq_len, W, li, v_first, *, tm_want=512):
    n, d = h.shape
    mix = W['a_w0'].shape[1]
    tm = _row_tile(n, tm_want)
    n_tiles = n // tm
    has_vfirst = v_first is not None
    row_spec = lambda w: pl.BlockSpec((tm, w), lambda i: (i, 0))
    full = lambda a: _resident(a.shape)

    args, specs = [h], [row_spec(d)]
    if seq_len >= tm:
        assert seq_len % tm == 0
        tail = 8
        args.append(h)
        specs.append(pl.BlockSpec((8, d), lambda i: (jnp.maximum(i * (tm // 8) - 1, 0), 0)))
        args.append(jnp.broadcast_to(shift[:, None, :], (shift.shape[0], 8, d)))
        specs.append(pl.BlockSpec((1, 8, d), lambda i: ((i * tm) // seq_len, 0, 0)))
    else:
        assert tm % seq_len == 0
        tail = tm
        args.append(jnp.repeat(shift, seq_len, axis=0))
        specs.append(row_spec(d))
    lead = [W['g_pre_mix'][li].reshape(1, d), W['a_mu'][li]]
    small = [W['a_w0'][li].reshape(1, mix), W['a_w1_bf16'][li], W['a_w2_bf16'][li],
             W['a_a0'][li].reshape(1, mix), W['a_a1_bf16'][li], W['a_a2_bf16'][li],
             W['a_g1_bf16'][li], W['a_g2_bf16'][li]]
    args += lead + [W['a_w_in_bf16']] + small
    specs += [full(a) for a in lead] + [_layer_resident(W['a_w_in_bf16'], li)] + [full(a) for a in small]
    if has_vfirst:
        vi = li - 1
        extra = [W['a_v0'][vi].reshape(1, mix), W['a_v1_bf16'][vi], W['a_v2_bf16'][vi]]
        args += [v_first] + extra
        specs += [row_spec(mix)] + [full(a) for a in extra]

    wide = jax.ShapeDtypeStruct((n, mix), F32)
    out_shape = [wide] * 6 + [jax.ShapeDtypeStruct((n, MEM_WIDTH), F32),
                              jax.ShapeDtypeStruct((n_tiles, tail, d), F32)]
    out_specs = [row_spec(mix)] * 6 + [row_spec(MEM_WIDTH),
                                       pl.BlockSpec((1, tail, d), lambda i: (i, 0, 0))]
    r, lw, k, v, a, gate, q_mem, xn_tail = pl.pallas_call(
        functools.partial(_rwkv_proj_kernel, mix=mix, has_vfirst=has_vfirst, tm=tm,
                          seq_len=seq_len, tail=tail),
        grid=(n_tiles,),
        in_specs=specs,
        out_specs=out_specs,
        out_shape=out_shape,
        compiler_params=_cparams("parallel"),
        name="rwkv_proj",
    )(*args)
    xn_rows = xn_tail.reshape(n_tiles * tail, d)
    last = xn_rows.reshape(-1, seq_len if seq_len < tm else tail * (seq_len // tm), d)[:, -1]
    return r, lw, k, v, a, gate, q_mem, last


def _head_sums(x, ones_bd):
    hi = x.astype(BF16)
    lo = (x - hi.astype(F32)).astype(BF16)
    tiles = []
    for t in range(x.shape[1] // COL_TILE):
        s = slice(t * COL_TILE, (t + 1) * COL_TILE)
        tiles.append(jnp.dot(hi[:, s], ones_bd, preferred_element_type=F32)
                     + jnp.dot(lo[:, s], ones_bd, preferred_element_type=F32))
    return jnp.concatenate(tiles, axis=1)


def _wkv_kernel(r_ref, lw_ref, k_ref, v_ref, a_ref, gate_ref, s0_ref, kkp_ref, kap_ref, rkp_ref,
                lnw_ref, lnb_ref, o_ref, sout_ref, s_scr, *, chunk, heads):
    c = pl.program_id(1)
    rows = r_ref.shape[0]
    mix = r_ref.shape[2]

    @pl.when(c == 0)
    def _():
        for b in range(rows):
            s_scr[b * heads:(b + 1) * heads] = s0_ref[b]

    side_by_side = lambda ref: jnp.concatenate([ref[b] for b in range(rows)], axis=1)
    per_row = lambda ref: jnp.concatenate([ref[...]] * rows, axis=1)
    r, lw, k, v, a = (side_by_side(ref) for ref in (r_ref, lw_ref, k_ref, v_ref, a_ref))
    row = lax.broadcasted_iota(jnp.int32, (chunk, chunk), 0)
    col = lax.broadcasted_iota(jnp.int32, (chunk, chunk), 1)
    strict = col < row
    incl = col <= row
    tri = incl.astype(BF16)
    lw_hi = lw.astype(BF16)
    lw_rest = lw - lw_hi.astype(F32)
    lw_mid = lw_rest.astype(BF16)
    lw_lo = (lw_rest - lw_mid.astype(F32)).astype(BF16)
    c_inc = (jnp.dot(tri, lw_hi, preferred_element_type=F32)
             + jnp.dot(tri, lw_mid, preferred_element_type=F32)
             + jnp.dot(tri, lw_lo, preferred_element_type=F32))
    c_last = c_inc[chunk - 1:chunk, :]
    e_inc = jnp.exp(c_inc)
    e_exc = jnp.exp(c_inc - lw)
    e_neg = jnp.exp(-c_inc)
    e_tail = jnp.exp(c_last - c_inc)
    p_last = jnp.exp(c_last)

    bd_r = lax.broadcasted_iota(jnp.int32, (COL_TILE, COL_TILE), 0) // HEAD_DIM
    bd_c = lax.broadcasted_iota(jnp.int32, (COL_TILE, COL_TILE), 1) // HEAD_DIM
    ones_bd = (bd_r == bd_c).astype(BF16)

    kk_raw = k * per_row(kkp_ref)
    kk = kk_raw * (1.0 / jnp.maximum(jnp.sqrt(_head_sums(kk_raw * kk_raw, ones_bd)), KK_EPS))
    kka = kk * a
    k2 = k * (1.0 + (a - 1.0) * per_row(kap_ref))
    bonus = _head_sums(r * k2 * per_row(rkp_ref), ones_bd) * v
    a_hat_f = -kk * e_exc
    r_hat_f = r * e_inc
    b_chk_f = kka * e_neg
    k_chk_f = k2 * e_neg
    b_til_f = kka * e_tail
    k_til_f = k2 * e_tail

    eye_h = (lax.broadcasted_iota(jnp.int32, (HEAD_DIM, HEAD_DIM), 0)
             == lax.broadcasted_iota(jnp.int32, (HEAD_DIM, HEAD_DIM), 1))
    n_levels = max(1, int(math.log2(chunk)))
    hs = range(rows * heads)
    sl = [slice(h * HEAD_DIM, (h + 1) * HEAD_DIM) for h in hs]
    a_hat = [a_hat_f[:, s] for s in sl]
    r_hat = [r_hat_f[:, s] for s in sl]
    v_h = [v[:, s] for s in sl]

    amat = [_mm_nt(jnp.concatenate([a_hat[h], r_hat[h]], axis=0),
                   jnp.concatenate([b_chk_f[:, sl[h]], k_chk_f[:, sl[h]]], axis=0)) for h in hs]
    a_ab = [jnp.where(strict, m[:chunk, :chunk], 0.0) for m in amat]
    a_ak = [jnp.where(strict, m[:chunk, chunk:], 0.0) for m in amat]
    a_rb = [jnp.where(incl, m[chunk:, :chunk], 0.0) for m in amat]
    a_rk = [jnp.where(incl, m[chunk:, chunk:], 0.0) for m in amat]
    av = [_mm(jnp.concatenate([a_ak[h], a_rk[h]], axis=0), v_h[h]) for h in hs]

    w12 = [jnp.concatenate([a_hat[h], av[h][:chunk]], axis=1) for h in hs]
    power = a_ab
    for level in range(n_levels):
        w12 = [w12[h] + _mm(power[h], w12[h]) for h in hs]
        if level + 1 < n_levels:
            power = [_mm(p, p) for p in power]

    rx = [_mm(a_rb[h], w12[h]) + jnp.concatenate([r_hat[h], av[h][chunk:]], axis=1) for h in hs]
    gh = [_mm_tn(w12[h], b_til_f[:, sl[h]]) for h in hs]
    vk = [_mm_tn(v_h[h], k_til_f[:, sl[h]]) for h in hs]
    g_t = [gh[h][:HEAD_DIM] + jnp.where(eye_h, p_last[:, sl[h]], 0.0) for h in hs]
    h_t = [gh[h][HEAD_DIM:] + vk[h] for h in hs]

    s_prev = [s_scr[h] for h in hs]
    o_h = [_mm_nt(rx[h][:, :HEAD_DIM], s_prev[h]) + rx[h][:, HEAD_DIM:] for h in hs]
    s_new = [_mm(s_prev[h], g_t[h]) + h_t[h] for h in hs]
    for h in hs:
        s_scr[h] = s_new[h]

    o = jnp.concatenate(o_h, axis=1)
    cen = o - _head_sums(o, ones_bd) * (1.0 / HEAD_DIM)
    var = _head_sums(cen * cen, ones_bd) * (1.0 / HEAD_DIM)
    o = cen * lax.rsqrt(var + LNX_EPS) * per_row(lnw_ref) + per_row(lnb_ref)
    o = (o + bonus) * side_by_side(gate_ref)
    for b in range(rows):
        o_ref[b] = o[:, b * mix:(b + 1) * mix]

    @pl.when(c == pl.num_programs(1) - 1)
    def _():
        for b in range(rows):
            sout_ref[b] = s_scr[b * heads:(b + 1) * heads]


def _wkv(r, lw, k, v, a, gate, s0_layers, W, li, batch, seq_len, chunk):
    mix = r.shape[1]
    heads = mix // HEAD_DIM
    if seq_len % chunk:
        raise ValueError("sequence length must be a multiple of the chunk")
    if mix % COL_TILE:
        raise ValueError("mixer width must be a whole number of 4-head tiles")
    n_chunks = seq_len // chunk
    seq = lambda t: t.reshape(batch, seq_len, mix)
    rows = 2 if batch % 2 == 0 else 1
    blk = pl.BlockSpec((rows, chunk, mix), lambda b, c: (b, c, 0))
    st = pl.BlockSpec((rows, heads, HEAD_DIM, HEAD_DIM), lambda b, c: (b, 0, 0, 0))
    st_in = pl.BlockSpec((None, rows, heads, HEAD_DIM, HEAD_DIM), lambda b, c: (li, b, 0, 0, 0))
    par = pl.BlockSpec((1, mix), lambda b, c: (0, 0))
    params = [W['a_k_k'][li], W['a_k_a'][li], W['a_r_k'][li], W['a_lnx_w'][li], W['a_lnx_b'][li]]
    o, s_out = pl.pallas_call(
        functools.partial(_wkv_kernel, chunk=chunk, heads=heads),
        grid=(batch // rows, n_chunks),
        in_specs=[blk] * 6 + [st_in] + [par] * 5,
        out_specs=[blk, st],
        out_shape=[jax.ShapeDtypeStruct((batch, seq_len, mix), F32),
                   jax.ShapeDtypeStruct(s0_layers.shape[1:], F32)],
        scratch_shapes=[pltpu.VMEM((rows * heads, HEAD_DIM, HEAD_DIM), F32)],
        compiler_params=_cparams("parallel", "arbitrary"),
        name="wkv7",
    )(seq(r), seq(lw), seq(k), seq(v), seq(a), seq(gate), s0_layers, *[p.reshape(1, mix) for p in params])
    return o.reshape(batch * seq_len, mix), s_out


def _mem_attn_kernel(q_ref, kv_ref, o_ref):
    masks = _head_lane_masks(MEM_WIDTH)
    for b in range(q_ref.shape[0]):
        q = q_ref[b]
        tq = q.shape[0]
        mk_t = kv_ref[b, :MEM_WIDTH, :].astype(BF16)
        mv_t = kv_ref[b, MEM_WIDTH:, :].astype(BF16)
        stacked = jnp.concatenate([jnp.where(m, q, 0.0) for m in masks], axis=0)
        s = _mm(stacked, mk_t) * ATTN_SCALE
        p = jnp.exp(s - jnp.max(s, axis=-1, keepdims=True))
        p = p * (1.0 / jnp.sum(p, axis=-1, keepdims=True))
        pv = _mm_nt(p, mv_t)
        out = jnp.zeros(q.shape, F32)
        for h, m in enumerate(masks):
            out = out + jnp.where(m, pv[h * tq:(h + 1) * tq, :], 0.0)
        o_ref[b] = out


def _mem_attn(q_mem, mem_kv, layer, batch, seq_len, *, tq_want=1024):
    tq = _row_tile(seq_len, tq_want)
    m = mem_kv.shape[3]
    nb = max(n for n in (8, 4, 2, 1) if batch % n == 0 and n * tq <= tq_want)
    out = pl.pallas_call(
        _mem_attn_kernel,
        grid=(batch // nb, seq_len // tq),
        in_specs=[pl.BlockSpec((nb, tq, MEM_WIDTH), lambda b, t: (b, t, 0)),
                  pl.BlockSpec((None, nb, 2 * MEM_WIDTH, m), lambda b, t: (layer, b, 0, 0))],
        out_specs=pl.BlockSpec((nb, tq, MEM_WIDTH), lambda b, t: (b, t, 0)),
        out_shape=jax.ShapeDtypeStruct((batch, seq_len, MEM_WIDTH), F32),
        compiler_params=_cparams("parallel", "parallel"),
        name="mem_attn",
    )(q_mem.reshape(batch, seq_len, MEM_WIDTH), mem_kv)
    return out.reshape(batch * seq_len, MEM_WIDTH)


def _out_ffn_kernel(*refs, dils, mix_width, d_ff, tf):
    n_groups = len(dils)
    n_scratch = 2 * sum(dil > 1 for dil in dils)
    scratch = list(refs[len(refs) - n_scratch:])
    it = iter(refs[:len(refs) - n_scratch])
    if n_groups:
        read = lambda ref, dil: ref[...] if dil == 1 else _from_class_view(ref, scratch.pop(0), dil)
        og = [read(next(it), dil) for dil in dils]
        lse = [read(next(it), dil) for dil in dils]
        top = functools.reduce(jnp.maximum, lse)
        wgt = [jnp.exp(s - top) for s in lse]
        o_mix = sum(w * o for w, o in zip(wgt, og)) / sum(wgt)
    else:
        o_mix = next(it)[...]
    om_ref, wout_ref, gpm_ref, h_ref, gpre_ref, win_ref, wo_ref, gpost_ref, o_ref, acc_ref = it
    mixed = (jnp.dot(o_mix.astype(BF16), wout_ref[:mix_width, :], preferred_element_type=F32)
             + jnp.dot(om_ref[...].astype(BF16), wout_ref[mix_width:, :], preferred_element_type=F32))
    h1 = h_ref[...] + _rms(mixed, gpm_ref[...])
    xn = _rms(h1, gpre_ref[...]).astype(BF16)
    for j in range(d_ff // tf):
        gate = jnp.dot(xn, win_ref[:, j * tf:(j + 1) * tf], preferred_element_type=F32)
        up = jnp.dot(xn, win_ref[:, d_ff + j * tf:d_ff + (j + 1) * tf], preferred_element_type=F32)
        act = (gate * _sigmoid(gate) * up).astype(BF16)
        part = jnp.dot(act, wo_ref[j * tf:(j + 1) * tf, :], preferred_element_type=F32)
        if j == 0:
            acc_ref[...] = part
        else:
            acc_ref[...] += part
    o_ref[...] = h1 + _rms(acc_ref[...], gpost_ref[...])


def _out_ffn(o_mix, lse, dils, o_mem, h, W, l, w_out_stack, w_out_layer, *, tm_want=512, tf=256):
    n, d = h.shape
    tm = _row_tile(n, tm_want)
    w_in, w_o = W['w_ffn_in_bf16'], W['w_ffn_out_bf16']
    d_ff = w_o.shape[1]
    assert d_ff % tf == 0
    mix_width = w_out_stack.shape[1] - MEM_WIDTH
    row = lambda w: pl.BlockSpec((tm, w), lambda i: (i, 0))
    if lse is not None:
        lead = list(o_mix) + list(lse)
        lead_specs = [pl.BlockSpec((tm // dil, dil * COL_TILE), lambda i: (i, 0)) for dil in dils] * 2
    else:
        lead, lead_specs, dils = [o_mix], [row(o_mix.shape[1])], ()
    vec = lambda g: g.reshape(1, d)
    return pl.pallas_call(
        functools.partial(_out_ffn_kernel, dils=tuple(dils), mix_width=mix_width, d_ff=d_ff, tf=tf),
        grid=(n // tm,),
        in_specs=lead_specs + [
            row(MEM_WIDTH), _layer_resident(w_out_stack, w_out_layer), _resident((1, d)), row(d),
            _resident((1, d)), _layer_resident(w_in, l), _layer_resident(w_o, l), _resident((1, d))],
        out_specs=row(d),
        out_shape=jax.ShapeDtypeStruct((n, d), F32),
        scratch_shapes=[pltpu.VMEM((tm, d), F32)] + [_class_view_scratch(tm) for dil in dils if dil > 1] * 2,
        compiler_params=_cparams("parallel"),
        name="out_ffn",
    )(*lead, o_mem, w_out_stack, vec(W['g_post_mix'][l]), h, vec(W['g_pre_ffn'][l]), w_in, w_o,
      vec(W['g_post_ffn'][l]))


def _dil_prompt_kernel(q_ref, kc_ref, kh_ref, vc_ref, vh_ref, o_ref, lse_ref, kx_ref, vx_ref, *, tile, n_cls):
    t = pl.program_id(2)
    rows = HEADS_PER_TILE * BAND
    qi = lax.broadcasted_iota(jnp.int32, (rows, 2 * BAND), 0) % BAND
    kj = lax.broadcasted_iota(jnp.int32, (rows, 2 * BAND), 1)
    band = (kj >= qi) & (kj <= qi + BAND)
    band_first = band & (kj >= jnp.where(t == 0, BAND, 0))
    masks = _head_lane_masks(COL_TILE)

    for ci in range(n_cls):
        lanes = slice(ci * COL_TILE, (ci + 1) * COL_TILE)
        kx_ref[ci, 0:BAND, :] = kh_ref[0, :, lanes].astype(BF16)
        kx_ref[ci, BAND:, :] = kc_ref[0, :, lanes].astype(BF16)
        vx_ref[ci, 0:BAND, :] = vh_ref[0, :, lanes].astype(BF16)
        vx_ref[ci, BAND:, :] = vc_ref[0, :, lanes].astype(BF16)
        for blk in range(tile // BAND):
            start = blk * BAND
            q = q_ref[0, start:start + BAND, lanes] * ATTN_SCALE
            keys = kx_ref[ci, start:start + 2 * BAND, :]
            vals = vx_ref[ci, start:start + 2 * BAND, :]
            stacked = jnp.concatenate([jnp.where(m, q, 0.0) for m in masks], axis=0)
            s = jnp.where(band_first if blk == 0 else band, _mm_nt(stacked, keys), -jnp.inf)
            top = jnp.max(s, axis=-1, keepdims=True)
            p = jnp.exp(s - top)
            den = jnp.sum(p, axis=-1, keepdims=True)
            pv = _mm(p, vals) * (1.0 / den)
            lse_rows = top + jnp.log(den)
            out = jnp.zeros((BAND, COL_TILE), F32)
            lse = jnp.zeros((BAND, COL_TILE), F32)
            for h, m in enumerate(masks):
                out = out + jnp.where(m, pv[h * BAND:(h + 1) * BAND, :], 0.0)
                lse = lse + jnp.where(m, lse_rows[h * BAND:(h + 1) * BAND, :], 0.0)
            o_ref[0, start:start + BAND, lanes] = out
            lse_ref[0, start:start + BAND, lanes] = lse


DIL_BLOCKS_PER_STEP = 8


def _dil_prompt_group(q, k, v, batch, seq_len, dil):
    cls_len = seq_len // dil
    tile = min(cls_len, DIL_BLOCKS_PER_STEP * BAND)
    assert cls_len % tile == 0 and tile % BAND == 0
    n_t = cls_len // tile
    n_cls = math.gcd(dil, DIL_BLOCKS_PER_STEP * BAND // tile)
    width = n_cls * COL_TILE
    view = lambda x: x.reshape(batch, cls_len, dil * COL_TILE)
    cur = pl.BlockSpec((1, tile, width), lambda b, c, t: (b, t, c))
    halo = pl.BlockSpec((1, BAND, width),
                        lambda b, c, t: (b, jnp.maximum(t * (tile // BAND) - 1, 0), c))
    out_shape = jax.ShapeDtypeStruct((batch, cls_len, dil * COL_TILE), F32)
    o, lse = pl.pallas_call(
        functools.partial(_dil_prompt_kernel, tile=tile, n_cls=n_cls),
        grid=(batch, dil // n_cls, n_t),
        in_specs=[cur, cur, halo, cur, halo],
        out_specs=[cur, cur],
        out_shape=[out_shape, out_shape],
        scratch_shapes=[pltpu.VMEM((n_cls, tile + BAND, COL_TILE), BF16)] * 2,
        compiler_params=_cparams("parallel", "parallel", "arbitrary"),
        name="dil_attn_prompt",
    )(view(q), view(k), view(k), view(v), view(v))
    flat = lambda x: x.reshape(batch * cls_len, dil * COL_TILE)
    return flat(o), flat(lse)


def _dil_sample_kernel(*refs, seq_len, write_cache):
    q_refs = refs[:N_GROUPS]
    k_refs = refs[N_GROUPS:2 * N_GROUPS]
    v_refs = refs[2 * N_GROUPS:3 * N_GROUPS]
    cache_refs = refs[3 * N_GROUPS:4 * N_GROUPS]
    n_in = (6 if write_cache else 4) * N_GROUPS
    kt_refs, vt_refs = refs[4 * N_GROUPS:5 * N_GROUPS], refs[5 * N_GROUPS:n_in]
    o_ref = refs[n_in]
    new_refs = refs[n_in + 1:]
    masks = _head_lane_masks(COL_TILE)
    rows = HEADS_PER_TILE * seq_len

    run_top = jnp.full((rows, 1), -jnp.inf, F32)
    run_den = jnp.zeros((rows, 1), F32)
    run_acc = jnp.zeros((rows, COL_TILE), F32)
    for g, (win, dil) in enumerate(DIL_GROUPS):
        cache = cache_refs[g]
        length = cache.shape[2]
        q = q_refs[g][...] * ATTN_SCALE
        stacked = jnp.concatenate([jnp.where(m, q, 0.0) for m in masks], axis=0)
        k_new, v_new = k_refs[g][...], v_refs[g][...]
        ck_t = cache[0, :COL_TILE, :]
        cv_t = cache[0, COL_TILE:, :]
        s_c = _mm(stacked, ck_t)
        s_n = _mm_nt(stacked, k_new)
        tq_c = lax.broadcasted_iota(jnp.int32, s_c.shape, 0) % seq_len
        dist_c = length + tq_c - lax.broadcasted_iota(jnp.int32, s_c.shape, 1)
        ok_c = (dist_c % dil == 0) & (dist_c <= win)
        tq_n = lax.broadcasted_iota(jnp.int32, s_n.shape, 0) % seq_len
        dist_n = tq_n - lax.broadcasted_iota(jnp.int32, s_n.shape, 1)
        ok_n = (dist_n >= 0) & (dist_n % dil == 0)
        s_c = jnp.where(ok_c, s_c, -jnp.inf)
        s_n = jnp.where(ok_n, s_n, -jnp.inf)
        top = jnp.maximum(jnp.maximum(jnp.max(s_c, axis=-1, keepdims=True),
                                      jnp.max(s_n, axis=-1, keepdims=True)), run_top)
        p_c = jnp.exp(s_c - top)
        p_n = jnp.exp(s_n - top)
        scale = jnp.exp(run_top - top)
        run_den = run_den * scale + jnp.sum(p_c, axis=-1, keepdims=True) + jnp.sum(p_n, axis=-1, keepdims=True)
        run_acc = run_acc * scale + _mm_nt(p_c, cv_t) + _mm(p_n, v_new)
        run_top = top
        if write_cache:
            new = new_refs[g]
            if length > seq_len:
                new[0] = pltpu.roll(cache[0], length - seq_len, axis=1)
            new[0, :COL_TILE, length - seq_len:] = kt_refs[g][0]
            new[0, COL_TILE:, length - seq_len:] = vt_refs[g][0]
    res = run_acc / run_den
    out = jnp.zeros((seq_len, COL_TILE), F32)
    for h, m in enumerate(masks):
        out = out + jnp.where(m, res[h * seq_len:(h + 1) * seq_len, :], 0.0)
    o_ref[...] = out


def _dil_sample(q_slabs, kv_slabs, caches_t, batch, seq_len, write_cache):
    n = batch * seq_len
    in_specs = [pl.BlockSpec((seq_len, COL_TILE), lambda b: (b, 0))] * (3 * N_GROUPS)
    cache_specs = [pl.BlockSpec((1,) + c.shape[1:], lambda b: (b, 0, 0)) for c in caches_t]
    out_shape = [jax.ShapeDtypeStruct((n, COL_TILE), F32)]
    out_specs = [pl.BlockSpec((seq_len, COL_TILE), lambda b: (b, 0))]
    new_rows_t, new_specs = [], []
    if write_cache:
        out_shape += [jax.ShapeDtypeStruct(c.shape, F32) for c in caches_t]
        out_specs += cache_specs
        new_rows_t = [jnp.swapaxes(s.reshape(batch, seq_len, COL_TILE), 1, 2) for s in kv_slabs]
        new_specs = [pl.BlockSpec((1, COL_TILE, seq_len), lambda b: (b, 0, 0))] * len(kv_slabs)
    res = pl.pallas_call(
        functools.partial(_dil_sample_kernel, seq_len=seq_len, write_cache=write_cache),
        grid=(batch,),
        in_specs=in_specs + cache_specs + new_specs,
        out_specs=out_specs,
        out_shape=out_shape,
        compiler_params=_cparams("parallel"),
        name="dil_attn_sample",
    )(*q_slabs[:N_GROUPS], *kv_slabs, *caches_t, *new_rows_t)
    return res[0], list(res[1:])


def _rope_tables(pos):
    half = HEAD_DIM // 2
    inv = ROPE_THETA ** (-jnp.arange(half, dtype=F32) / half)
    ang = pos.astype(F32)[:, None] * inv[None, :]
    cos, sin = jnp.cos(ang), jnp.sin(ang)
    cos = jnp.tile(jnp.concatenate([cos, cos], axis=-1), (1, LANES // HEAD_DIM))
    sin_signed = jnp.tile(jnp.concatenate([-sin, sin], axis=-1), (1, LANES // HEAD_DIM))
    return cos, sin_signed


def _trunk(h, pos, mem_kv_all, wkv0, shift0, caches, W, batch, seq_len, chunk):
    depth = W['g_pre_mix'].shape[0]
    n_a = W['a_mu'].shape[0]
    cos, sin_signed = _rope_tables(pos)
    new_wkv, new_shift, new_caches = [], [], None
    v_first = None
    kv_tiles = None
    for l in range(depth):
        if l < n_a:
            r, lw, k, v, a, gate, q_mem, xn_last = _rwkv_proj(h, shift0[l], seq_len, W, l, v_first)
            if l == 0:
                v_first = v
            o_mix, s_out = _wkv(r, lw, k, v, a, gate, wkv0, W, l, batch, seq_len, chunk)
            new_wkv.append(s_out)
            new_shift.append(xn_last)
            lse = None
            w_out, w_out_layer = W['a_w_out_bf16'], l
        else:
            j = l - n_a
            dils = tuple(dil for _, dil in DIL_GROUPS) if caches is None else (1,) * N_GROUPS
            q_proj = (W['g_pre_mix'][l], W['b_w_in_bf16'][j], N_GROUPS, dils + (1,))
            if kv_tiles is None:
                kv_proj = (W['g_kv'], W['w_kv_bf16'], N_GROUPS, dils * 2)
                kv_tiles, proj = _norm_proj(h, [kv_proj, q_proj], cos, sin_signed, seq_len)
            else:
                proj, = _norm_proj(h, [q_proj], cos, sin_signed, seq_len)
            q_mem = proj[N_GROUPS]
            if caches is None:
                outs = [_dil_prompt_group(proj[g], kv_tiles[g], kv_tiles[N_GROUPS + g], batch, seq_len, dil)
                        for g, dil in enumerate(dils)]
                o_mix = [o for o, _ in outs]
                lse = [s for _, s in outs]
            else:
                o_mix, written = _dil_sample(proj, kv_tiles, caches, batch, seq_len, new_caches is None)
                if new_caches is None:
                    new_caches = written
                lse = None
            w_out, w_out_layer = W['b_w_out_bf16'], j
        o_mem = _mem_attn(q_mem, mem_kv_all, l, batch, seq_len)
        h = _out_ffn(o_mix, lse, dils if lse is not None else (), o_mem, h, W, l, w_out, w_out_layer)
    return h, jnp.stack(new_wkv), jnp.stack(new_shift), kv_tiles, new_caches


def _prompt_windows(kv_tiles, batch, seq_len):
    bufs = []
    for g, (win, dil) in enumerate(DIL_GROUPS):
        length = min(win, PAST_LEN)
        assert seq_len >= length and length % dil == 0
        rows = seq_len // dil

        def tail(t):
            t = t.reshape(batch, rows, dil * COL_TILE)[:, rows - length // dil:]
            return t.reshape(batch, length, HEADS_PER_TILE, HEAD_DIM)

        bufs.append(jnp.stack([tail(kv_tiles[g]), tail(kv_tiles[N_GROUPS + g])], axis=2))
    return bufs


def kernel(x_prompt, x_sample, state_wkv, state_shift, cache_win_g0, cache_win_g1, cache_win_g2, cache_mem, mem_prompt, g_pre_mix, g_post_mix, g_pre_ffn, g_post_ffn, g_mem, w_mem_kv, w_ffn_in, w_ffn_out, a_mu, a_w_in, a_w0, a_w1, a_w2, a_a0, a_a1, a_a2, a_v0, a_v1, a_v2, a_g1, a_g2, a_k_k, a_k_a, a_r_k, a_lnx_w, a_lnx_b, a_w_out, g_kv, w_kv, b_w_in, b_w_out):
    W = dict(g_pre_mix=g_pre_mix, g_post_mix=g_post_mix, g_pre_ffn=g_pre_ffn, g_post_ffn=g_post_ffn,
             a_mu=a_mu, a_w0=a_w0, a_a0=a_a0, a_v0=a_v0, a_k_k=a_k_k, a_k_a=a_k_a,
             a_r_k=a_r_k.reshape(a_r_k.shape[0], -1), a_lnx_w=a_lnx_w, a_lnx_b=a_lnx_b, g_kv=g_kv)
    for name, w in dict(w_ffn_in=w_ffn_in, w_ffn_out=w_ffn_out, a_w_in=a_w_in, a_w1=a_w1, a_w2=a_w2,
                        a_a1=a_a1, a_a2=a_a2, a_v1=a_v1, a_v2=a_v2, a_g1=a_g1, a_g2=a_g2,
                        a_w_out=a_w_out, w_kv=w_kv, b_w_in=b_w_in, b_w_out=b_w_out,
                        w_mem_kv=w_mem_kv).items():
        W[name + '_bf16'] = w.astype(BF16)

    bp, tp, d = x_prompt.shape
    bs, ts, _ = x_sample.shape
    depth = g_pre_mix.shape[0]
    n_a = a_mu.shape[0]
    heads = a_w0.shape[1] // HEAD_DIM
    m_tok = mem_prompt.shape[1]

    mem_rows = mem_prompt.reshape(bp * m_tok, d)
    no_tab = jnp.zeros((m_tok, LANES), F32)
    mem_kv_p = _norm_proj(mem_rows, [(g_mem[l], W['w_mem_kv_bf16'][l], 0, (1,)) for l in range(depth)],
                          no_tab, no_tab, m_tok)
    mem_kv_p = jnp.stack([slabs[0].reshape(bp, m_tok, 2 * MEM_WIDTH) for slabs in mem_kv_p])
    mem_kv_p_t = jnp.swapaxes(mem_kv_p, 2, 3)
    wkv_zero = jnp.zeros((n_a, bp, heads, HEAD_DIM, HEAD_DIM), F32)
    shift_zero = jnp.zeros((n_a, bp, d), F32)
    y_p, wkv_p, shift_p, kv_p, _ = _trunk(
        x_prompt.reshape(bp * tp, d), jnp.arange(tp, dtype=jnp.int32), mem_kv_p_t, wkv_zero, shift_zero,
        None, W, bp, tp, chunk=64)
    bufs_p = _prompt_windows(kv_p, bp, tp)

    windows = (cache_win_g0, cache_win_g1, cache_win_g2)
    caches_t = [jnp.transpose(c, (0, 2, 3, 4, 1)).reshape(bs, 2 * COL_TILE, c.shape[1]) for c in windows]
    mem_kv_s_t = jnp.transpose(cache_mem, (0, 1, 3, 4, 5, 2)).reshape(depth, bs, 2 * MEM_WIDTH, m_tok)
    y_s, wkv_s, shift_s, _, bufs_s_t = _trunk(
        x_sample.reshape(bs * ts, d), PAST_LEN + jnp.arange(ts, dtype=jnp.int32), mem_kv_s_t, state_wkv,
        state_shift, caches_t, W, bs, ts, chunk=ts)
    bufs_s = [jnp.transpose(b.reshape(bs, 2, HEADS_PER_TILE, HEAD_DIM, c.shape[1]), (0, 4, 1, 2, 3))
              for b, c in zip(bufs_s_t, windows)]

    return (y_p.reshape(bp, tp, d), y_s.reshape(bs, ts, d), wkv_p, shift_p, bufs_p[0], bufs_p[1], bufs_p[2],
            jnp.transpose(mem_kv_p_t.reshape(depth, bp, 2, MEM_HEADS, HEAD_DIM, m_tok), (0, 1, 5, 2, 3, 4)),
            wkv_s, shift_s, bufs_s[0], bufs_s[1], bufs_s[2])
```

```python
import functools
import math

import jax
import jax.numpy as jnp
from jax import lax
from jax.experimental import pallas as pl
from jax.experimental.pallas import tpu as pltpu

F32 = jnp.float32
BF16 = jnp.bfloat16

HEAD_DIM = 64
LANES = 128
COL_TILE = 256
HEADS_PER_TILE = COL_TILE // HEAD_DIM
MEM_HEADS = 4
MEM_WIDTH = MEM_HEADS * HEAD_DIM
DIL_GROUPS = ((128, 1), (512, 4), (2048, 16))
N_GROUPS = len(DIL_GROUPS)
PAST_LEN = 8192
ROPE_THETA = 10000.0
NORM_EPS = 1e-6
LNX_EPS = HEAD_DIM * 1e-5
KK_EPS = 1e-12
ATTN_SCALE = HEAD_DIM ** -0.5
BAND = 128
VMEM_LIMIT = 56 * 1024 * 1024


def _cparams(*sem):
    return pltpu.CompilerParams(dimension_semantics=sem, vmem_limit_bytes=VMEM_LIMIT)


def _rms(x, g):
    return x * lax.rsqrt(jnp.mean(x * x, axis=-1, keepdims=True) + NORM_EPS) * g


def _mm(a, b):
    return jnp.dot(a.astype(BF16), b.astype(BF16), preferred_element_type=F32)


def _mm_nt(a, b):
    return lax.dot_general(a.astype(BF16), b.astype(BF16), (((1,), (1,)), ((), ())),
                           preferred_element_type=F32)


def _mm_tn(a, b):
    return lax.dot_general(a.astype(BF16), b.astype(BF16), (((0,), (0,)), ((), ())),
                           preferred_element_type=F32)


def _sigmoid(x):
    return 1.0 / (1.0 + jnp.exp(-x))


def _head_lane_masks(width):
    lane = lax.broadcasted_iota(jnp.int32, (1, width), 1)
    return [(lane >= h * HEAD_DIM) & (lane < (h + 1) * HEAD_DIM) for h in range(width // HEAD_DIM)]


def _row_tile(n, want):
    t = min(n, want)
    assert n % t == 0, (n, t)
    return t


def _rotate_heads(y, cos, sin_signed):
    width = y.shape[1]
    cos = jnp.concatenate([cos] * (width // LANES), axis=1)
    sin_signed = jnp.concatenate([sin_signed] * (width // LANES), axis=1)
    lane = lax.broadcasted_iota(jnp.int32, y.shape, 1)
    first_half = (lane % HEAD_DIM) < (HEAD_DIM // 2)
    partner = jnp.where(first_half, pltpu.roll(y, width - HEAD_DIM // 2, axis=1),
                        pltpu.roll(y, HEAD_DIM // 2, axis=1))
    return y * cos + partner * sin_signed


def _to_class_view(y, o_ref, scr, dil):
    rows = y.shape[0] // dil
    for half in range(COL_TILE // LANES):
        scr[half] = y[:, half * LANES:(half + 1) * LANES]
    for cls in range(dil):
        for half in range(COL_TILE // LANES):
            lane0 = cls * COL_TILE + half * LANES
            o_ref[:, lane0:lane0 + LANES] = scr[half, pl.ds(cls, rows, stride=dil), :]


def _from_class_view(ref, scr, dil):
    rows = ref.shape[0]
    for cls in range(dil):
        for half in range(COL_TILE // LANES):
            lane0 = cls * COL_TILE + half * LANES
            scr[half, pl.ds(cls, rows, stride=dil), :] = ref[:, lane0:lane0 + LANES]
    return jnp.concatenate([scr[half] for half in range(COL_TILE // LANES)], axis=1)


def _class_view_scratch(tm):
    return pltpu.VMEM((COL_TILE // LANES, tm, LANES), F32)


def _norm_proj_kernel(x_ref, cos_ref, sin_ref, *refs, plan):
    n_proj = len(plan)
    n_outputs = sum(len(dils) for _, dils in plan)
    out_refs = list(refs[2 * n_proj:2 * n_proj + n_outputs])
    scratch = list(refs[2 * n_proj + n_outputs:])
    x = x_ref[...]
    inv_rms = lax.rsqrt(jnp.mean(x * x, axis=-1, keepdims=True) + NORM_EPS)
    for p, (n_rot, dils) in enumerate(plan):
        g_ref, w_ref = refs[2 * p], refs[2 * p + 1]
        xn = (x * inv_rms * g_ref[...]).astype(BF16)
        width = w_ref.shape[1] // len(dils)
        for j, dil in enumerate(dils):
            o_ref = out_refs.pop(0)
            y = jnp.dot(xn, w_ref[:, j * width:(j + 1) * width], preferred_element_type=F32)
            if j < n_rot:
                y = _rotate_heads(y, cos_ref[...], sin_ref[...])
            if dil == 1:
                o_ref[...] = y
            else:
                _to_class_view(y, o_ref, scratch.pop(0), dil)


def _resident(shape):
    return pl.BlockSpec(shape, lambda *_: (0,) * len(shape), pipeline_mode=pl.Buffered(1))


def _layer_resident(stacked, layer):
    shape = stacked.shape[1:]
    return pl.BlockSpec((None,) + shape, lambda *_: (layer,) + (0,) * len(shape),
                        pipeline_mode=pl.Buffered(1))


def _norm_proj(x, projections, cos, sin_signed, seq_len, *, tm_want=1024):
    n, d = x.shape
    plan = tuple((n_rot, tuple(dils)) for _, _, n_rot, dils in projections)
    widths = [w.shape[1] // len(dils) for _, w, _, dils in projections for _ in dils]
    all_dils = [dil for _, dils in plan for dil in dils]
    assert all(width == COL_TILE or dil == 1 for width, dil in zip(widths, all_dils))
    tm = _row_tile(n, tm_want)
    if tm <= seq_len:
        assert seq_len % tm == 0
        per_seq = seq_len // tm
        tab_map = lambda i: (i % per_seq, 0)
    else:
        assert tm % seq_len == 0
        cos = jnp.tile(cos, (tm // seq_len, 1))
        sin_signed = jnp.tile(sin_signed, (tm // seq_len, 1))
        tab_map = lambda i: (0, 0)
    assert all(tm % (8 * dil) == 0 or dil == 1 for dil in all_dils)
    weights, weight_specs = [], []
    for g, w_bf16, _, _ in projections:
        weights += [g.reshape(1, d), w_bf16]
        weight_specs += [_resident((1, d)), _resident(w_bf16.shape)]
    outs = pl.pallas_call(
        functools.partial(_norm_proj_kernel, plan=plan),
        grid=(n // tm,),
        in_specs=[pl.BlockSpec((tm, d), lambda i: (i, 0)),
                  pl.BlockSpec((tm, LANES), tab_map),
                  pl.BlockSpec((tm, LANES), tab_map)] + weight_specs,
        out_specs=[pl.BlockSpec((tm // dil, dil * width), lambda i: (i, 0))
                   for width, dil in zip(widths, all_dils)],
        out_shape=[jax.ShapeDtypeStruct((n // dil, dil * width), F32) for width, dil in zip(widths, all_dils)],
        scratch_shapes=[_class_view_scratch(tm) for dil in all_dils if dil > 1],
        compiler_params=_cparams("parallel"),
        name="norm_proj",
    )(x, cos, sin_signed, *weights)
    outs = list(outs)
    return [[outs.pop(0) for _ in dils] for _, dils in plan]


def _rwkv_proj_kernel(*refs, mix, has_vfirst, tm, seq_len, tail):
    it = iter(refs)
    h_ref = next(it)
    hp_ref = next(it) if seq_len >= tm else None
    sh_ref = next(it)
    g_ref, mu_ref, win_ref = next(it), next(it), next(it)
    w0_ref, w1_ref, w2_ref = next(it), next(it), next(it)
    a0_ref, a1_ref, a2_ref = next(it), next(it), next(it)
    g1_ref, g2_ref = next(it), next(it)
    if has_vfirst:
        vf_ref, v0_ref, v1_ref, v2_ref = next(it), next(it), next(it), next(it)
    r_ref, lw_ref, k_ref, v_ref, a_ref, gate_ref, qm_ref, tail_ref = (next(it) for _ in range(8))

    i = pl.program_id(0)
    gain = g_ref[...]
    xn = _rms(h_ref[...], gain)
    row = lax.broadcasted_iota(jnp.int32, (tm, 1), 0)
    rolled = pltpu.roll(xn, 1, axis=0)
    if seq_len >= tm:
        prev_tile_last = _rms(hp_ref[...], gain)[7:8, :]
        at_seq_start = (i * tm) % seq_len == 0
        first = jnp.where(at_seq_start, sh_ref[0][7:8, :], prev_tile_last)
        x_prev = jnp.where(row == 0, first, rolled)
    else:
        x_prev = jnp.where(row % seq_len == 0, sh_ref[...], rolled)
    xx = x_prev - xn
    mu = mu_ref[...]

    def mixed(idx):
        return (xn + xx * mu[idx:idx + 1, :]).astype(BF16)

    qm_ref[...] = jnp.dot(xn.astype(BF16), win_ref[:, 3 * mix:], preferred_element_type=F32)
    tail_ref[0] = xn[tm - tail:, :]
    r_ref[...] = jnp.dot(mixed(0), win_ref[:, 0:mix], preferred_element_type=F32)
    w_lin = w0_ref[...] + _mm(jnp.tanh(_mm(mixed(1), w1_ref[...])), w2_ref[...])
    lw_ref[...] = -math.exp(-0.5) * _sigmoid(w_lin)
    k_ref[...] = jnp.dot(mixed(2), win_ref[:, mix:2 * mix], preferred_element_type=F32)
    xv = mixed(3)
    v = jnp.dot(xv, win_ref[:, 2 * mix:3 * mix], preferred_element_type=F32)
    if has_vfirst:
        v = v + (vf_ref[...] - v) * _sigmoid(v0_ref[...] + _mm(_mm(xv, v1_ref[...]), v2_ref[...]))
    v_ref[...] = v
    a_ref[...] = _sigmoid(a0_ref[...] + _mm(_mm(mixed(4), a1_ref[...]), a2_ref[...]))
    gate_ref[...] = _mm(_sigmoid(_mm(mixed(5), g1_ref[...])), g2_ref[...])


def _rwkv_proj(h, shift, seq_len, W, li, v_first, *, tm_want=512):
    n, d = h.shape
    mix = W['a_w0'].shape[1]
    tm = _row_tile(n, tm_want)
    n_tiles = n // tm
    has_vfirst = v_first is not None
    row_spec = lambda w: pl.BlockSpec((tm, w), lambda i: (i, 0))
    full = lambda a: _resident(a.shape)

    args, specs = [h], [row_spec(d)]
    if seq_len >= tm:
        assert seq_len % tm == 0
        tail = 8
        args.append(h)
        specs.append(pl.BlockSpec((8, d), lambda i: (jnp.maximum(i * (tm // 8) - 1, 0), 0)))
        args.append(jnp.broadcast_to(shift[:, None, :], (shift.shape[0], 8, d)))
        specs.append(pl.BlockSpec((1, 8, d), lambda i: ((i * tm) // seq_len, 0, 0)))
    else:
        assert tm % seq_len == 0
        tail = tm
        args.append(jnp.repeat(shift, seq_len, axis=0))
        specs.append(row_spec(d))
    lead = [W['g_pre_mix'][li].reshape(1, d), W['a_mu'][li]]
    small = [W['a_w0'][li].reshape(1, mix), W['a_w1_bf16'][li], W['a_w2_bf16'][li],
             W['a_a0'][li].reshape(1, mix), W['a_a1_bf16'][li], W['a_a2_bf16'][li],
             W['a_g1_bf16'][li], W['a_g2_bf16'][li]]
    args += lead + [W['a_w_in_bf16']] + small
    specs += [full(a) for a in lead] + [_layer_resident(W['a_w_in_bf16'], li)] + [full(a) for a in small]
    if has_vfirst:
        vi = li - 1
        extra = [W['a_v0'][vi].reshape(1, mix), W['a_v1_bf16'][vi], W['a_v2_bf16'][vi]]
        args += [v_first] + extra
        specs += [row_spec(mix)] + [full(a) for a in extra]

    wide = jax.ShapeDtypeStruct((n, mix), F32)
    out_shape = [wide] * 6 + [jax.ShapeDtypeStruct((n, MEM_WIDTH), F32),
                              jax.ShapeDtypeStruct((n_tiles, tail, d), F32)]
    out_specs = [row_spec(mix)] * 6 + [row_spec(MEM_WIDTH),
                                       pl.BlockSpec((1, tail, d), lambda i: (i, 0, 0))]
    r, lw, k, v, a, gate, q_mem, xn_tail = pl.pallas_call(
        functools.partial(_rwkv_proj_kernel, mix=mix, has_vfirst=has_vfirst, tm=tm,
                          seq_len=seq_len, tail=tail),
        grid=(n_tiles,),
        in_specs=specs,
        out_specs=out_specs,
        out_shape=out_shape,
        compiler_params=_cparams("parallel"),
        name="rwkv_proj",
    )(*args)
    xn_rows = xn_tail.reshape(n_tiles * tail, d)
    last = xn_rows.reshape(-1, seq_len if seq_len < tm else tail * (seq_len // tm), d)[:, -1]
    return r, lw, k, v, a, gate, q_mem, last


def _head_sums(x, ones_bd):
    hi = x.astype(BF16)
    lo = (x - hi.astype(F32)).astype(BF16)
    tiles = []
    for t in range(x.shape[1] // COL_TILE):
        s = slice(t * COL_TILE, (t + 1) * COL_TILE)
        tiles.append(jnp.dot(hi[:, s], ones_bd, preferred_element_type=F32)
                     + jnp.dot(lo[:, s], ones_bd, preferred_element_type=F32))
    return jnp.concatenate(tiles, axis=1)


def _wkv_kernel(r_ref, lw_ref, k_ref, v_ref, a_ref, gate_ref, s0_ref, kkp_ref, kap_ref, rkp_ref,
                lnw_ref, lnb_ref, o_ref, sout_ref, s_scr, *, chunk, heads):
    c = pl.program_id(1)
    rows = r_ref.shape[0]
    mix = r_ref.shape[2]

    @pl.when(c == 0)
    def _():
        for b in range(rows):
            s_scr[b * heads:(b + 1) * heads] = s0_ref[b]

    side_by_side = lambda ref: jnp.concatenate([ref[b] for b in range(rows)], axis=1)
    per_row = lambda ref: jnp.concatenate([ref[...]] * rows, axis=1)
    r, lw, k, v, a = (side_by_side(ref) for ref in (r_ref, lw_ref, k_ref, v_ref, a_ref))
    row = lax.broadcasted_iota(jnp.int32, (chunk, chunk), 0)
    col = lax.broadcasted_iota(jnp.int32, (chunk, chunk), 1)
    strict = col < row
    incl = col <= row
    tri = incl.astype(BF16)
    lw_hi = lw.astype(BF16)
    lw_rest = lw - lw_hi.astype(F32)
    lw_mid = lw_rest.astype(BF16)
    lw_lo = (lw_rest - lw_mid.astype(F32)).astype(BF16)
    c_inc = (jnp.dot(tri, lw_hi, preferred_element_type=F32)
             + jnp.dot(tri, lw_mid, preferred_element_type=F32)
             + jnp.dot(tri, lw_lo, preferred_element_type=F32))
    c_last = c_inc[chunk - 1:chunk, :]
    e_inc = jnp.exp(c_inc)
    e_exc = jnp.exp(c_inc - lw)
    e_neg = jnp.exp(-c_inc)
    e_tail = jnp.exp(c_last - c_inc)
    p_last = jnp.exp(c_last)

    bd_r = lax.broadcasted_iota(jnp.int32, (COL_TILE, COL_TILE), 0) // HEAD_DIM
    bd_c = lax.broadcasted_iota(jnp.int32, (COL_TILE, COL_TILE), 1) // HEAD_DIM
    ones_bd = (bd_r == bd_c).astype(BF16)

    kk_raw = k * per_row(kkp_ref)
    kk = kk_raw * (1.0 / jnp.maximum(jnp.sqrt(_head_sums(kk_raw * kk_raw, ones_bd)), KK_EPS))
    kka = kk * a
    k2 = k * (1.0 + (a - 1.0) * per_row(kap_ref))
    bonus = _head_sums(r * k2 * per_row(rkp_ref), ones_bd) * v
    a_hat_f = -kk * e_exc
    r_hat_f = r * e_inc
    b_chk_f = kka * e_neg
    k_chk_f = k2 * e_neg
    b_til_f = kka * e_tail
    k_til_f = k2 * e_tail

    eye_h = (lax.broadcasted_iota(jnp.int32, (HEAD_DIM, HEAD_DIM), 0)
             == lax.broadcasted_iota(jnp.int32, (HEAD_DIM, HEAD_DIM), 1))
    n_levels = max(1, int(math.log2(chunk)))
    hs = range(rows * heads)
    sl = [slice(h * HEAD_DIM, (h + 1) * HEAD_DIM) for h in hs]
    a_hat = [a_hat_f[:, s] for s in sl]
    r_hat = [r_hat_f[:, s] for s in sl]
    v_h = [v[:, s] for s in sl]

    amat = [_mm_nt(jnp.concatenate([a_hat[h], r_hat[h]], axis=0),
                   jnp.concatenate([b_chk_f[:, sl[h]], k_chk_f[:, sl[h]]], axis=0)) for h in hs]
    a_ab = [jnp.where(strict, m[:chunk, :chunk], 0.0) for m in amat]
    a_ak = [jnp.where(strict, m[:chunk, chunk:], 0.0) for m in amat]
    a_rb = [jnp.where(incl, m[chunk:, :chunk], 0.0) for m in amat]
    a_rk = [jnp.where(incl, m[chunk:, chunk:], 0.0) for m in amat]
    av = [_mm(jnp.concatenate([a_ak[h], a_rk[h]], axis=0), v_h[h]) for h in hs]

    w12 = [jnp.concatenate([a_hat[h], av[h][:chunk]], axis=1) for h in hs]
    power = a_ab
    for level in range(n_levels):
        w12 = [w12[h] + _mm(power[h], w12[h]) for h in hs]
        if level + 1 < n_levels:
            power = [_mm(p, p) for p in power]

    rx = [_mm(a_rb[h], w12[h]) + jnp.concatenate([r_hat[h], av[h][chunk:]], axis=1) for h in hs]
    gh = [_mm_tn(w12[h], b_til_f[:, sl[h]]) for h in hs]
    vk = [_mm_tn(v_h[h], k_til_f[:, sl[h]]) for h in hs]
    g_t = [gh[h][:HEAD_DIM] + jnp.where(eye_h, p_last[:, sl[h]], 0.0) for h in hs]
    h_t = [gh[h][HEAD_DIM:] + vk[h] for h in hs]

    s_prev = [s_scr[h] for h in hs]
    o_h = [_mm_nt(rx[h][:, :HEAD_DIM], s_prev[h]) + rx[h][:, HEAD_DIM:] for h in hs]
    s_new = [_mm(s_prev[h], g_t[h]) + h_t[h] for h in hs]
    for h in hs:
        s_scr[h] = s_new[h]

    o = jnp.concatenate(o_h, axis=1)
    cen = o - _head_sums(o, ones_bd) * (1.0 / HEAD_DIM)
    var = _head_sums(cen * cen, ones_bd) * (1.0 / HEAD_DIM)
    o = cen * lax.rsqrt(var + LNX_EPS) * per_row(lnw_ref) + per_row(lnb_ref)
    o = (o + bonus) * side_by_side(gate_ref)
    for b in range(rows):
        o_ref[b] = o[:, b * mix:(b + 1) * mix]

    @pl.when(c == pl.num_programs(1) - 1)
    def _():
        for b in range(rows):
            sout_ref[b] = s_scr[b * heads:(b + 1) * heads]


def _wkv(r, lw, k, v, a, gate, s0_layers, W, li, batch, seq_len, chunk):
    mix = r.shape[1]
    heads = mix // HEAD_DIM
    if seq_len % chunk:
        raise ValueError("sequence length must be a multiple of the chunk")
    if mix % COL_TILE:
        raise ValueError("mixer width must be a whole number of 4-head tiles")
    n_chunks = seq_len // chunk
    seq = lambda t: t.reshape(batch, seq_len, mix)
    rows = max(n for n in (4, 2, 1) if batch % n == 0)
    blk = pl.BlockSpec((rows, chunk, mix), lambda b, c: (b, c, 0))
    st = pl.BlockSpec((rows, heads, HEAD_DIM, HEAD_DIM), lambda b, c: (b, 0, 0, 0))
    st_in = pl.BlockSpec((None, rows, heads, HEAD_DIM, HEAD_DIM), lambda b, c: (li, b, 0, 0, 0))
    par = pl.BlockSpec((1, mix), lambda b, c: (0, 0))
    params = [W['a_k_k'][li], W['a_k_a'][li], W['a_r_k'][li], W['a_lnx_w'][li], W['a_lnx_b'][li]]
    o, s_out = pl.pallas_call(
        functools.partial(_wkv_kernel, chunk=chunk, heads=heads),
        grid=(batch // rows, n_chunks),
        in_specs=[blk] * 6 + [st_in] + [par] * 5,
        out_specs=[blk, st],
        out_shape=[jax.ShapeDtypeStruct((batch, seq_len, mix), F32),
                   jax.ShapeDtypeStruct(s0_layers.shape[1:], F32)],
        scratch_shapes=[pltpu.VMEM((rows * heads, HEAD_DIM, HEAD_DIM), F32)],
        compiler_params=_cparams("parallel", "arbitrary"),
        name="wkv7",
    )(seq(r), seq(lw), seq(k), seq(v), seq(a), seq(gate), s0_layers, *[p.reshape(1, mix) for p in params])
    return o.reshape(batch * seq_len, mix), s_out


def _mem_attn_kernel(q_ref, kv_ref, o_ref):
    masks = _head_lane_masks(MEM_WIDTH)
    for b in range(q_ref.shape[0]):
        q = q_ref[b]
        tq = q.shape[0]
        mk_t = kv_ref[b, :MEM_WIDTH, :].astype(BF16)
        mv_t = kv_ref[b, MEM_WIDTH:, :].astype(BF16)
        stacked = jnp.concatenate([jnp.where(m, q, 0.0) for m in masks], axis=0)
        s = _mm(stacked, mk_t) * ATTN_SCALE
        p = jnp.exp(s - jnp.max(s, axis=-1, keepdims=True))
        p = p * (1.0 / jnp.sum(p, axis=-1, keepdims=True))
        pv = _mm_nt(p, mv_t)
        out = jnp.zeros(q.shape, F32)
        for h, m in enumerate(masks):
            out = out + jnp.where(m, pv[h * tq:(h + 1) * tq, :], 0.0)
        o_ref[b] = out


def _mem_attn(q_mem, mem_kv, layer, batch, seq_len, *, tq_want=1024):
    tq = _row_tile(seq_len, tq_want)
    m = mem_kv.shape[3]
    nb = max(n for n in (8, 4, 2, 1) if batch % n == 0 and n * tq <= tq_want)
    out = pl.pallas_call(
        _mem_attn_kernel,
        grid=(batch // nb, seq_len // tq),
        in_specs=[pl.BlockSpec((nb, tq, MEM_WIDTH), lambda b, t: (b, t, 0)),
                  pl.BlockSpec((None, nb, 2 * MEM_WIDTH, m), lambda b, t: (layer, b, 0, 0))],
        out_specs=pl.BlockSpec((nb, tq, MEM_WIDTH), lambda b, t: (b, t, 0)),
        out_shape=jax.ShapeDtypeStruct((batch, seq_len, MEM_WIDTH), F32),
        compiler_params=_cparams("parallel", "parallel"),
        name="mem_attn",
    )(q_mem.reshape(batch, seq_len, MEM_WIDTH), mem_kv)
    return out.reshape(batch * seq_len, MEM_WIDTH)


def _out_ffn_kernel(*refs, dils, mix_width, d_ff, tf):
    n_groups = len(dils)
    n_scratch = 2 * sum(dil > 1 for dil in dils)
    scratch = list(refs[len(refs) - n_scratch:])
    it = iter(refs[:len(refs) - n_scratch])
    if n_groups:
        read = lambda ref, dil: ref[...] if dil == 1 else _from_class_view(ref, scratch.pop(0), dil)
        og = [read(next(it), dil) for dil in dils]
        lse = [read(next(it), dil) for dil in dils]
        top = functools.reduce(jnp.maximum, lse)
        wgt = [jnp.exp(s - top) for s in lse]
        o_mix = sum(w * o for w, o in zip(wgt, og)) / sum(wgt)
    else:
        o_mix = next(it)[...]
    om_ref, wout_ref, gpm_ref, h_ref, gpre_ref, win_ref, wo_ref, gpost_ref, o_ref, acc_ref = it
    mixed = (jnp.dot(o_mix.astype(BF16), wout_ref[:mix_width, :], preferred_element_type=F32)
             + jnp.dot(om_ref[...].astype(BF16), wout_ref[mix_width:, :], preferred_element_type=F32))
    h1 = h_ref[...] + _rms(mixed, gpm_ref[...])
    xn = _rms(h1, gpre_ref[...]).astype(BF16)
    for j in range(d_ff // tf):
        gate = jnp.dot(xn, win_ref[:, j * tf:(j + 1) * tf], preferred_element_type=F32)
        up = jnp.dot(xn, win_ref[:, d_ff + j * tf:d_ff + (j + 1) * tf], preferred_element_type=F32)
        act = (gate * _sigmoid(gate) * up).astype(BF16)
        part = jnp.dot(act, wo_ref[j * tf:(j + 1) * tf, :], preferred_element_type=F32)
        if j == 0:
            acc_ref[...] = part
        else:
            acc_ref[...] += part
    o_ref[...] = h1 + _rms(acc_ref[...], gpost_ref[...])


def _out_ffn(o_mix, lse, dils, o_mem, h, W, l, w_out_stack, w_out_layer, *, tm_want=512, tf=256):
    n, d = h.shape
    tm = _row_tile(n, tm_want)
    w_in, w_o = W['w_ffn_in_bf16'], W['w_ffn_out_bf16']
    d_ff = w_o.shape[1]
    assert d_ff % tf == 0
    mix_width = w_out_stack.shape[1] - MEM_WIDTH
    row = lambda w: pl.BlockSpec((tm, w), lambda i: (i, 0))
    if lse is not None:
        lead = list(o_mix) + list(lse)
        lead_specs = [pl.BlockSpec((tm // dil, dil * COL_TILE), lambda i: (i, 0)) for dil in dils] * 2
    else:
        lead, lead_specs, dils = [o_mix], [row(o_mix.shape[1])], ()
    vec = lambda g: g.reshape(1, d)
    return pl.pallas_call(
        functools.partial(_out_ffn_kernel, dils=tuple(dils), mix_width=mix_width, d_ff=d_ff, tf=tf),
        grid=(n // tm,),
        in_specs=lead_specs + [
            row(MEM_WIDTH), _layer_resident(w_out_stack, w_out_layer), _resident((1, d)), row(d),
            _resident((1, d)), _layer_resident(w_in, l), _layer_resident(w_o, l), _resident((1, d))],
        out_specs=row(d),
        out_shape=jax.ShapeDtypeStruct((n, d), F32),
        scratch_shapes=[pltpu.VMEM((tm, d), F32)] + [_class_view_scratch(tm) for dil in dils if dil > 1] * 2,
        compiler_params=_cparams("parallel"),
        name="out_ffn",
    )(*lead, o_mem, w_out_stack, vec(W['g_post_mix'][l]), h, vec(W['g_pre_ffn'][l]), w_in, w_o,
      vec(W['g_post_ffn'][l]))


def _dil_prompt_kernel(q_ref, kc_ref, kh_ref, vc_ref, vh_ref, o_ref, lse_ref, kx_ref, vx_ref, *, tile, n_cls):
    t = pl.program_id(2)
    rows = HEADS_PER_TILE * BAND
    qi = lax.broadcasted_iota(jnp.int32, (rows, 2 * BAND), 0) % BAND
    kj = lax.broadcasted_iota(jnp.int32, (rows, 2 * BAND), 1)
    band = (kj >= qi) & (kj <= qi + BAND)
    band_first = band & (kj >= jnp.where(t == 0, BAND, 0))
    masks = _head_lane_masks(COL_TILE)

    for ci in range(n_cls):
        lanes = slice(ci * COL_TILE, (ci + 1) * COL_TILE)
        kx_ref[ci, 0:BAND, :] = kh_ref[0, :, lanes].astype(BF16)
        kx_ref[ci, BAND:, :] = kc_ref[0, :, lanes].astype(BF16)
        vx_ref[ci, 0:BAND, :] = vh_ref[0, :, lanes].astype(BF16)
        vx_ref[ci, BAND:, :] = vc_ref[0, :, lanes].astype(BF16)
        for blk in range(tile // BAND):
            start = blk * BAND
            q = q_ref[0, start:start + BAND, lanes] * ATTN_SCALE
            keys = kx_ref[ci, start:start + 2 * BAND, :]
            vals = vx_ref[ci, start:start + 2 * BAND, :]
            stacked = jnp.concatenate([jnp.where(m, q, 0.0) for m in masks], axis=0)
            s = jnp.where(band_first if blk == 0 else band, _mm_nt(stacked, keys), -jnp.inf)
            top = jnp.max(s, axis=-1, keepdims=True)
            p = jnp.exp(s - top)
            den = jnp.sum(p, axis=-1, keepdims=True)
            pv = _mm(p, vals) * (1.0 / den)
            lse_rows = top + jnp.log(den)
            out = jnp.zeros((BAND, COL_TILE), F32)
            lse = jnp.zeros((BAND, COL_TILE), F32)
            for h, m in enumerate(masks):
                out = out + jnp.where(m, pv[h * BAND:(h + 1) * BAND, :], 0.0)
                lse = lse + jnp.where(m, lse_rows[h * BAND:(h + 1) * BAND, :], 0.0)
            o_ref[0, start:start + BAND, lanes] = out
            lse_ref[0, start:start + BAND, lanes] = lse


DIL_BLOCKS_PER_STEP = 8


def _dil_prompt_group(q, k, v, batch, seq_len, dil):
    cls_len = seq_len // dil
    tile = min(cls_len, DIL_BLOCKS_PER_STEP * BAND)
    assert cls_len % tile == 0 and tile % BAND == 0
    n_t = cls_len // tile
    n_cls = math.gcd(dil, DIL_BLOCKS_PER_STEP * BAND // tile)
    width = n_cls * COL_TILE
    view = lambda x: x.reshape(batch, cls_len, dil * COL_TILE)
    cur = pl.BlockSpec((1, tile, width), lambda b, c, t: (b, t, c))
    halo = pl.BlockSpec((1, BAND, width),
                        lambda b, c, t: (b, jnp.maximum(t * (tile // BAND) - 1, 0), c))
    out_shape = jax.ShapeDtypeStruct((batch, cls_len, dil * COL_TILE), F32)
    o, lse = pl.pallas_call(
        functools.partial(_dil_prompt_kernel, tile=tile, n_cls=n_cls),
        grid=(batch, dil // n_cls, n_t),
        in_specs=[cur, cur, halo, cur, halo],
        out_specs=[cur, cur],
        out_shape=[out_shape, out_shape],
        scratch_shapes=[pltpu.VMEM((n_cls, tile + BAND, COL_TILE), BF16)] * 2,
        compiler_params=_cparams("parallel", "parallel", "arbitrary"),
        name="dil_attn_prompt",
    )(view(q), view(k), view(k), view(v), view(v))
    flat = lambda x: x.reshape(batch * cls_len, dil * COL_TILE)
    return flat(o), flat(lse)


def _dil_sample_kernel(*refs, seq_len, write_cache):
    q_refs = refs[:N_GROUPS]
    k_refs = refs[N_GROUPS:2 * N_GROUPS]
    v_refs = refs[2 * N_GROUPS:3 * N_GROUPS]
    cache_refs = refs[3 * N_GROUPS:4 * N_GROUPS]
    n_in = (6 if write_cache else 4) * N_GROUPS
    kt_refs, vt_refs = refs[4 * N_GROUPS:5 * N_GROUPS], refs[5 * N_GROUPS:n_in]
    o_ref = refs[n_in]
    new_refs = refs[n_in + 1:]
    masks = _head_lane_masks(COL_TILE)
    rows = HEADS_PER_TILE * seq_len

    run_top = jnp.full((rows, 1), -jnp.inf, F32)
    run_den = jnp.zeros((rows, 1), F32)
    run_acc = jnp.zeros((rows, COL_TILE), F32)
    for g, (win, dil) in enumerate(DIL_GROUPS):
        cache = cache_refs[g]
        length = cache.shape[2]
        q = q_refs[g][...] * ATTN_SCALE
        stacked = jnp.concatenate([jnp.where(m, q, 0.0) for m in masks], axis=0)
        k_new, v_new = k_refs[g][...], v_refs[g][...]
        ck_t = cache[0, :COL_TILE, :]
        cv_t = cache[0, COL_TILE:, :]
        s_c = _mm(stacked, ck_t)
        s_n = _mm_nt(stacked, k_new)
        tq_c = lax.broadcasted_iota(jnp.int32, s_c.shape, 0) % seq_len
        dist_c = length + tq_c - lax.broadcasted_iota(jnp.int32, s_c.shape, 1)
        ok_c = (dist_c % dil == 0) & (dist_c <= win)
        tq_n = lax.broadcasted_iota(jnp.int32, s_n.shape, 0) % seq_len
        dist_n = tq_n - lax.broadcasted_iota(jnp.int32, s_n.shape, 1)
        ok_n = (dist_n >= 0) & (dist_n % dil == 0)
        s_c = jnp.where(ok_c, s_c, -jnp.inf)
        s_n = jnp.where(ok_n, s_n, -jnp.inf)
        top = jnp.maximum(jnp.maximum(jnp.max(s_c, axis=-1, keepdims=True),
                                      jnp.max(s_n, axis=-1, keepdims=True)), run_top)
        p_c = jnp.exp(s_c - top)
        p_n = jnp.exp(s_n - top)
        scale = jnp.exp(run_top - top)
        run_den = run_den * scale + jnp.sum(p_c, axis=-1, keepdims=True) + jnp.sum(p_n, axis=-1, keepdims=True)
        run_acc = run_acc * scale + _mm_nt(p_c, cv_t) + _mm(p_n, v_new)
        run_top = top
        if write_cache:
            new = new_refs[g]
            if length > seq_len:
                new[0] = pltpu.roll(cache[0], length - seq_len, axis=1)
            new[0, :COL_TILE, length - seq_len:] = kt_refs[g][0]
            new[0, COL_TILE:, length - seq_len:] = vt_refs[g][0]
    res = run_acc / run_den
    out = jnp.zeros((seq_len, COL_TILE), F32)
    for h, m in enumerate(masks):
        out = out + jnp.where(m, res[h * seq_len:(h + 1) * seq_len, :], 0.0)
    o_ref[...] = out


def _dil_sample(q_slabs, kv_slabs, caches_t, batch, seq_len, write_cache):
    n = batch * seq_len
    in_specs = [pl.BlockSpec((seq_len, COL_TILE), lambda b: (b, 0))] * (3 * N_GROUPS)
    cache_specs = [pl.BlockSpec((1,) + c.shape[1:], lambda b: (b, 0, 0)) for c in caches_t]
    out_shape = [jax.ShapeDtypeStruct((n, COL_TILE), F32)]
    out_specs = [pl.BlockSpec((seq_len, COL_TILE), lambda b: (b, 0))]
    new_rows_t, new_specs = [], []
    if write_cache:
        out_shape += [jax.ShapeDtypeStruct(c.shape, F32) for c in caches_t]
        out_specs += cache_specs
        new_rows_t = [jnp.swapaxes(s.reshape(batch, seq_len, COL_TILE), 1, 2) for s in kv_slabs]
        new_specs = [pl.BlockSpec((1, COL_TILE, seq_len), lambda b: (b, 0, 0))] * len(kv_slabs)
    res = pl.pallas_call(
        functools.partial(_dil_sample_kernel, seq_len=seq_len, write_cache=write_cache),
        grid=(batch,),
        in_specs=in_specs + cache_specs + new_specs,
        out_specs=out_specs,
        out_shape=out_shape,
        compiler_params=_cparams("parallel"),
        name="dil_attn_sample",
    )(*q_slabs[:N_GROUPS], *kv_slabs, *caches_t, *new_rows_t)
    return res[0], list(res[1:])


def _rope_tables(pos):
    half = HEAD_DIM // 2
    inv = ROPE_THETA ** (-jnp.arange(half, dtype=F32) / half)
    ang = pos.astype(F32)[:, None] * inv[None, :]
    cos, sin = jnp.cos(ang), jnp.sin(ang)
    cos = jnp.tile(jnp.concatenate([cos, cos], axis=-1), (1, LANES // HEAD_DIM))
    sin_signed = jnp.tile(jnp.concatenate([-sin, sin], axis=-1), (1, LANES // HEAD_DIM))
    return cos, sin_signed


def _trunk(h, pos, mem_kv_all, wkv0, shift0, caches, W, batch, seq_len, chunk):
    depth = W['g_pre_mix'].shape[0]
    n_a = W['a_mu'].shape[0]
    cos, sin_signed = _rope_tables(pos)
    new_wkv, new_shift, new_caches = [], [], None
    v_first = None
    kv_tiles = None
    for l in range(depth):
        if l < n_a:
            r, lw, k, v, a, gate, q_mem, xn_last = _rwkv_proj(h, shift0[l], seq_len, W, l, v_first)
            if l == 0:
                v_first = v
            o_mix, s_out = _wkv(r, lw, k, v, a, gate, wkv0, W, l, batch, seq_len, chunk)
            new_wkv.append(s_out)
            new_shift.append(xn_last)
            lse = None
            w_out, w_out_layer = W['a_w_out_bf16'], l
        else:
            j = l - n_a
            dils = tuple(dil for _, dil in DIL_GROUPS) if caches is None else (1,) * N_GROUPS
            q_proj = (W['g_pre_mix'][l], W['b_w_in_bf16'][j], N_GROUPS, dils + (1,))
            if kv_tiles is None:
                kv_proj = (W['g_kv'], W['w_kv_bf16'], N_GROUPS, dils * 2)
                kv_tiles, proj = _norm_proj(h, [kv_proj, q_proj], cos, sin_signed, seq_len)
            else:
                proj, = _norm_proj(h, [q_proj], cos, sin_signed, seq_len)
            q_mem = proj[N_GROUPS]
            if caches is None:
                outs = [_dil_prompt_group(proj[g], kv_tiles[g], kv_tiles[N_GROUPS + g], batch, seq_len, dil)
                        for g, dil in enumerate(dils)]
                o_mix = [o for o, _ in outs]
                lse = [s for _, s in outs]
            else:
                o_mix, written = _dil_sample(proj, kv_tiles, caches, batch, seq_len, new_caches is None)
                if new_caches is None:
                    new_caches = written
                lse = None
            w_out, w_out_layer = W['b_w_out_bf16'], j
        o_mem = _mem_attn(q_mem, mem_kv_all, l, batch, seq_len)
        h = _out_ffn(o_mix, lse, dils if lse is not None else (), o_mem, h, W, l, w_out, w_out_layer)
    return h, jnp.stack(new_wkv), jnp.stack(new_shift), kv_tiles, new_caches


def _prompt_windows(kv_tiles, batch, seq_len):
    bufs = []
    for g, (win, dil) in enumerate(DIL_GROUPS):
        length = min(win, PAST_LEN)
        assert seq_len >= length and length % dil == 0
        rows = seq_len // dil

        def tail(t):
            t = t.reshape(batch, rows, dil * COL_TILE)[:, rows - length // dil:]
            return t.reshape(batch, length, HEADS_PER_TILE, HEAD_DIM)

        bufs.append(jnp.stack([tail(kv_tiles[g]), tail(kv_tiles[N_GROUPS + g])], axis=2))
    return bufs


def kernel(x_prompt, x_sample, state_wkv, state_shift, cache_win_g0, cache_win_g1, cache_win_g2, cache_mem, mem_prompt, g_pre_mix, g_post_mix, g_pre_ffn, g_post_ffn, g_mem, w_mem_kv, w_ffn_in, w_ffn_out, a_mu, a_w_in, a_w0, a_w1, a_w2, a_a0, a_a1, a_a2, a_v0, a_v1, a_v2, a_g1, a_g2, a_k_k, a_k_a, a_r_k, a_lnx_w, a_lnx_b, a_w_out, g_kv, w_kv, b_w_in, b_w_out):
    W = dict(g_pre_mix=g_pre_mix, g_post_mix=g_post_mix, g_pre_ffn=g_pre_ffn, g_post_ffn=g_post_ffn,
             a_mu=a_mu, a_w0=a_w0, a_a0=a_a0, a_v0=a_v0, a_k_k=a_k_k, a_k_a=a_k_a,
             a_r_k=a_r_k.reshape(a_r_k.shape[0], -1), a_lnx_w=a_lnx_w, a_lnx_b=a_lnx_b, g_kv=g_kv)
    for name, w in dict(w_ffn_in=w_ffn_in, w_ffn_out=w_ffn_out, a_w_in=a_w_in, a_w1=a_w1, a_w2=a_w2,
                        a_a1=a_a1, a_a2=a_a2, a_v1=a_v1, a_v2=a_v2, a_g1=a_g1, a_g2=a_g2,
                        a_w_out=a_w_out, w_kv=w_kv, b_w_in=b_w_in, b_w_out=b_w_out,
                        w_mem_kv=w_mem_kv).items():
        W[name + '_bf16'] = w.astype(BF16)

    bp, tp, d = x_prompt.shape
    bs, ts, _ = x_sample.shape
    depth = g_pre_mix.shape[0]
    n_a = a_mu.shape[0]
    heads = a_w0.shape[1] // HEAD_DIM
    m_tok = mem_prompt.shape[1]

    mem_rows = mem_prompt.reshape(bp * m_tok, d)
    no_tab = jnp.zeros((m_tok, LANES), F32)
    mem_kv_p = _norm_proj(mem_rows, [(g_mem[l], W['w_mem_kv_bf16'][l], 0, (1,)) for l in range(depth)],
                          no_tab, no_tab, m_tok)
    mem_kv_p = jnp.stack([slabs[0].reshape(bp, m_tok, 2 * MEM_WIDTH) for slabs in mem_kv_p])
    mem_kv_p_t = jnp.swapaxes(mem_kv_p, 2, 3)
    wkv_zero = jnp.zeros((n_a, bp, heads, HEAD_DIM, HEAD_DIM), F32)
    shift_zero = jnp.zeros((n_a, bp, d), F32)
    y_p, wkv_p, shift_p, kv_p, _ = _trunk(
        x_prompt.reshape(bp * tp, d), jnp.arange(tp, dtype=jnp.int32), mem_kv_p_t, wkv_zero, shift_zero,
        None, W, bp, tp, chunk=64)
    bufs_p = _prompt_windows(kv_p, bp, tp)

    windows = (cache_win_g0, cache_win_g1, cache_win_g2)
    caches_t = [jnp.transpose(c, (0, 2, 3, 4, 1)).reshape(bs, 2 * COL_TILE, c.shape[1]) for c in windows]
    mem_kv_s_t = jnp.transpose(cache_mem, (0, 1, 3, 4, 5, 2)).reshape(depth, bs, 2 * MEM_WIDTH, m_tok)
    y_s, wkv_s, shift_s, _, bufs_s_t = _trunk(
        x_sample.reshape(bs * ts, d), PAST_LEN + jnp.arange(ts, dtype=jnp.int32), mem_kv_s_t, state_wkv,
        state_shift, caches_t, W, bs, ts, chunk=ts)
    bufs_s = [jnp.transpose(b.reshape(bs, 2, HEADS_PER_TILE, HEAD_DIM, c.shape[1]), (0, 4, 1, 2, 3))
              for b, c in zip(bufs_s_t, windows)]

    return (y_p.reshape(bp, tp, d), y_s.reshape(bs, ts, d), wkv_p, shift_p, bufs_p[0], bufs_p[1], bufs_p[2],
            jnp.transpose(mem_kv_p_t.reshape(depth, bp, 2, MEM_HEADS, HEAD_DIM, m_tok), (0, 1, 5, 2, 3, 4)),
            wkv_s, shift_s, bufs_s[0], bufs_s[1], bufs_s[2])
```

```python
import functools
import math

import jax
import jax.numpy as jnp
from jax import lax
from jax.experimental import pallas as pl
from jax.experimental.pallas import tpu as pltpu

F32 = jnp.float32
BF16 = jnp.bfloat16

HEAD_DIM = 64
LANES = 128
COL_TILE = 256
HEADS_PER_TILE = COL_TILE // HEAD_DIM
MEM_HEADS = 4
MEM_WIDTH = MEM_HEADS * HEAD_DIM
DIL_GROUPS = ((128, 1), (512, 4), (2048, 16))
N_GROUPS = len(DIL_GROUPS)
PAST_LEN = 8192
ROPE_THETA = 10000.0
NORM_EPS = 1e-6
LNX_EPS = HEAD_DIM * 1e-5
KK_EPS = 1e-12
ATTN_SCALE = HEAD_DIM ** -0.5
BAND = 128
VMEM_LIMIT = 56 * 1024 * 1024


def _cparams(*sem):
    return pltpu.CompilerParams(dimension_semantics=sem, vmem_limit_bytes=VMEM_LIMIT)


def _rms(x, g):
    return x * lax.rsqrt(jnp.mean(x * x, axis=-1, keepdims=True) + NORM_EPS) * g


def _mm(a, b):
    return jnp.dot(a.astype(BF16), b.astype(BF16), preferred_element_type=F32)


def _mm_nt(a, b):
    return lax.dot_general(a.astype(BF16), b.astype(BF16), (((1,), (1,)), ((), ())),
                           preferred_element_type=F32)


def _mm_tn(a, b):
    return lax.dot_general(a.astype(BF16), b.astype(BF16), (((0,), (0,)), ((), ())),
                           preferred_element_type=F32)


def _sigmoid(x):
    return 1.0 / (1.0 + jnp.exp(-x))


def _head_lane_masks(width):
    lane = lax.broadcasted_iota(jnp.int32, (1, width), 1)
    return [(lane >= h * HEAD_DIM) & (lane < (h + 1) * HEAD_DIM) for h in range(width // HEAD_DIM)]


def _row_tile(n, want):
    t = min(n, want)
    assert n % t == 0, (n, t)
    return t


def _rotate_heads(y, cos, sin_signed):
    width = y.shape[1]
    cos = jnp.concatenate([cos] * (width // LANES), axis=1)
    sin_signed = jnp.concatenate([sin_signed] * (width // LANES), axis=1)
    lane = lax.broadcasted_iota(jnp.int32, y.shape, 1)
    first_half = (lane % HEAD_DIM) < (HEAD_DIM // 2)
    partner = jnp.where(first_half, pltpu.roll(y, width - HEAD_DIM // 2, axis=1),
                        pltpu.roll(y, HEAD_DIM // 2, axis=1))
    return y * cos + partner * sin_signed


def _to_class_view(y, o_ref, scr, dil):
    rows = y.shape[0] // dil
    for half in range(COL_TILE // LANES):
        scr[half] = y[:, half * LANES:(half + 1) * LANES]
    for cls in range(dil):
        for half in range(COL_TILE // LANES):
            lane0 = cls * COL_TILE + half * LANES
            o_ref[:, lane0:lane0 + LANES] = scr[half, pl.ds(cls, rows, stride=dil), :]


def _from_class_view(ref, scr, dil):
    rows = ref.shape[0]
    for cls in range(dil):
        for half in range(COL_TILE // LANES):
            lane0 = cls * COL_TILE + half * LANES
            scr[half, pl.ds(cls, rows, stride=dil), :] = ref[:, lane0:lane0 + LANES]
    return jnp.concatenate([scr[half] for half in range(COL_TILE // LANES)], axis=1)


def _class_view_scratch(tm):
    return pltpu.VMEM((COL_TILE // LANES, tm, LANES), F32)


def _norm_proj_kernel(x_ref, cos_ref, sin_ref, *refs, plan):
    n_proj = len(plan)
    n_outputs = sum(len(dils) for _, dils in plan)
    out_refs = list(refs[2 * n_proj:2 * n_proj + n_outputs])
    scratch = list(refs[2 * n_proj + n_outputs:])
    x = x_ref[...]
    inv_rms = lax.rsqrt(jnp.mean(x * x, axis=-1, keepdims=True) + NORM_EPS)
    for p, (n_rot, dils) in enumerate(plan):
        g_ref, w_ref = refs[2 * p], refs[2 * p + 1]
        xn = (x * inv_rms * g_ref[...]).astype(BF16)
        width = w_ref.shape[1] // len(dils)
        for j, dil in enumerate(dils):
            o_ref = out_refs.pop(0)
            y = jnp.dot(xn, w_ref[:, j * width:(j + 1) * width], preferred_element_type=F32)
            if j < n_rot:
                y = _rotate_heads(y, cos_ref[...], sin_ref[...])
            if dil == 1:
                o_ref[...] = y
            else:
                _to_class_view(y, o_ref, scratch.pop(0), dil)


def _resident(shape):
    return pl.BlockSpec(shape, lambda *_: (0,) * len(shape), pipeline_mode=pl.Buffered(1))


def _layer_resident(stacked, layer):
    shape = stacked.shape[1:]
    return pl.BlockSpec((None,) + shape, lambda *_: (layer,) + (0,) * len(shape),
                        pipeline_mode=pl.Buffered(1))


def _norm_proj(x, projections, cos, sin_signed, seq_len, *, tm_want=1024):
    n, d = x.shape
    plan = tuple((n_rot, tuple(dils)) for _, _, n_rot, dils in projections)
    widths = [w.shape[1] // len(dils) for _, w, _, dils in projections for _ in dils]
    all_dils = [dil for _, dils in plan for dil in dils]
    assert all(width == COL_TILE or dil == 1 for width, dil in zip(widths, all_dils))
    tm = _row_tile(n, tm_want)
    if tm <= seq_len:
        assert seq_len % tm == 0
        per_seq = seq_len // tm
        tab_map = lambda i: (i % per_seq, 0)
    else:
        assert tm % seq_len == 0
        cos = jnp.tile(cos, (tm // seq_len, 1))
        sin_signed = jnp.tile(sin_signed, (tm // seq_len, 1))
        tab_map = lambda i: (0, 0)
    assert all(tm % (8 * dil) == 0 or dil == 1 for dil in all_dils)
    weights, weight_specs = [], []
    for g, w_bf16, _, _ in projections:
        weights += [g.reshape(1, d), w_bf16]
        weight_specs += [_resident((1, d)), _resident(w_bf16.shape)]
    outs = pl.pallas_call(
        functools.partial(_norm_proj_kernel, plan=plan),
        grid=(n // tm,),
        in_specs=[pl.BlockSpec((tm, d), lambda i: (i, 0)),
                  pl.BlockSpec((tm, LANES), tab_map),
                  pl.BlockSpec((tm, LANES), tab_map)] + weight_specs,
        out_specs=[pl.BlockSpec((tm // dil, dil * width), lambda i: (i, 0))
                   for width, dil in zip(widths, all_dils)],
        out_shape=[jax.ShapeDtypeStruct((n // dil, dil * width), F32) for width, dil in zip(widths, all_dils)],
        scratch_shapes=[_class_view_scratch(tm) for dil in all_dils if dil > 1],
        compiler_params=_cparams("parallel"),
        name="norm_proj",
    )(x, cos, sin_signed, *weights)
    outs = list(outs)
    return [[outs.pop(0) for _ in dils] for _, dils in plan]


def _rwkv_proj_kernel(*refs, mix, has_vfirst, tm, seq_len, tail):
    it = iter(refs)
    h_ref = next(it)
    hp_ref = next(it) if seq_len >= tm else None
    sh_ref = next(it)
    g_ref, mu_ref, win_ref = next(it), next(it), next(it)
    w0_ref, w1_ref, w2_ref = next(it), next(it), next(it)
    a0_ref, a1_ref, a2_ref = next(it), next(it), next(it)
    g1_ref, g2_ref = next(it), next(it)
    if has_vfirst:
        vf_ref, v0_ref, v1_ref, v2_ref = next(it), next(it), next(it), next(it)
    r_ref, lw_ref, k_ref, v_ref, a_ref, gate_ref, qm_ref, tail_ref = (next(it) for _ in range(8))

    i = pl.program_id(0)
    gain = g_ref[...]
    xn = _rms(h_ref[...], gain)
    row = lax.broadcasted_iota(jnp.int32, (tm, 1), 0)
    rolled = pltpu.roll(xn, 1, axis=0)
    if seq_len >= tm:
        prev_tile_last = _rms(hp_ref[...], gain)[7:8, :]
        at_seq_start = (i * tm) % seq_len == 0
        first = jnp.where(at_seq_start, sh_ref[0][7:8, :], prev_tile_last)
        x_prev = jnp.where(row == 0, first, rolled)
    else:
        x_prev = jnp.where(row % seq_len == 0, sh_ref[...], rolled)
    xx = x_prev - xn
    mu = mu_ref[...]

    def mixed(idx):
        return (xn + xx * mu[idx:idx + 1, :]).astype(BF16)

    qm_ref[...] = jnp.dot(xn.astype(BF16), win_ref[:, 3 * mix:], preferred_element_type=F32)
    tail_ref[0] = xn[tm - tail:, :]
    r_ref[...] = jnp.dot(mixed(0), win_ref[:, 0:mix], preferred_element_type=F32)
    w_lin = w0_ref[...] + _mm(jnp.tanh(_mm(mixed(1), w1_ref[...])), w2_ref[...])
    lw_ref[...] = -math.exp(-0.5) * _sigmoid(w_lin)
    k_ref[...] = jnp.dot(mixed(2), win_ref[:, mix:2 * mix], preferred_element_type=F32)
    xv = mixed(3)
    v = jnp.dot(xv, win_ref[:, 2 * mix:3 * mix], preferred_element_type=F32)
    if has_vfirst:
        v = v + (vf_ref[...] - v) * _sigmoid(v0_ref[...] + _mm(_mm(xv, v1_ref[...]), v2_ref[...]))
    v_ref[...] = v
    a_ref[...] = _sigmoid(a0_ref[...] + _mm(_mm(mixed(4), a1_ref[...]), a2_ref[...]))
    gate_ref[...] = _mm(_sigmoid(_mm(mixed(5), g1_ref[...])), g2_ref[...])


def _rwkv_proj(h, shift, seq_len, W, li, v_first, *, tm_want=512):
    n, d = h.shape
    mix = W['a_w0'].shape[1]
    tm = _row_tile(n, tm_want)
    n_tiles = n // tm
    has_vfirst = v_first is not None
    row_spec = lambda w: pl.BlockSpec((tm, w), lambda i: (i, 0))
    full = lambda a: _resident(a.shape)

    args, specs = [h], [row_spec(d)]
    if seq_len >= tm:
        assert seq_len % tm == 0
        tail = 8
        args.append(h)
        specs.append(pl.BlockSpec((8, d), lambda i: (jnp.maximum(i * (tm // 8) - 1, 0), 0)))
        args.append(jnp.broadcast_to(shift[:, None, :], (shift.shape[0], 8, d)))
        specs.append(pl.BlockSpec((1, 8, d), lambda i: ((i * tm) // seq_len, 0, 0)))
    else:
        assert tm % seq_len == 0
        tail = tm
        args.append(jnp.repeat(shift, seq_len, axis=0))
        specs.append(row_spec(d))
    lead = [W['g_pre_mix'][li].reshape(1, d), W['a_mu'][li]]
    small = [W['a_w0'][li].reshape(1, mix), W['a_w1_bf16'][li], W['a_w2_bf16'][li],
             W['a_a0'][li].reshape(1, mix), W['a_a1_bf16'][li], W['a_a2_bf16'][li],
             W['a_g1_bf16'][li], W['a_g2_bf16'][li]]
    args += lead + [W['a_w_in_bf16']] + small
    specs += [full(a) for a in lead] + [_layer_resident(W['a_w_in_bf16'], li)] + [full(a) for a in small]
    if has_vfirst:
        vi = li - 1
        extra = [W['a_v0'][vi].reshape(1, mix), W['a_v1_bf16'][vi], W['a_v2_bf16'][vi]]
        args += [v_first] + extra
        specs += [row_spec(mix)] + [full(a) for a in extra]

    wide = jax.ShapeDtypeStruct((n, mix), F32)
    out_shape = [wide] * 6 + [jax.ShapeDtypeStruct((n, MEM_WIDTH), F32),
                              jax.ShapeDtypeStruct((n_tiles, tail, d), F32)]
    out_specs = [row_spec(mix)] * 6 + [row_spec(MEM_WIDTH),
                                       pl.BlockSpec((1, tail, d), lambda i: (i, 0, 0))]
    r, lw, k, v, a, gate, q_mem, xn_tail = pl.pallas_call(
        functools.partial(_rwkv_proj_kernel, mix=mix, has_vfirst=has_vfirst, tm=tm,
                          seq_len=seq_len, tail=tail),
        grid=(n_tiles,),
        in_specs=specs,
        out_specs=out_specs,
        out_shape=out_shape,
        compiler_params=_cparams("parallel"),
        name="rwkv_proj",
    )(*args)
    xn_rows = xn_tail.reshape(n_tiles * tail, d)
    last = xn_rows.reshape(-1, seq_len if seq_len < tm else tail * (seq_len // tm), d)[:, -1]
    return r, lw, k, v, a, gate, q_mem, last


def _head_sums(x, ones_bd):
    hi = x.astype(BF16)
    lo = (x - hi.astype(F32)).astype(BF16)
    tiles = []
    for t in range(x.shape[1] // COL_TILE):
        s = slice(t * COL_TILE, (t + 1) * COL_TILE)
        tiles.append(jnp.dot(hi[:, s], ones_bd, preferred_element_type=F32)
                     + jnp.dot(lo[:, s], ones_bd, preferred_element_type=F32))
    return jnp.concatenate(tiles, axis=1)


def _wkv_kernel(r_ref, lw_ref, k_ref, v_ref, a_ref, gate_ref, s0_ref, kkp_ref, kap_ref, rkp_ref,
                lnw_ref, lnb_ref, o_ref, sout_ref, s_scr, *, chunk, heads):
    c = pl.program_id(1)
    rows = r_ref.shape[0]
    mix = r_ref.shape[2]

    @pl.when(c == 0)
    def _():
        for b in range(rows):
            s_scr[b * heads:(b + 1) * heads] = s0_ref[b]

    side_by_side = lambda ref: jnp.concatenate([ref[b] for b in range(rows)], axis=1)
    per_row = lambda ref: jnp.concatenate([ref[...]] * rows, axis=1)
    r, lw, k, v, a = (side_by_side(ref) for ref in (r_ref, lw_ref, k_ref, v_ref, a_ref))
    row = lax.broadcasted_iota(jnp.int32, (chunk, chunk), 0)
    col = lax.broadcasted_iota(jnp.int32, (chunk, chunk), 1)
    strict = col < row
    incl = col <= row
    tri = incl.astype(BF16)
    lw_hi = lw.astype(BF16)
    lw_rest = lw - lw_hi.astype(F32)
    lw_mid = lw_rest.astype(BF16)
    lw_lo = (lw_rest - lw_mid.astype(F32)).astype(BF16)
    c_inc = (jnp.dot(tri, lw_hi, preferred_element_type=F32)
             + jnp.dot(tri, lw_mid, preferred_element_type=F32)
             + jnp.dot(tri, lw_lo, preferred_element_type=F32))
    c_last = c_inc[chunk - 1:chunk, :]
    e_inc = jnp.exp(c_inc)
    e_exc = jnp.exp(c_inc - lw)
    e_neg = jnp.exp(-c_inc)
    e_tail = jnp.exp(c_last - c_inc)
    p_last = jnp.exp(c_last)

    bd_r = lax.broadcasted_iota(jnp.int32, (COL_TILE, COL_TILE), 0) // HEAD_DIM
    bd_c = lax.broadcasted_iota(jnp.int32, (COL_TILE, COL_TILE), 1) // HEAD_DIM
    ones_bd = (bd_r == bd_c).astype(BF16)

    kk_raw = k * per_row(kkp_ref)
    kk = kk_raw * (1.0 / jnp.maximum(jnp.sqrt(_head_sums(kk_raw * kk_raw, ones_bd)), KK_EPS))
    kka = kk * a
    k2 = k * (1.0 + (a - 1.0) * per_row(kap_ref))
    bonus = _head_sums(r * k2 * per_row(rkp_ref), ones_bd) * v
    a_hat_f = -kk * e_exc
    r_hat_f = r * e_inc
    b_chk_f = kka * e_neg
    k_chk_f = k2 * e_neg
    b_til_f = kka * e_tail
    k_til_f = k2 * e_tail

    eye_h = (lax.broadcasted_iota(jnp.int32, (HEAD_DIM, HEAD_DIM), 0)
             == lax.broadcasted_iota(jnp.int32, (HEAD_DIM, HEAD_DIM), 1))
    n_levels = max(1, int(math.log2(chunk)))
    hs = range(rows * heads)
    sl = [slice(h * HEAD_DIM, (h + 1) * HEAD_DIM) for h in hs]
    a_hat = [a_hat_f[:, s] for s in sl]
    r_hat = [r_hat_f[:, s] for s in sl]
    v_h = [v[:, s] for s in sl]

    amat = [_mm_nt(jnp.concatenate([a_hat[h], r_hat[h]], axis=0),
                   jnp.concatenate([b_chk_f[:, sl[h]], k_chk_f[:, sl[h]]], axis=0)) for h in hs]
    a_ab = [jnp.where(strict, m[:chunk, :chunk], 0.0) for m in amat]
    a_ak = [jnp.where(strict, m[:chunk, chunk:], 0.0) for m in amat]
    a_rb = [jnp.where(incl, m[chunk:, :chunk], 0.0) for m in amat]
    a_rk = [jnp.where(incl, m[chunk:, chunk:], 0.0) for m in amat]
    av = [_mm(jnp.concatenate([a_ak[h], a_rk[h]], axis=0), v_h[h]) for h in hs]

    w12 = [jnp.concatenate([a_hat[h], av[h][:chunk]], axis=1) for h in hs]
    power = a_ab
    for level in range(n_levels):
        w12 = [w12[h] + _mm(power[h], w12[h]) for h in hs]
        if level + 1 < n_levels:
            power = [_mm(p, p) for p in power]

    rx = [_mm(a_rb[h], w12[h]) + jnp.concatenate([r_hat[h], av[h][chunk:]], axis=1) for h in hs]
    gh = [_mm_tn(w12[h], b_til_f[:, sl[h]]) for h in hs]
    vk = [_mm_tn(v_h[h], k_til_f[:, sl[h]]) for h in hs]
    g_t = [gh[h][:HEAD_DIM] + jnp.where(eye_h, p_last[:, sl[h]], 0.0) for h in hs]
    h_t = [gh[h][HEAD_DIM:] + vk[h] for h in hs]

    s_prev = [s_scr[h] for h in hs]
    o_h = [_mm_nt(rx[h][:, :HEAD_DIM], s_prev[h]) + rx[h][:, HEAD_DIM:] for h in hs]
    s_new = [_mm(s_prev[h], g_t[h]) + h_t[h] for h in hs]
    for h in hs:
        s_scr[h] = s_new[h]

    o = jnp.concatenate(o_h, axis=1)
    cen = o - _head_sums(o, ones_bd) * (1.0 / HEAD_DIM)
    var = _head_sums(cen * cen, ones_bd) * (1.0 / HEAD_DIM)
    o = cen * lax.rsqrt(var + LNX_EPS) * per_row(lnw_ref) + per_row(lnb_ref)
    o = (o + bonus) * side_by_side(gate_ref)
    for b in range(rows):
        o_ref[b] = o[:, b * mix:(b + 1) * mix]

    @pl.when(c == pl.num_programs(1) - 1)
    def _():
        for b in range(rows):
            sout_ref[b] = s_scr[b * heads:(b + 1) * heads]


def _wkv(r, lw, k, v, a, gate, s0_layers, W, li, batch, seq_len, chunk):
    mix = r.shape[1]
    heads = mix // HEAD_DIM
    if seq_len % chunk:
        raise ValueError("sequence length must be a multiple of the chunk")
    if mix % COL_TILE:
        raise ValueError("mixer width must be a whole number of 4-head tiles")
    n_chunks = seq_len // chunk
    seq = lambda t: t.reshape(batch, seq_len, mix)
    rows = max(n for n in (4, 2, 1) if batch % n == 0)
    blk = pl.BlockSpec((rows, chunk, mix), lambda b, c: (b, c, 0))
    st = pl.BlockSpec((rows, heads, HEAD_DIM, HEAD_DIM), lambda b, c: (b, 0, 0, 0))
    st_in = pl.BlockSpec((None, rows, heads, HEAD_DIM, HEAD_DIM), lambda b, c: (li, b, 0, 0, 0))
    par = pl.BlockSpec((1, mix), lambda b, c: (0, 0))
    params = [W['a_k_k'][li], W['a_k_a'][li], W['a_r_k'][li], W['a_lnx_w'][li], W['a_lnx_b'][li]]
    o, s_out = pl.pallas_call(
        functools.partial(_wkv_kernel, chunk=chunk, heads=heads),
        grid=(batch // rows, n_chunks),
        in_specs=[blk] * 6 + [st_in] + [par] * 5,
        out_specs=[blk, st],
        out_shape=[jax.ShapeDtypeStruct((batch, seq_len, mix), F32),
                   jax.ShapeDtypeStruct(s0_layers.shape[1:], F32)],
        scratch_shapes=[pltpu.VMEM((rows * heads, HEAD_DIM, HEAD_DIM), F32)],
        compiler_params=_cparams("parallel", "arbitrary"),
        name="wkv7",
    )(seq(r), seq(lw), seq(k), seq(v), seq(a), seq(gate), s0_layers, *[p.reshape(1, mix) for p in params])
    return o.reshape(batch * seq_len, mix), s_out


def _mem_attn_kernel(q_ref, kv_ref, o_ref):
    masks = _head_lane_masks(MEM_WIDTH)
    for b in range(q_ref.shape[0]):
        q = q_ref[b]
        tq = q.shape[0]
        mk_t = kv_ref[b, :MEM_WIDTH, :].astype(BF16)
        mv_t = kv_ref[b, MEM_WIDTH:, :].astype(BF16)
        stacked = jnp.concatenate([jnp.where(m, q, 0.0) for m in masks], axis=0)
        s = _mm(stacked, mk_t) * ATTN_SCALE
        p = jnp.exp(s - jnp.max(s, axis=-1, keepdims=True))
        p = p * (1.0 / jnp.sum(p, axis=-1, keepdims=True))
        pv = _mm_nt(p, mv_t)
        out = jnp.zeros(q.shape, F32)
        for h, m in enumerate(masks):
            out = out + jnp.where(m, pv[h * tq:(h + 1) * tq, :], 0.0)
        o_ref[b] = out


def _mem_attn(q_mem, mem_kv, layer, batch, seq_len, *, tq_want=1024):
    tq = _row_tile(seq_len, tq_want)
    m = mem_kv.shape[3]
    nb = max(n for n in (8, 4, 2, 1) if batch % n == 0 and n * tq <= tq_want)
    out = pl.pallas_call(
        _mem_attn_kernel,
        grid=(batch // nb, seq_len // tq),
        in_specs=[pl.BlockSpec((nb, tq, MEM_WIDTH), lambda b, t: (b, t, 0)),
                  pl.BlockSpec((None, nb, 2 * MEM_WIDTH, m), lambda b, t: (layer, b, 0, 0))],
        out_specs=pl.BlockSpec((nb, tq, MEM_WIDTH), lambda b, t: (b, t, 0)),
        out_shape=jax.ShapeDtypeStruct((batch, seq_len, MEM_WIDTH), F32),
        compiler_params=_cparams("parallel", "parallel"),
        name="mem_attn",
    )(q_mem.reshape(batch, seq_len, MEM_WIDTH), mem_kv)
    return out.reshape(batch * seq_len, MEM_WIDTH)


def _out_ffn_kernel(*refs, dils, mix_width, d_ff, tf):
    n_groups = len(dils)
    n_scratch = 2 * sum(dil > 1 for dil in dils)
    scratch = list(refs[len(refs) - n_scratch:])
    it = iter(refs[:len(refs) - n_scratch])
    if n_groups:
        read = lambda ref, dil: ref[...] if dil == 1 else _from_class_view(ref, scratch.pop(0), dil)
        og = [read(next(it), dil) for dil in dils]
        lse = [read(next(it), dil) for dil in dils]
        top = functools.reduce(jnp.maximum, lse)
        wgt = [jnp.exp(s - top) for s in lse]
        o_mix = sum(w * o for w, o in zip(wgt, og)) / sum(wgt)
    else:
        o_mix = next(it)[...]
    om_ref, wout_ref, gpm_ref, h_ref, gpre_ref, win_ref, wo_ref, gpost_ref, o_ref, acc_ref = it
    mixed = (jnp.dot(o_mix.astype(BF16), wout_ref[:mix_width, :], preferred_element_type=F32)
             + jnp.dot(om_ref[...].astype(BF16), wout_ref[mix_width:, :], preferred_element_type=F32))
    h1 = h_ref[...] + _rms(mixed, gpm_ref[...])
    xn = _rms(h1, gpre_ref[...]).astype(BF16)
    for j in range(d_ff // tf):
        gate = jnp.dot(xn, win_ref[:, j * tf:(j + 1) * tf], preferred_element_type=F32)
        up = jnp.dot(xn, win_ref[:, d_ff + j * tf:d_ff + (j + 1) * tf], preferred_element_type=F32)
        act = (gate * _sigmoid(gate) * up).astype(BF16)
        part = jnp.dot(act, wo_ref[j * tf:(j + 1) * tf, :], preferred_element_type=F32)
        if j == 0:
            acc_ref[...] = part
        else:
            acc_ref[...] += part
    o_ref[...] = h1 + _rms(acc_ref[...], gpost_ref[...])


def _out_ffn(o_mix, lse, dils, o_mem, h, W, l, w_out_stack, w_out_layer, *, tm_want=512, tf=256):
    n, d = h.shape
    tm = _row_tile(n, tm_want)
    w_in, w_o = W['w_ffn_in_bf16'], W['w_ffn_out_bf16']
    d_ff = w_o.shape[1]
    assert d_ff % tf == 0
    mix_width = w_out_stack.shape[1] - MEM_WIDTH
    row = lambda w: pl.BlockSpec((tm, w), lambda i: (i, 0))
    if lse is not None:
        lead = list(o_mix) + list(lse)
        lead_specs = [pl.BlockSpec((tm // dil, dil * COL_TILE), lambda i: (i, 0)) for dil in dils] * 2
    else:
        lead, lead_specs, dils = [o_mix], [row(o_mix.shape[1])], ()
    vec = lambda g: g.reshape(1, d)
    return pl.pallas_call(
        functools.partial(_out_ffn_kernel, dils=tuple(dils), mix_width=mix_width, d_ff=d_ff, tf=tf),
        grid=(n // tm,),
        in_specs=lead_specs + [
            row(MEM_WIDTH), _layer_resident(w_out_stack, w_out_layer), _resident((1, d)), row(d),
            _resident((1, d)), _layer_resident(w_in, l), _layer_resident(w_o, l), _resident((1, d))],
        out_specs=row(d),
        out_shape=jax.ShapeDtypeStruct((n, d), F32),
        scratch_shapes=[pltpu.VMEM((tm, d), F32)] + [_class_view_scratch(tm) for dil in dils if dil > 1] * 2,
        compiler_params=_cparams("parallel"),
        name="out_ffn",
    )(*lead, o_mem, w_out_stack, vec(W['g_post_mix'][l]), h, vec(W['g_pre_ffn'][l]), w_in, w_o,
      vec(W['g_post_ffn'][l]))


def _dil_prompt_kernel(q_ref, kc_ref, kh_ref, vc_ref, vh_ref, o_ref, lse_ref, kx_ref, vx_ref, *, tile, n_cls):
    t = pl.program_id(2)
    rows = HEADS_PER_TILE * BAND
    qi = lax.broadcasted_iota(jnp.int32, (rows, 2 * BAND), 0) % BAND
    kj = lax.broadcasted_iota(jnp.int32, (rows, 2 * BAND), 1)
    band = (kj >= qi) & (kj <= qi + BAND)
    band_first = band & (kj >= jnp.where(t == 0, BAND, 0))
    masks = _head_lane_masks(COL_TILE)

    for ci in range(n_cls):
        lanes = slice(ci * COL_TILE, (ci + 1) * COL_TILE)
        kx_ref[ci, 0:BAND, :] = kh_ref[0, :, lanes].astype(BF16)
        kx_ref[ci, BAND:, :] = kc_ref[0, :, lanes].astype(BF16)
        vx_ref[ci, 0:BAND, :] = vh_ref[0, :, lanes].astype(BF16)
        vx_ref[ci, BAND:, :] = vc_ref[0, :, lanes].astype(BF16)
        for blk in range(tile // BAND):
            start = blk * BAND
            q = q_ref[0, start:start + BAND, lanes] * ATTN_SCALE
            keys = kx_ref[ci, start:start + 2 * BAND, :]
            vals = vx_ref[ci, start:start + 2 * BAND, :]
            stacked = jnp.concatenate([jnp.where(m, q, 0.0) for m in masks], axis=0)
            s = jnp.where(band_first if blk == 0 else band, _mm_nt(stacked, keys), -jnp.inf)
            top = jnp.max(s, axis=-1, keepdims=True)
            p = jnp.exp(s - top)
            den = jnp.sum(p, axis=-1, keepdims=True)
            pv = _mm(p, vals) * (1.0 / den)
            lse_rows = top + jnp.log(den)
            out = jnp.zeros((BAND, COL_TILE), F32)
            lse = jnp.zeros((BAND, COL_TILE), F32)
            for h, m in enumerate(masks):
                out = out + jnp.where(m, pv[h * BAND:(h + 1) * BAND, :], 0.0)
                lse = lse + jnp.where(m, lse_rows[h * BAND:(h + 1) * BAND, :], 0.0)
            o_ref[0, start:start + BAND, lanes] = out
            lse_ref[0, start:start + BAND, lanes] = lse


DIL_BLOCKS_PER_STEP = 16


def _dil_prompt_group(q, k, v, batch, seq_len, dil):
    cls_len = seq_len // dil
    tile = min(cls_len, DIL_BLOCKS_PER_STEP * BAND)
    assert cls_len % tile == 0 and tile % BAND == 0
    n_t = cls_len // tile
    n_cls = math.gcd(dil, DIL_BLOCKS_PER_STEP * BAND // tile)
    width = n_cls * COL_TILE
    view = lambda x: x.reshape(batch, cls_len, dil * COL_TILE)
    cur = pl.BlockSpec((1, tile, width), lambda b, c, t: (b, t, c))
    halo = pl.BlockSpec((1, BAND, width),
                        lambda b, c, t: (b, jnp.maximum(t * (tile // BAND) - 1, 0), c))
    out_shape = jax.ShapeDtypeStruct((batch, cls_len, dil * COL_TILE), F32)
    o, lse = pl.pallas_call(
        functools.partial(_dil_prompt_kernel, tile=tile, n_cls=n_cls),
        grid=(batch, dil // n_cls, n_t),
        in_specs=[cur, cur, halo, cur, halo],
        out_specs=[cur, cur],
        out_shape=[out_shape, out_shape],
        scratch_shapes=[pltpu.VMEM((n_cls, tile + BAND, COL_TILE), BF16)] * 2,
        compiler_params=_cparams("parallel", "parallel", "arbitrary"),
        name="dil_attn_prompt",
    )(view(q), view(k), view(k), view(v), view(v))
    flat = lambda x: x.reshape(batch * cls_len, dil * COL_TILE)
    return flat(o), flat(lse)


def _dil_sample_kernel(*refs, seq_len, write_cache):
    q_refs = refs[:N_GROUPS]
    k_refs = refs[N_GROUPS:2 * N_GROUPS]
    v_refs = refs[2 * N_GROUPS:3 * N_GROUPS]
    cache_refs = refs[3 * N_GROUPS:4 * N_GROUPS]
    n_in = (6 if write_cache else 4) * N_GROUPS
    kt_refs, vt_refs = refs[4 * N_GROUPS:5 * N_GROUPS], refs[5 * N_GROUPS:n_in]
    o_ref = refs[n_in]
    new_refs = refs[n_in + 1:]
    masks = _head_lane_masks(COL_TILE)
    rows = HEADS_PER_TILE * seq_len

    run_top = jnp.full((rows, 1), -jnp.inf, F32)
    run_den = jnp.zeros((rows, 1), F32)
    run_acc = jnp.zeros((rows, COL_TILE), F32)
    for g, (win, dil) in enumerate(DIL_GROUPS):
        cache = cache_refs[g]
        length = cache.shape[2]
        q = q_refs[g][...] * ATTN_SCALE
        stacked = jnp.concatenate([jnp.where(m, q, 0.0) for m in masks], axis=0)
        k_new, v_new = k_refs[g][...], v_refs[g][...]
        ck_t = cache[0, :COL_TILE, :]
        cv_t = cache[0, COL_TILE:, :]
        s_c = _mm(stacked, ck_t)
        s_n = _mm_nt(stacked, k_new)
        tq_c = lax.broadcasted_iota(jnp.int32, s_c.shape, 0) % seq_len
        dist_c = length + tq_c - lax.broadcasted_iota(jnp.int32, s_c.shape, 1)
        ok_c = (dist_c % dil == 0) & (dist_c <= win)
        tq_n = lax.broadcasted_iota(jnp.int32, s_n.shape, 0) % seq_len
        dist_n = tq_n - lax.broadcasted_iota(jnp.int32, s_n.shape, 1)
        ok_n = (dist_n >= 0) & (dist_n % dil == 0)
        s_c = jnp.where(ok_c, s_c, -jnp.inf)
        s_n = jnp.where(ok_n, s_n, -jnp.inf)
        top = jnp.maximum(jnp.maximum(jnp.max(s_c, axis=-1, keepdims=True),
                                      jnp.max(s_n, axis=-1, keepdims=True)), run_top)
        p_c = jnp.exp(s_c - top)
        p_n = jnp.exp(s_n - top)
        scale = jnp.exp(run_top - top)
        run_den = run_den * scale + jnp.sum(p_c, axis=-1, keepdims=True) + jnp.sum(p_n, axis=-1, keepdims=True)
        run_acc = run_acc * scale + _mm_nt(p_c, cv_t) + _mm(p_n, v_new)
        run_top = top
        if write_cache:
            new = new_refs[g]
            if length > seq_len:
                new[0] = pltpu.roll(cache[0], length - seq_len, axis=1)
            new[0, :COL_TILE, length - seq_len:] = kt_refs[g][0]
            new[0, COL_TILE:, length - seq_len:] = vt_refs[g][0]
    res = run_acc / run_den
    out = jnp.zeros((seq_len, COL_TILE), F32)
    for h, m in enumerate(masks):
        out = out + jnp.where(m, res[h * seq_len:(h + 1) * seq_len, :], 0.0)
    o_ref[...] = out


def _dil_sample(q_slabs, kv_slabs, caches_t, batch, seq_len, write_cache):
    n = batch * seq_len
    in_specs = [pl.BlockSpec((seq_len, COL_TILE), lambda b: (b, 0))] * (3 * N_GROUPS)
    cache_specs = [pl.BlockSpec((1,) + c.shape[1:], lambda b: (b, 0, 0)) for c in caches_t]
    out_shape = [jax.ShapeDtypeStruct((n, COL_TILE), F32)]
    out_specs = [pl.BlockSpec((seq_len, COL_TILE), lambda b: (b, 0))]
    new_rows_t, new_specs = [], []
    if write_cache:
        out_shape += [jax.ShapeDtypeStruct(c.shape, F32) for c in caches_t]
        out_specs += cache_specs
        new_rows_t = [jnp.swapaxes(s.reshape(batch, seq_len, COL_TILE), 1, 2) for s in kv_slabs]
        new_specs = [pl.BlockSpec((1, COL_TILE, seq_len), lambda b: (b, 0, 0))] * len(kv_slabs)
    res = pl.pallas_call(
        functools.partial(_dil_sample_kernel, seq_len=seq_len, write_cache=write_cache),
        grid=(batch,),
        in_specs=in_specs + cache_specs + new_specs,
        out_specs=out_specs,
        out_shape=out_shape,
        compiler_params=_cparams("parallel"),
        name="dil_attn_sample",
    )(*q_slabs[:N_GROUPS], *kv_slabs, *caches_t, *new_rows_t)
    return res[0], list(res[1:])


def _rope_tables(pos):
    half = HEAD_DIM // 2
    inv = ROPE_THETA ** (-jnp.arange(half, dtype=F32) / half)
    ang = pos.astype(F32)[:, None] * inv[None, :]
    cos, sin = jnp.cos(ang), jnp.sin(ang)
    cos = jnp.tile(jnp.concatenate([cos, cos], axis=-1), (1, LANES // HEAD_DIM))
    sin_signed = jnp.tile(jnp.concatenate([-sin, sin], axis=-1), (1, LANES // HEAD_DIM))
    return cos, sin_signed


def _trunk(h, pos, mem_kv_all, wkv0, shift0, caches, W, batch, seq_len, chunk):
    depth = W['g_pre_mix'].shape[0]
    n_a = W['a_mu'].shape[0]
    cos, sin_signed = _rope_tables(pos)
    new_wkv, new_shift, new_caches = [], [], None
    v_first = None
    kv_tiles = None
    for l in range(depth):
        if l < n_a:
            r, lw, k, v, a, gate, q_mem, xn_last = _rwkv_proj(h, shift0[l], seq_len, W, l, v_first)
            if l == 0:
                v_first = v
            o_mix, s_out = _wkv(r, lw, k, v, a, gate, wkv0, W, l, batch, seq_len, chunk)
            new_wkv.append(s_out)
            new_shift.append(xn_last)
            lse = None
            w_out, w_out_layer = W['a_w_out_bf16'], l
        else:
            j = l - n_a
            dils = tuple(dil for _, dil in DIL_GROUPS) if caches is None else (1,) * N_GROUPS
            q_proj = (W['g_pre_mix'][l], W['b_w_in_bf16'][j], N_GROUPS, dils + (1,))
            if kv_tiles is None:
                kv_proj = (W['g_kv'], W['w_kv_bf16'], N_GROUPS, dils * 2)
                kv_tiles, proj = _norm_proj(h, [kv_proj, q_proj], cos, sin_signed, seq_len)
            else:
                proj, = _norm_proj(h, [q_proj], cos, sin_signed, seq_len)
            q_mem = proj[N_GROUPS]
            if caches is None:
                outs = [_dil_prompt_group(proj[g], kv_tiles[g], kv_tiles[N_GROUPS + g], batch, seq_len, dil)
                        for g, dil in enumerate(dils)]
                o_mix = [o for o, _ in outs]
                lse = [s for _, s in outs]
            else:
                o_mix, written = _dil_sample(proj, kv_tiles, caches, batch, seq_len, new_caches is None)
                if new_caches is None:
                    new_caches = written
                lse = None
            w_out, w_out_layer = W['b_w_out_bf16'], j
        o_mem = _mem_attn(q_mem, mem_kv_all, l, batch, seq_len)
        h = _out_ffn(o_mix, lse, dils if lse is not None else (), o_mem, h, W, l, w_out, w_out_layer)
    return h, jnp.stack(new_wkv), jnp.stack(new_shift), kv_tiles, new_caches


def _prompt_windows(kv_tiles, batch, seq_len):
    bufs = []
    for g, (win, dil) in enumerate(DIL_GROUPS):
        length = min(win, PAST_LEN)
        assert seq_len >= length and length % dil == 0
        rows = seq_len // dil

        def tail(t):
            t = t.reshape(batch, rows, dil * COL_TILE)[:, rows - length // dil:]
            return t.reshape(batch, length, HEADS_PER_TILE, HEAD_DIM)

        bufs.append(jnp.stack([tail(kv_tiles[g]), tail(kv_tiles[N_GROUPS + g])], axis=2))
    return bufs


def kernel(x_prompt, x_sample, state_wkv, state_shift, cache_win_g0, cache_win_g1, cache_win_g2, cache_mem, mem_prompt, g_pre_mix, g_post_mix, g_pre_ffn, g_post_ffn, g_mem, w_mem_kv, w_ffn_in, w_ffn_out, a_mu, a_w_in, a_w0, a_w1, a_w2, a_a0, a_a1, a_a2, a_v0, a_v1, a_v2, a_g1, a_g2, a_k_k, a_k_a, a_r_k, a_lnx_w, a_lnx_b, a_w_out, g_kv, w_kv, b_w_in, b_w_out):
    W = dict(g_pre_mix=g_pre_mix, g_post_mix=g_post_mix, g_pre_ffn=g_pre_ffn, g_post_ffn=g_post_ffn,
             a_mu=a_mu, a_w0=a_w0, a_a0=a_a0, a_v0=a_v0, a_k_k=a_k_k, a_k_a=a_k_a,
             a_r_k=a_r_k.reshape(a_r_k.shape[0], -1), a_lnx_w=a_lnx_w, a_lnx_b=a_lnx_b, g_kv=g_kv)
    for name, w in dict(w_ffn_in=w_ffn_in, w_ffn_out=w_ffn_out, a_w_in=a_w_in, a_w1=a_w1, a_w2=a_w2,
                        a_a1=a_a1, a_a2=a_a2, a_v1=a_v1, a_v2=a_v2, a_g1=a_g1, a_g2=a_g2,
                        a_w_out=a_w_out, w_kv=w_kv, b_w_in=b_w_in, b_w_out=b_w_out,
                        w_mem_kv=w_mem_kv).items():
        W[name + '_bf16'] = w.astype(BF16)

    bp, tp, d = x_prompt.shape
    bs, ts, _ = x_sample.shape
    depth = g_pre_mix.shape[0]
    n_a = a_mu.shape[0]
    heads = a_w0.shape[1] // HEAD_DIM
    m_tok = mem_prompt.shape[1]

    mem_rows = mem_prompt.reshape(bp * m_tok, d)
    no_tab = jnp.zeros((m_tok, LANES), F32)
    mem_kv_p = _norm_proj(mem_rows, [(g_mem[l], W['w_mem_kv_bf16'][l], 0, (1,)) for l in range(depth)],
                          no_tab, no_tab, m_tok)
    mem_kv_p = jnp.stack([slabs[0].reshape(bp, m_tok, 2 * MEM_WIDTH) for slabs in mem_kv_p])
    mem_kv_p_t = jnp.swapaxes(mem_kv_p, 2, 3)
    wkv_zero = jnp.zeros((n_a, bp, heads, HEAD_DIM, HEAD_DIM), F32)
    shift_zero = jnp.zeros((n_a, bp, d), F32)
    y_p, wkv_p, shift_p, kv_p, _ = _trunk(
        x_prompt.reshape(bp * tp, d), jnp.arange(tp, dtype=jnp.int32), mem_kv_p_t, wkv_zero, shift_zero,
        None, W, bp, tp, chunk=64)
    bufs_p = _prompt_windows(kv_p, bp, tp)

    windows = (cache_win_g0, cache_win_g1, cache_win_g2)
    caches_t = [jnp.transpose(c, (0, 2, 3, 4, 1)).reshape(bs, 2 * COL_TILE, c.shape[1]) for c in windows]
    mem_kv_s_t = jnp.transpose(cache_mem, (0, 1, 3, 4, 5, 2)).reshape(depth, bs, 2 * MEM_WIDTH, m_tok)
    y_s, wkv_s, shift_s, _, bufs_s_t = _trunk(
        x_sample.reshape(bs * ts, d), PAST_LEN + jnp.arange(ts, dtype=jnp.int32), mem_kv_s_t, state_wkv,
        state_shift, caches_t, W, bs, ts, chunk=ts)
    bufs_s = [jnp.transpose(b.reshape(bs, 2, HEADS_PER_TILE, HEAD_DIM, c.shape[1]), (0, 4, 1, 2, 3))
              for b, c in zip(bufs_s_t, windows)]

    return (y_p.reshape(bp, tp, d), y_s.reshape(bs, ts, d), wkv_p, shift_p, bufs_p[0], bufs_p[1], bufs_p[2],
            jnp.transpose(mem_kv_p_t.reshape(depth, bp, 2, MEM_HEADS, HEAD_DIM, m_tok), (0, 1, 5, 2, 3, 4)),
            wkv_s, shift_s, bufs_s[0], bufs_s[1], bufs_s[2])
```
